```python
import jax
import jax.numpy as jnp
from jax import lax
import numpy as np

D_MODEL = 1024
BATCH = 4
SEQ = 4096
DEPTH = 4

GRID_W = 64
CTX_LEN = 256
HEAD_DIM = 64
EPS = 1e-6
MASK_VALUE = -1e30
ROPE_THETA = 10000.0
N_MOD = 9
D_FF = 256 * ((8 * D_MODEL // 3 + 255) // 256)
HG_WIDTH = D_MODEL // 4
HG_DK = 64
HG_HEADS = HG_WIDTH // HG_DK
HG_DV = HG_WIDTH // HG_HEADS
HG_CHUNK = 64
NA_WIDTH = 3 * D_MODEL // 8
NA_HEADS = NA_WIDTH // HEAD_DIM
NA_ROWS = 8
NA_COLS = 16
SW_WIDTH = D_MODEL - HG_WIDTH - NA_WIDTH
SW_HEADS = SW_WIDTH // HEAD_DIM
SW_KV_HEADS = 2
SW_KV_WIDTH = SW_KV_HEADS * HEAD_DIM
SW_WINDOW = 128
SW_BLOCK = 128
MIX_WIDTH = HG_WIDTH + NA_WIDTH + SW_WIDTH
IN_SPLITS = (HG_WIDTH,) * 5 + (NA_WIDTH,) * 3 + (SW_WIDTH, SW_KV_WIDTH, SW_KV_WIDTH)
IN_WIDTH = sum(IN_SPLITS)

kernel_name = 'hybrid_hgrn2_natten_swa_prefix_dit'


def _rms_norm(x, g):
    xf = x.astype(jnp.float32)
    y = xf * lax.rsqrt(jnp.mean(xf * xf, axis=-1, keepdims=True) + EPS)
    return y.astype(x.dtype) * g


def _modulate(x, g, shift, scale):
    return _rms_norm(x, g) * (1 + scale) + shift


def _swiglu(h, w1, w3, w2):
    return (jax.nn.silu(h @ w1) * (h @ w3)) @ w2


def _softmax32(s):
    return jax.nn.softmax(s.astype(jnp.float32), axis=-1)


def _heads(a, n):
    b, t, _ = a.shape
    return a.reshape(b, t, n, -1).transpose(0, 2, 1, 3)


def _merge(a):
    b, h, t, d = a.shape
    return a.transpose(0, 2, 1, 3).reshape(b, t, h * d)


def _split_cols(p):
    return jnp.split(p, [int(s) for s in np.cumsum(IN_SPLITS)[:-1]], axis=-1)


def _axial_rope(x):
    t = x.shape[2]
    pos = jnp.arange(t)
    pos = jnp.stack([pos // GRID_W, pos % GRID_W], axis=-1).astype(jnp.float32)
    nf = HEAD_DIM // 4
    inv = ROPE_THETA ** (-jnp.arange(nf, dtype=jnp.float32) / nf)
    ang = pos[:, :, None] * inv
    cos, sin = jnp.cos(ang), jnp.sin(ang)
    xs = x.astype(jnp.float32).reshape(*x.shape[:-1], 2, 2, nf)
    x1, x2 = xs[..., 0, :], xs[..., 1, :]
    out = jnp.stack([x1 * cos - x2 * sin, x1 * sin + x2 * cos], axis=-2)
    return out.reshape(x.shape).astype(x.dtype)


def _context_attention(q, k, v, sink):
    s = jnp.einsum('bhgqd,bhkd->bhgqk', q, k).astype(jnp.float32) * q.shape[-1] ** -0.5
    n = s.shape[-1]
    if sink is not None:
        s = jnp.concatenate([s, jnp.broadcast_to(sink[None, :, :, None, None], s.shape[:-1] + (1,))], axis=-1)
    p = _softmax32(s)[..., :n].astype(v.dtype)
    return jnp.einsum('bhgqk,bhkd->bhgqd', p, v)


def _hgrn_gates(z, lb):
    z = z.astype(jnp.float32)
    log_f = jnp.log(lb + (1.0 - lb) * jax.nn.sigmoid(z))
    k = (1.0 - lb) * jax.nn.sigmoid(-z)
    return log_f, k


def _gla_chunk_scan(q, k, v, log_f, s0, with_output):
    b, h, t, _ = q.shape
    n = t // HG_CHUNK

    def blocks(a):
        return jnp.moveaxis(a.reshape(b, h, n, HG_CHUNK, a.shape[-1]), 2, 0)

    lower = jnp.tril(jnp.ones((HG_CHUNK, HG_CHUNK), bool))

    def step(s, inp):
        qc, kc, vc, gc = inp
        cum = jnp.cumsum(gc, axis=2)
        last = cum[:, :, -1]
        s_new = jnp.exp(last)[..., None] * s + jnp.einsum('bhsk,bhsv->bhkv', kc * jnp.exp(last[:, :, None] - cum), vc)
        if not with_output:
            return s_new, None
        o_inter = jnp.einsum('bhtk,bhkv->bhtv', qc * jnp.exp(cum), s)
        rel = jnp.where(lower[:, :, None], cum[:, :, :, None, :] - cum[:, :, None, :, :], MASK_VALUE)
        a = jnp.einsum('bhtk,bhtsk,bhsk->bhts', qc, jnp.exp(rel), kc)
        return s_new, o_inter + jnp.einsum('bhts,bhsv->bhtv', a, vc)

    s, o = lax.scan(step, s0, (blocks(q), blocks(k), blocks(v), blocks(log_f)))
    if with_output:
        o = jnp.moveaxis(o, 0, 2).reshape(b, h, t, -1)
    return s, o


def _hgrn_readout(o, g, norm_g):
    o = o * lax.rsqrt(jnp.mean(o * o, axis=-1, keepdims=True) + EPS)
    o = _merge(o) * norm_g.astype(jnp.float32)
    return (o * jax.nn.silu(g.astype(jnp.float32))).astype(g.dtype)


def _hgrn2_mixer(lat, ctx, lb, norm_g, need_ctx):
    def prep(parts):
        q, z_f, z_b, i, g = parts
        lf_f, k_f = _hgrn_gates(z_f, lb[0])
        lf_b, k_b = _hgrn_gates(z_b, lb[1])
        hd = lambda a: _heads(a, HG_HEADS)
        return (hd(jax.nn.silu(q.astype(jnp.float32))), hd(i.astype(jnp.float32)),
                hd(lf_f), hd(k_f), hd(lf_b), hd(k_b), g)

    def flip(a):
        return a[:, :, ::-1]

    def bidir(parts, s_f, s_b, with_output):
        q, v, lf_f, k_f, lf_b, k_b, g = prep(parts)
        s_f, o_f = _gla_chunk_scan(q, k_f, v, lf_f, s_f, with_output)
        s_b, o_b = _gla_chunk_scan(flip(q), flip(k_b), flip(v), flip(lf_b), s_b, with_output)
        out = _hgrn_readout(o_f + flip(o_b), g, norm_g) if with_output else None
        return s_f, s_b, out

    s0 = jnp.zeros((lat[0].shape[0], HG_HEADS, HG_DK, HG_DV), jnp.float32)
    s_f, s_b, out_ctx = bidir(ctx, s0, s0, need_ctx)
    _, _, out_lat = bidir(lat, s_f, s_b, True)
    return out_lat, out_ctx


def _neighbourhood_attention(q, k, v, qc, kc, vc, rpb, need_ctx):
    b, h, t, dh = q.shape
    rows = t // GRID_W
    wr = min(NA_ROWS, rows)
    n_keys = wr * GRID_W
    scale = dh ** -0.5
    r = jnp.arange(rows)
    row_idx = jnp.clip(r - wr // 2, 0, rows - wr)[:, None] + jnp.arange(wr)[None, :]

    def gather_rows(a):
        return a.reshape(b, h, rows, GRID_W, dh)[:, :, row_idx].reshape(b, h, rows, n_keys, dh)

    kg, vg = gather_rows(k), gather_rows(v)
    qg = q.reshape(b, h, rows, GRID_W, dh)
    col = jnp.arange(GRID_W)
    c0 = jnp.clip(col - NA_COLS // 2, 0, GRID_W - NA_COLS)
    col_ok = (col[None, :] >= c0[:, None]) & (col[None, :] < c0[:, None] + NA_COLS)
    d_row = row_idx - r[:, None]
    d_col = jnp.clip(col[None, :] - col[:, None], 1 - NA_COLS, NA_COLS - 1)
    bias = rpb[:, d_row[:, None, :, None] + NA_ROWS - 1, d_col[None, :, None, :] + NA_COLS - 1].astype(jnp.float32)
    bias = jnp.where(col_ok[None, None, :, None, :], bias, MASK_VALUE).reshape(h, rows, GRID_W, n_keys)
    s_lat = jnp.einsum('bhrqd,bhrkd->bhrqk', qg, kg).astype(jnp.float32) * scale + bias
    s_ctx = jnp.einsum('bhrqd,bhld->bhrql', qg, kc).astype(jnp.float32) * scale
    p = _softmax32(jnp.concatenate([s_lat, s_ctx], axis=-1)).astype(v.dtype)
    o = (jnp.einsum('bhrqk,bhrkd->bhrqd', p[..., :n_keys], vg)
         + jnp.einsum('bhrql,bhld->bhrqd', p[..., n_keys:], vc))
    o = o.reshape(b, h, t, dh)
    oc = _context_attention(qc[:, :, None], kc, vc, None)[:, :, 0] if need_ctx else None
    return o, oc


def _sliding_window_attention(q, k, v, qc, kc, vc, sink, need_ctx):
    b, hq, t, dh = q.shape
    hkv = k.shape[1]
    g = hq // hkv
    n = t // SW_BLOCK
    nk = 3 * SW_BLOCK
    scale = dh ** -0.5
    sink_g = sink.reshape(hkv, g).astype(jnp.float32)
    qb = q.reshape(b, hkv, g, n, SW_BLOCK, dh)

    def band(a):
        ap = jnp.pad(a, ((0, 0), (0, 0), (SW_BLOCK, SW_BLOCK), (0, 0))).reshape(b, hkv, n + 2, SW_BLOCK, dh)
        return jnp.concatenate([ap[:, :, :-2], ap[:, :, 1:-1], ap[:, :, 2:]], axis=3)

    kb, vb = band(k), band(v)
    blk = jnp.arange(n)[:, None]
    q_pos = blk * SW_BLOCK + jnp.arange(SW_BLOCK)[None, :]
    k_pos = (blk - 1) * SW_BLOCK + jnp.arange(nk)[None, :]
    ok = ((jnp.abs(q_pos[:, :, None] - k_pos[:, None, :]) <= SW_WINDOW)
          & (k_pos[:, None, :] >= 0) & (k_pos[:, None, :] < t))
    s_lat = jnp.where(ok, jnp.einsum('bhgnqd,bhnkd->bhgnqk', qb, kb).astype(jnp.float32) * scale, MASK_VALUE)
    s_ctx = jnp.einsum('bhgnqd,bhld->bhgnql', qb, kc).astype(jnp.float32) * scale
    s_sink = jnp.broadcast_to(sink_g[None, :, :, None, None, None], s_ctx.shape[:-1] + (1,))
    p = _softmax32(jnp.concatenate([s_lat, s_ctx, s_sink], axis=-1)).astype(v.dtype)
    n_ctx = kc.shape[2]
    o = (jnp.einsum('bhgnqk,bhnkd->bhgnqd', p[..., :nk], vb)
         + jnp.einsum('bhgnql,bhld->bhgnqd', p[..., nk:nk + n_ctx], vc))
    o = o.reshape(b, hq, t, dh)
    oc = _context_attention(qc.reshape(b, hkv, g, -1, dh), kc, vc, sink_g).reshape(b, hq, -1, dh) if need_ctx else None
    return o, oc


def _mixer_block(hx, hc, w_in, lb, hg_norm_g, rpb, sink, need_ctx):
    px = _split_cols(hx @ w_in)
    pc = _split_cols(hc @ w_in)
    o_hg, oc_hg = _hgrn2_mixer(px[0:5], pc[0:5], lb, hg_norm_g, need_ctx)
    na_lat = [_heads(a, NA_HEADS) for a in px[5:8]]
    na_ctx = [_heads(a, NA_HEADS) for a in pc[5:8]]
    o_na, oc_na = _neighbourhood_attention(*na_lat, *na_ctx, rpb, need_ctx)
    sw_q = _axial_rope(_heads(px[8], SW_HEADS))
    sw_k = _axial_rope(_heads(px[9], SW_KV_HEADS))
    sw_v = _heads(px[10], SW_KV_HEADS)
    o_sw, oc_sw = _sliding_window_attention(sw_q, sw_k, sw_v, _heads(pc[8], SW_HEADS), _heads(pc[9], SW_KV_HEADS),
                                            _heads(pc[10], SW_KV_HEADS), sink, need_ctx)
    out_lat = jnp.concatenate([o_hg, _merge(o_na), _merge(o_sw)], axis=-1)
    out_ctx = jnp.concatenate([oc_hg, _merge(oc_na), _merge(oc_sw)], axis=-1) if need_ctx else None
    return out_lat, out_ctx


def setup_inputs(seed: int = 0) -> dict:
    key = jax.random.key(seed)
    ks = jax.random.split(key, 17)
    f32 = jnp.float32

    def nrm(k, shape, scale):
        return jax.random.normal(k, shape, f32) * scale

    return {
        'x': nrm(ks[0], (BATCH, SEQ, D_MODEL), 1.0),
        'c': nrm(ks[1], (BATCH, D_MODEL), 1.0),
        'ctx': nrm(ks[2], (BATCH, CTX_LEN, D_MODEL), 1.0),
        'c_ctx': nrm(ks[3], (D_MODEL,), 1.0),
        'ada_w': nrm(ks[4], (DEPTH, D_MODEL, N_MOD * D_MODEL), 0.5 * D_MODEL ** -0.5),
        'ada_b': nrm(ks[5], (DEPTH, N_MOD * D_MODEL), 0.02),
        'norm_g': 1.0 + nrm(ks[6], (DEPTH, 3, D_MODEL), 0.02),
        'ffn_w1': nrm(ks[7], (DEPTH, 2, D_MODEL, D_FF), D_MODEL ** -0.5),
        'ffn_w3': nrm(ks[8], (DEPTH, 2, D_MODEL, D_FF), D_MODEL ** -0.5),
        'ffn_w2': nrm(ks[9], (DEPTH, 2, D_FF, D_MODEL), D_FF ** -0.5),
        'w_in': nrm(ks[10], (DEPTH, D_MODEL, IN_WIDTH), D_MODEL ** -0.5),
        'w_out': nrm(ks[11], (DEPTH, MIX_WIDTH, D_MODEL), MIX_WIDTH ** -0.5),
        'hg_lb_logits': nrm(ks[12], (DEPTH, 2, HG_WIDTH), 0.5),
        'hg_norm_g': 1.0 + nrm(ks[13], (DEPTH, HG_WIDTH), 0.02),
        'na_rpb': nrm(ks[14], (DEPTH, NA_HEADS, 2 * NA_ROWS - 1, 2 * NA_COLS - 1), 0.1),
        'sw_sink': nrm(ks[15], (DEPTH, SW_HEADS), 0.5),
        'final_g': 1.0 + nrm(ks[16], (D_MODEL,), 0.02),
    }


def reference(x, c, ctx, c_ctx, ada_w, ada_b, norm_g, ffn_w1, ffn_w3, ffn_w2, w_in, w_out,
              hg_lb_logits, hg_norm_g, na_rpb, sw_sink, final_g):
    lb_soft = jax.nn.softmax(hg_lb_logits.astype(jnp.float32), axis=0)
    lower_bounds = jnp.cumsum(lb_soft, axis=0) - lb_soft[0]
    s_c = jax.nn.silu(c)
    s_cc = jax.nn.silu(c_ctx)
    h = ctx
    for l in range(DEPTH):
        need_ctx = l < DEPTH - 1
        m = jnp.split((s_c @ ada_w[l] + ada_b[l])[:, None, :], N_MOD, axis=-1)
        mc = jnp.split(s_cc @ ada_w[l] + ada_b[l], N_MOD, axis=-1)
        x = x + 0.5 * m[2] * _swiglu(_modulate(x, norm_g[l, 0], m[0], m[1]), ffn_w1[l, 0], ffn_w3[l, 0], ffn_w2[l, 0])
        h = h + 0.5 * mc[2] * _swiglu(_modulate(h, norm_g[l, 0], mc[0], mc[1]), ffn_w1[l, 0], ffn_w3[l, 0], ffn_w2[l, 0])
        o_lat, o_ctx = _mixer_block(_modulate(x, norm_g[l, 1], m[3], m[4]), _modulate(h, norm_g[l, 1], mc[3], mc[4]),
                                    w_in[l], lower_bounds[l], hg_norm_g[l], na_rpb[l], sw_sink[l], need_ctx)
        x = x + m[5] * (o_lat @ w_out[l])
        x = x + 0.5 * m[8] * _swiglu(_modulate(x, norm_g[l, 2], m[6], m[7]), ffn_w1[l, 1], ffn_w3[l, 1], ffn_w2[l, 1])
        if need_ctx:
            h = h + mc[5] * (o_ctx @ w_out[l])
            h = h + 0.5 * mc[8] * _swiglu(_modulate(h, norm_g[l, 2], mc[6], mc[7]), ffn_w1[l, 1], ffn_w3[l, 1], ffn_w2[l, 1])
    return _rms_norm(x, final_g)
```

```python
import functools

import jax
import jax.numpy as jnp
import numpy as np
from jax import lax
from jax.experimental import pallas as pl
from jax.experimental.pallas import tpu as pltpu

F32 = jnp.float32
BF16 = jnp.bfloat16

D_MODEL = 1024
BATCH = 4
SEQ = 4096
DEPTH = 4
GRID_W = 64
CTX_LEN = 256
HEAD_DIM = 64
EPS = 1e-6
MASK_VALUE = -1e30
ROPE_THETA = 10000.0
N_MOD = 9
D_FF = 2816
HG_WIDTH = 256
HG_HEADS = 4
NA_WIDTH = 384
NA_HEADS = 6
NA_ROWS = 8
NA_COLS = 16
SW_WIDTH = 384
SW_HEADS = 6
SW_KV_WIDTH = 128
SW_WINDOW = 128
SW_BLOCK = 128
IN_WIDTH = 3072

LANES = 128
VMEM_LIMIT_BYTES = 56 * 1024 * 1024

N_LAT = BATCH * SEQ
N_CTX = BATCH * CTX_LEN
N_TOK = N_LAT + N_CTX
SEQ_ALL = SEQ + CTX_LEN
TM_PROJ = 256
TM_FFN = 512
TF_FFN = D_FF // 2
ROWS_PER_STEP = TM_PROJ // GRID_W
ATT_STEPS = SEQ // TM_PROJ
HG_CHUNK = 128
HG_SUB = 16
N_CHUNK_LAT = SEQ // HG_CHUNK
N_CHUNK_CTX = CTX_LEN // HG_CHUNK
N_CHUNK = N_CHUNK_LAT + N_CHUNK_CTX
SW_PERM = (0, 3, 1, 4, 2, 5)
NA_KEYS = NA_ROWS * GRID_W
NA_CASES = 8
SW_KEYS = 3 * SW_BLOCK


def _cparams(sem):
    return pltpu.CompilerParams(dimension_semantics=sem, vmem_limit_bytes=VMEM_LIMIT_BYTES)


def _silu(a):
    return a * jax.nn.sigmoid(a)


def _dot(a, b):
    return jnp.dot(a, b, preferred_element_type=F32)


def _dot_nt(a, b):
    return lax.dot_general(a, b, (((1,), (1,)), ((), ())), preferred_element_type=F32)


def _modulated_norm(x, g, shift, scale):
    ms = jnp.mean(x * x, axis=-1, keepdims=True)
    return (x * lax.rsqrt(ms + EPS)) * g * (1.0 + scale) + shift


ADA_TN = 1536


def _ada_kernel(c_ref, w_ref, b_ref, o_ref):
    s = _silu(c_ref[...]).astype(BF16)
    o_ref[...] = _dot(s, w_ref[...].astype(BF16)) + b_ref[...]


def _ada_call(c8, ada_w, ada_b):
    n_out = N_MOD * D_MODEL
    return pl.pallas_call(
        _ada_kernel,
        grid=(DEPTH, n_out // ADA_TN),
        in_specs=[
            pl.BlockSpec((8, D_MODEL), lambda l, j: (0, 0)),
            pl.BlockSpec((None, D_MODEL, ADA_TN), lambda l, j: (l, 0, j)),
            pl.BlockSpec((None, 1, ADA_TN), lambda l, j: (l, 0, j)),
        ],
        out_specs=pl.BlockSpec((None, 8, ADA_TN), lambda l, j: (l, 0, j)),
        out_shape=jax.ShapeDtypeStruct((DEPTH, 8, n_out), F32),
        compiler_params=_cparams(("parallel", "parallel")),
        name="ada_mod",
    )(c8, ada_w, ada_b.reshape(DEPTH, 1, n_out))


def _ffn_kernel(x_ref, mod_ref, g_ref, w1_ref, w3_ref, w2_ref, fg_ref, o_ref, h_scr, acc_scr, *, mod0, final):
    k = pl.program_id(1)

    @pl.when(k == 0)
    def _():
        h = _modulated_norm(x_ref[...], g_ref[...], mod_ref[mod0:mod0 + 1, :], mod_ref[mod0 + 1:mod0 + 2, :])
        h_scr[...] = h.astype(BF16)

    h = h_scr[...]
    a = _silu(_dot(h, w1_ref[...])) * _dot(h, w3_ref[...])
    part = _dot(a.astype(BF16), w2_ref[...])

    @pl.when(k == 0)
    def _():
        acc_scr[...] = part

    @pl.when(k > 0)
    def _():
        acc_scr[...] += part

    @pl.when(k == pl.num_programs(1) - 1)
    def _():
        y = x_ref[...] + (0.5 * mod_ref[mod0 + 2:mod0 + 3, :]) * acc_scr[...]
        if final:
            ms = jnp.mean(y * y, axis=-1, keepdims=True)
            y = (y * lax.rsqrt(ms + EPS)) * fg_ref[...]
        o_ref[...] = y


def _ffn_call(x, mods, g, w1, w3, w2, final_g, *, mod0, latent_only, final):
    n_rows = N_LAT if latent_only else N_TOK
    tiles_per_batch = SEQ // TM_FFN

    def mod_idx(i, k):
        return (jnp.where(i < BATCH * tiles_per_batch, i // tiles_per_batch, BATCH), 0, 0)

    return pl.pallas_call(
        functools.partial(_ffn_kernel, mod0=mod0, final=final),
        grid=(n_rows // TM_FFN, D_FF // TF_FFN),
        in_specs=[
            pl.BlockSpec((TM_FFN, D_MODEL), lambda i, k: (i, 0)),
            pl.BlockSpec((None, N_MOD, D_MODEL), mod_idx),
            pl.BlockSpec((1, D_MODEL), lambda i, k: (0, 0)),
            pl.BlockSpec((D_MODEL, TF_FFN), lambda i, k: (0, k)),
            pl.BlockSpec((D_MODEL, TF_FFN), lambda i, k: (0, k)),
            pl.BlockSpec((TF_FFN, D_MODEL), lambda i, k: (k, 0)),
            pl.BlockSpec((1, D_MODEL), lambda i, k: (0, 0)),
        ],
        out_specs=pl.BlockSpec((TM_FFN, D_MODEL), lambda i, k: (i, 0)),
        out_shape=jax.ShapeDtypeStruct((n_rows, D_MODEL), F32),
        scratch_shapes=[pltpu.VMEM((TM_FFN, D_MODEL), BF16), pltpu.VMEM((TM_FFN, D_MODEL), F32)],
        compiler_params=_cparams(("parallel", "arbitrary")),
        name="ffn",
    )(x, mods, g, w1, w3, w2, final_g)


def _tile_batch(i):
    tiles_per_batch = SEQ // TM_PROJ
    return jnp.where(i < BATCH * tiles_per_batch, i // tiles_per_batch, i - BATCH * tiles_per_batch)


def _tile_pos(i):
    tiles_per_batch = SEQ // TM_PROJ
    return jnp.where(i < BATCH * tiles_per_batch, i % tiles_per_batch, tiles_per_batch)


def _tile_mod(i):
    tiles_per_batch = SEQ // TM_PROJ
    return jnp.where(i < BATCH * tiles_per_batch, i // tiles_per_batch, BATCH)


def _rope(z, cos, sin_signed, first_of_pair):
    partner = jnp.where(first_of_pair, pltpu.roll(z, LANES - 16, 1), pltpu.roll(z, 16, 1))
    return z * cos + partner * sin_signed


def _inproj_kernel(x_ref, mod_ref, g_ref, w_ref, lb_ref, cos_ref, sin_ref,
                   qv_ref, ff_ref, fb_ref, sg_ref, at_ref):
    h = _modulated_norm(x_ref[...], g_ref[...], mod_ref[3:4, :], mod_ref[4:5, :]).astype(BF16)
    W = HG_WIDTH

    def proj(lo, hi):
        return _dot(h, w_ref[:, lo:hi])

    qv_ref[:, 0:W] = _silu(proj(0, W))
    qv_ref[:, W:2 * W] = proj(3 * W, 4 * W)
    for d, dst in ((0, ff_ref), (1, fb_ref)):
        z = proj((1 + d) * W, (2 + d) * W)
        lb = lb_ref[d:d + 1, :]
        dst[:, 0:W] = jnp.log(lb + (1.0 - lb) * jax.nn.sigmoid(z))
        dst[:, W:2 * W] = (1.0 - lb) * jax.nn.sigmoid(-z)
    sg_ref[...] = _silu(proj(4 * W, 5 * W))

    scale = HEAD_DIM ** -0.5
    na0 = 5 * W
    at_ref[:, 0:NA_WIDTH] = (proj(na0, na0 + NA_WIDTH) * scale).astype(BF16)
    at_ref[:, NA_WIDTH:3 * NA_WIDTH] = proj(na0 + NA_WIDTH, na0 + 3 * NA_WIDTH).astype(BF16)
    sw0 = na0 + 3 * NA_WIDTH
    cos = cos_ref[...]
    sin = sin_ref[...]
    lane = lax.broadcasted_iota(jnp.int32, (TM_PROJ, LANES), 1)
    first = (lane % 32) < 16
    for j in range((SW_WIDTH + SW_KV_WIDTH) // LANES):
        z = _rope(proj(sw0 + j * LANES, sw0 + (j + 1) * LANES), cos, sin, first)
        if j < SW_WIDTH // LANES:
            z = z * scale
        at_ref[:, 3 * NA_WIDTH + j * LANES:3 * NA_WIDTH + (j + 1) * LANES] = z.astype(BF16)
    v0 = sw0 + SW_WIDTH + SW_KV_WIDTH
    at_ref[:, 3 * NA_WIDTH + SW_WIDTH + SW_KV_WIDTH:] = proj(v0, v0 + SW_KV_WIDTH).astype(BF16)


AT_WIDTH = 3 * NA_WIDTH + SW_WIDTH + 2 * SW_KV_WIDTH


def _inproj_call(x, mods, g, w_in, lb, cos_t, sin_t):
    def out_spec(c):
        return pl.BlockSpec((None, TM_PROJ, c), lambda i: (_tile_batch(i), _tile_pos(i), 0))

    def out_shape(c, dt):
        return jax.ShapeDtypeStruct((BATCH, SEQ_ALL, c), dt)

    return pl.pallas_call(
        _inproj_kernel,
        grid=(N_TOK // TM_PROJ,),
        in_specs=[
            pl.BlockSpec((TM_PROJ, D_MODEL), lambda i: (i, 0)),
            pl.BlockSpec((None, N_MOD, D_MODEL), lambda i: (_tile_mod(i), 0, 0)),
            pl.BlockSpec((1, D_MODEL), lambda i: (0, 0)),
            pl.BlockSpec((D_MODEL, IN_WIDTH), lambda i: (0, 0)),
            pl.BlockSpec((2, HG_WIDTH), lambda i: (0, 0)),
            pl.BlockSpec((TM_PROJ, LANES), lambda i: (_tile_pos(i), 0)),
            pl.BlockSpec((TM_PROJ, LANES), lambda i: (_tile_pos(i), 0)),
        ],
        out_specs=[out_spec(2 * HG_WIDTH), out_spec(2 * HG_WIDTH), out_spec(2 * HG_WIDTH),
                   out_spec(HG_WIDTH), out_spec(AT_WIDTH)],
        out_shape=[out_shape(2 * HG_WIDTH, F32), out_shape(2 * HG_WIDTH, F32), out_shape(2 * HG_WIDTH, F32),
                   out_shape(HG_WIDTH, F32), out_shape(AT_WIDTH, BF16)],
        compiler_params=_cparams(("parallel",)),
        name="in_proj",
    )(x, mods, g, w_in, lb, cos_t, sin_t)


N_SUB = HG_CHUNK // HG_SUB
N_PAIR = HG_HEADS // 2


def _hgrn_consts():
    t = np.arange(HG_CHUNK)
    same = (t[:, None] // HG_SUB) == (t[None, :] // HG_SUB)
    lower = same & (t[None, :] <= t[:, None])
    upper = same & (t[None, :] >= t[:, None])
    tri = np.stack([np.concatenate([lower, same], 0), np.concatenate([upper, same], 0)]).astype(np.float32)
    d = np.arange(LANES)
    head_blocks = ((d[:, None] // HEAD_DIM) == (d[None, :] // HEAD_DIM)).astype(np.float32)
    return jnp.asarray(tri, BF16), jnp.asarray(head_blocks, BF16)


def _split3(a):
    hi = a.astype(BF16)
    r1 = a - hi.astype(F32)
    mid = r1.astype(BF16)
    lo = (r1 - mid.astype(F32)).astype(BF16)
    return jnp.concatenate([hi, mid, lo], axis=1)


def _bcast_sub(a, s):
    a3 = a.reshape(N_SUB, HG_SUB, LANES)
    return jnp.broadcast_to(a3[:, s:s + 1, :], (N_SUB, HG_SUB, LANES)).reshape(HG_CHUNK, LANES)


def _hgrn_group(qs, v, gl, kk, tri, ones_blk, st_ref, direction):
    c = _dot(tri, _split3(gl))
    c = c[:, 0:LANES] + c[:, LANES:2 * LANES] + c[:, 2 * LANES:]
    cum = c[0:HG_CHUNK]
    tot = c[HG_CHUNK:]
    row = lax.broadcasted_iota(jnp.int32, (HG_CHUNK, LANES), 0)
    t_sub = row % HG_SUB
    blk = row // HG_SUB

    o = jnp.zeros((HG_CHUNK, LANES), F32)
    for s in range(HG_SUB):
        keep = (t_sub >= s) if direction == 0 else (t_sub <= s)
        w = jnp.where(keep, qs * jnp.exp(cum - _bcast_sub(cum, s)) * _bcast_sub(kk, s), 0.0)
        o = o + _dot(w.astype(BF16), ones_blk) * _bcast_sub(v, s)

    qd = (qs * jnp.exp(cum)).astype(BF16)
    kd = kk * jnp.exp(tot - cum)
    dec = jnp.exp(tot)
    vt = v.T.astype(BF16)
    st = st_ref[...]
    head_mask = ones_blk.astype(F32)
    inter = [None] * N_SUB
    order = range(N_SUB) if direction == 0 else range(N_SUB - 1, -1, -1)
    for j in order:
        lo = j * HG_SUB
        inter[j] = _dot_nt(qd[lo:lo + HG_SUB], st.astype(BF16))
        upd = _dot(vt, jnp.where(blk == j, kd, 0.0).astype(BF16))
        st = st * dec[lo:lo + 1, :] + upd * head_mask
    st_ref[...] = st
    return o + jnp.concatenate(inter, axis=0)


def _hgrn_kernel(qvf_ref, ff_ref, qvb_ref, fb_ref, tri_ref, ones_ref, of_ref, ob_ref, st_scr):
    @pl.when(pl.program_id(0) == 0)
    def _():
        st_scr[...] = jnp.zeros_like(st_scr)

    ones_blk = ones_ref[...]
    W = HG_WIDTH
    for direction, (qv_ref, f_ref, o_ref) in enumerate(((qvf_ref, ff_ref, of_ref), (qvb_ref, fb_ref, ob_ref))):
        tri = tri_ref[direction]
        for hp in range(N_PAIR):
            c0 = hp * LANES

            def body(b, carry, qv_ref=qv_ref, f_ref=f_ref, o_ref=o_ref, tri=tri, c0=c0, direction=direction, hp=hp):
                g = (direction * N_PAIR + hp) * BATCH + b
                o_ref[b, :, c0:c0 + LANES] = _hgrn_group(
                    qv_ref[b, :, c0:c0 + LANES], qv_ref[b, :, W + c0:W + c0 + LANES],
                    f_ref[b, :, c0:c0 + LANES], f_ref[b, :, W + c0:W + c0 + LANES],
                    tri, ones_blk, st_scr.at[g], direction)
                return carry

            lax.fori_loop(0, BATCH, body, 0)


def _hgrn_call(qv, ff, fb, tri, ones_blk):
    def chunked(a):
        return a.reshape(BATCH, N_CHUNK, HG_CHUNK, a.shape[-1])

    def fwd_idx(s):
        return jnp.where(s < N_CHUNK_CTX, N_CHUNK_LAT + s, s - N_CHUNK_CTX)

    def bwd_idx(s):
        return jnp.where(s < N_CHUNK_CTX, N_CHUNK - 1 - s, N_CHUNK - 1 - s)

    def spec(c, idx):
        return pl.BlockSpec((BATCH, None, HG_CHUNK, c), lambda s: (0, idx(s), 0, 0))

    out_sds = jax.ShapeDtypeStruct((BATCH, N_CHUNK, HG_CHUNK, HG_WIDTH), F32)
    o_f, o_b = pl.pallas_call(
        _hgrn_kernel,
        grid=(N_CHUNK,),
        in_specs=[
            spec(2 * HG_WIDTH, fwd_idx), spec(2 * HG_WIDTH, fwd_idx),
            spec(2 * HG_WIDTH, bwd_idx), spec(2 * HG_WIDTH, bwd_idx),
            pl.BlockSpec((2, 2 * HG_CHUNK, HG_CHUNK), lambda s: (0, 0, 0)),
            pl.BlockSpec((LANES, LANES), lambda s: (0, 0)),
        ],
        out_specs=[spec(HG_WIDTH, fwd_idx), spec(HG_WIDTH, bwd_idx)],
        out_shape=[out_sds, out_sds],
        scratch_shapes=[pltpu.VMEM((2 * N_PAIR * BATCH, LANES, LANES), F32)],
        compiler_params=_cparams(("arbitrary",)),
        name="hgrn2",
    )(chunked(qv), chunked(ff), chunked(qv), chunked(fb), tri, ones_blk)
    return o_f.reshape(BATCH, SEQ_ALL, HG_WIDTH), o_b.reshape(BATCH, SEQ_ALL, HG_WIDTH)


def _pair_queries(q):
    lane = lax.broadcasted_iota(jnp.int32, q.shape, 1)
    zero = jnp.zeros_like(q)
    return jnp.concatenate([jnp.where(lane < HEAD_DIM, q, zero), jnp.where(lane >= HEAD_DIM, q, zero)], axis=0)


def _pair_merge(o):
    m = o.shape[0] // 2
    lane = lax.broadcasted_iota(jnp.int32, (m, LANES), 1)
    return jnp.where(lane < HEAD_DIM, o[0:m], o[m:])


def _softmax_pv(scores, values, extra=None):
    m = None
    for s in scores:
        mx = jnp.max(s, axis=-1, keepdims=True)
        m = mx if m is None else jnp.maximum(m, mx)
    if extra is not None:
        m = jnp.maximum(m, extra)
    denom = jnp.exp(extra - m) if extra is not None else 0.0
    acc = None
    for s, v in zip(scores, values):
        e = jnp.exp(s - m)
        denom = denom + jnp.sum(e, axis=-1, keepdims=True)
        pv = _dot(e.astype(BF16), v)
        acc = pv if acc is None else acc + pv
    return acc / denom


def _na_kernel(q_ref, k_ref, v_ref, kc_ref, vc_ref, bias_ref, o_ref):
    j = pl.program_id(1)

    @pl.when(j < ATT_STEPS)
    def _():
        def body(rr, carry):
            r = j * ROWS_PER_STEP + rr
            start = jnp.clip(r - NA_ROWS // 2, 0, GRID_W - NA_ROWS)
            case = jnp.where(r < NA_ROWS // 2, r,
                             jnp.where(r <= GRID_W - NA_ROWS // 2, NA_ROWS // 2, r - (GRID_W - NA_ROWS)))
            k0 = pl.multiple_of(start * GRID_W, GRID_W)
            q0 = pl.multiple_of(rr * GRID_W, GRID_W)
            for p in range(NA_HEADS // 2):
                c0 = p * LANES
                q2 = _pair_queries(q_ref[pl.ds(q0, GRID_W), c0:c0 + LANES])
                bias = jnp.concatenate([bias_ref[case, 2 * p], bias_ref[case, 2 * p + 1]], axis=0)
                s_lat = _dot_nt(q2, k_ref[pl.ds(k0, NA_KEYS), c0:c0 + LANES]) + bias
                s_ctx = _dot_nt(q2, kc_ref[:, c0:c0 + LANES])
                o = _softmax_pv([s_lat, s_ctx], [v_ref[pl.ds(k0, NA_KEYS), c0:c0 + LANES], vc_ref[:, c0:c0 + LANES]])
                o_ref[pl.ds(q0, GRID_W), c0:c0 + LANES] = _pair_merge(o).astype(BF16)
            return carry

        lax.fori_loop(0, ROWS_PER_STEP, body, 0)

    @pl.when(j == ATT_STEPS)
    def _():
        for p in range(NA_HEADS // 2):
            c0 = p * LANES
            q2 = _pair_queries(q_ref[:, c0:c0 + LANES])
            o = _softmax_pv([_dot_nt(q2, kc_ref[:, c0:c0 + LANES])], [vc_ref[:, c0:c0 + LANES]])
            o_ref[:, c0:c0 + LANES] = _pair_merge(o).astype(BF16)


def _att_steps(need_ctx):
    return ATT_STEPS + 1 if need_ctx else ATT_STEPS


def _na_call(at, bias, need_ctx):
    w = NA_WIDTH
    return pl.pallas_call(
        _na_kernel,
        grid=(BATCH, _att_steps(need_ctx)),
        in_specs=[
            pl.BlockSpec((None, TM_PROJ, w), lambda b, j: (b, j, 0)),
            pl.BlockSpec((None, SEQ, w), lambda b, j: (b, 0, 1)),
            pl.BlockSpec((None, SEQ, w), lambda b, j: (b, 0, 2)),
            pl.BlockSpec((None, CTX_LEN, w), lambda b, j: (b, SEQ // CTX_LEN, 1)),
            pl.BlockSpec((None, CTX_LEN, w), lambda b, j: (b, SEQ // CTX_LEN, 2)),
            pl.BlockSpec((NA_CASES, NA_HEADS, GRID_W, NA_KEYS), lambda b, j: (0, 0, 0, 0)),
        ],
        out_specs=pl.BlockSpec((None, TM_PROJ, w), lambda b, j: (b, j, 0)),
        out_shape=jax.ShapeDtypeStruct((BATCH, SEQ_ALL, w), BF16),
        compiler_params=_cparams(("parallel", "arbitrary")),
        name="nbr_attn",
    )(at, at, at, at, at, bias)


def _na_bias_table(rpb):
    half = NA_ROWS // 2
    case_row = np.concatenate([np.arange(half), [half], np.arange(GRID_W - half + 1, GRID_W)])
    start = np.clip(case_row - half, 0, GRID_W - NA_ROWS)
    d_row = start[:, None] + np.arange(NA_ROWS)[None, :] - case_row[:, None]
    col = np.arange(GRID_W)
    c0 = np.clip(col - NA_COLS // 2, 0, GRID_W - NA_COLS)
    col_ok = (col[None, :] >= c0[:, None]) & (col[None, :] < c0[:, None] + NA_COLS)
    d_col = np.clip(col[None, :] - col[:, None], 1 - NA_COLS, NA_COLS - 1)
    ri = (d_row + NA_ROWS - 1)[:, None, :, None]
    ci = (d_col + NA_COLS - 1)[None, :, None, :]
    b = rpb[:, ri, ci].astype(F32)
    b = jnp.where(col_ok[None, None, :, None, :], b, MASK_VALUE)
    return b.transpose(1, 0, 2, 3, 4).reshape(NA_CASES, NA_HEADS, GRID_W, NA_KEYS)


def _sw_kernel(sink_ref, q_ref, k_ref, v_ref, kc_ref, vc_ref, o_ref):
    j = pl.program_id(1)
    n_pair = SW_HEADS // 2

    def sink_col(p, m):
        row = lax.broadcasted_iota(jnp.int32, (2 * m, 1), 0)
        return jnp.where(row < m, sink_ref[2 * p], sink_ref[2 * p + 1])

    @pl.when(j < ATT_STEPS)
    def _():
        for u in range(TM_PROJ // SW_BLOCK):
            n = j * (TM_PROJ // SW_BLOCK) + u
            start = jnp.clip(n * SW_BLOCK - SW_BLOCK, 0, SEQ - SW_KEYS)
            k0 = pl.multiple_of(start, SW_BLOCK)
            rel = (n * SW_BLOCK - start
                   + lax.broadcasted_iota(jnp.int32, (SW_BLOCK, SW_KEYS), 0)
                   - lax.broadcasted_iota(jnp.int32, (SW_BLOCK, SW_KEYS), 1))
            band = jnp.where(jnp.abs(rel) <= SW_WINDOW, 0.0, MASK_VALUE).astype(F32)
            band2 = jnp.concatenate([band, band], axis=0)
            kw = k_ref[pl.ds(k0, SW_KEYS), :]
            vw = v_ref[pl.ds(k0, SW_KEYS), :]
            for p in range(n_pair):
                c0 = p * LANES
                q2 = _pair_queries(q_ref[u * SW_BLOCK:(u + 1) * SW_BLOCK, c0:c0 + LANES])
                s_lat = _dot_nt(q2, kw) + band2
                s_ctx = _dot_nt(q2, kc_ref[...])
                o = _softmax_pv([s_lat, s_ctx], [vw, vc_ref[...]], extra=sink_col(p, SW_BLOCK))
                o_ref[u * SW_BLOCK:(u + 1) * SW_BLOCK, c0:c0 + LANES] = _pair_merge(o).astype(BF16)

    @pl.when(j == ATT_STEPS)
    def _():
        for p in range(n_pair):
            c0 = p * LANES
            q2 = _pair_queries(q_ref[:, c0:c0 + LANES])
            o = _softmax_pv([_dot_nt(q2, kc_ref[...])], [vc_ref[...]], extra=sink_col(p, TM_PROJ))
            o_ref[:, c0:c0 + LANES] = _pair_merge(o).astype(BF16)


def _sw_call(at, sink_perm, need_ctx):
    q_blk = 3 * NA_WIDTH // SW_WIDTH
    k_blk = (3 * NA_WIDTH + SW_WIDTH) // SW_KV_WIDTH
    grid_spec = pltpu.PrefetchScalarGridSpec(
        num_scalar_prefetch=1,
        grid=(BATCH, _att_steps(need_ctx)),
        in_specs=[
            pl.BlockSpec((None, TM_PROJ, SW_WIDTH), lambda b, j, s: (b, j, q_blk)),
            pl.BlockSpec((None, SEQ, SW_KV_WIDTH), lambda b, j, s: (b, 0, k_blk)),
            pl.BlockSpec((None, SEQ, SW_KV_WIDTH), lambda b, j, s: (b, 0, k_blk + 1)),
            pl.BlockSpec((None, CTX_LEN, SW_KV_WIDTH), lambda b, j, s: (b, SEQ // CTX_LEN, k_blk)),
            pl.BlockSpec((None, CTX_LEN, SW_KV_WIDTH), lambda b, j, s: (b, SEQ // CTX_LEN, k_blk + 1)),
        ],
        out_specs=pl.BlockSpec((None, TM_PROJ, SW_WIDTH), lambda b, j, s: (b, j, 0)),
    )
    return pl.pallas_call(
        _sw_kernel,
        grid_spec=grid_spec,
        out_shape=jax.ShapeDtypeStruct((BATCH, SEQ_ALL, SW_WIDTH), BF16),
        compiler_params=_cparams(("parallel", "arbitrary")),
        name="win_attn",
    )(sink_perm, at, at, at, at, at)


def _outproj_kernel(x_ref, mod_ref, of_ref, ob_ref, sg_ref, na_ref, sw_ref, w_ref, ng_ref, ones_ref, o_ref):
    o = of_ref[...] + ob_ref[...]
    ms = jnp.concatenate(
        [_dot((o[:, c:c + LANES] * o[:, c:c + LANES]).astype(BF16), ones_ref[...]) for c in range(0, HG_WIDTH, LANES)],
        axis=1) * (1.0 / HEAD_DIM)
    hg = (o * lax.rsqrt(ms + EPS)) * ng_ref[...] * sg_ref[...]
    y = _dot(hg.astype(BF16), w_ref[0:HG_WIDTH, :])
    y = y + _dot(na_ref[...], w_ref[HG_WIDTH:HG_WIDTH + NA_WIDTH, :])
    y = y + _dot(sw_ref[...], w_ref[HG_WIDTH + NA_WIDTH:, :])
    o_ref[...] = x_ref[...] + mod_ref[5:6, :] * y


def _outproj_call(x, mods, o_f, o_b, sg, o_na, o_sw, w_out, ng, ones_blk, latent_only):
    n_rows = N_LAT if latent_only else N_TOK

    def per_batch(c):
        return pl.BlockSpec((None, TM_PROJ, c), lambda i: (_tile_batch(i), _tile_pos(i), 0))

    return pl.pallas_call(
        _outproj_kernel,
        grid=(n_rows // TM_PROJ,),
        in_specs=[
            pl.BlockSpec((TM_PROJ, D_MODEL), lambda i: (i, 0)),
            pl.BlockSpec((None, N_MOD, D_MODEL), lambda i: (_tile_mod(i), 0, 0)),
            per_batch(HG_WIDTH), per_batch(HG_WIDTH), per_batch(HG_WIDTH),
            per_batch(NA_WIDTH), per_batch(SW_WIDTH),
            pl.BlockSpec((D_MODEL, D_MODEL), lambda i: (0, 0)),
            pl.BlockSpec((1, HG_WIDTH), lambda i: (0, 0)),
            pl.BlockSpec((LANES, LANES), lambda i: (0, 0)),
        ],
        out_specs=pl.BlockSpec((TM_PROJ, D_MODEL), lambda i: (i, 0)),
        out_shape=jax.ShapeDtypeStruct((n_rows, D_MODEL), F32),
        compiler_params=_cparams(("parallel",)),
        name="out_proj",
    )(x, mods, o_f, o_b, sg, o_na, o_sw, w_out, ng, ones_blk)


def _rope_tables():
    pos = jnp.arange(SEQ)
    pos = jnp.stack([pos // GRID_W, pos % GRID_W], axis=-1).astype(F32)
    nf = HEAD_DIM // 4
    inv = ROPE_THETA ** (-jnp.arange(nf, dtype=F32) / nf)
    ang = pos[:, :, None] * inv
    cos, sin = jnp.cos(ang), jnp.sin(ang)
    cos_h = jnp.stack([cos, cos], axis=2).reshape(SEQ, HEAD_DIM)
    sin_h = jnp.stack([-sin, sin], axis=2).reshape(SEQ, HEAD_DIM)
    reps = LANES // HEAD_DIM
    cos_t = jnp.concatenate([jnp.tile(cos_h, (1, reps)), jnp.ones((CTX_LEN, LANES), F32)], axis=0)
    sin_t = jnp.concatenate([jnp.tile(sin_h, (1, reps)), jnp.zeros((CTX_LEN, LANES), F32)], axis=0)
    return cos_t, sin_t


def _permute_heads(a, axis):
    shape = a.shape
    a = a.reshape(shape[:axis] + (SW_HEADS, HEAD_DIM) + shape[axis + 1:])
    a = jnp.take(a, jnp.asarray(SW_PERM), axis=axis)
    return a.reshape(shape)


def kernel(x, c, ctx, c_ctx, ada_w, ada_b, norm_g, ffn_w1, ffn_w3, ffn_w2, w_in, w_out,
           hg_lb_logits, hg_norm_g, na_rpb, sw_sink, final_g):
    lb_soft = jax.nn.softmax(hg_lb_logits.astype(F32), axis=0)
    lower_bounds = jnp.cumsum(lb_soft, axis=0) - lb_soft[0]
    sw_q0 = 5 * HG_WIDTH + 3 * NA_WIDTH
    w_in_p = jnp.concatenate(
        [w_in[:, :, :sw_q0], _permute_heads(w_in[:, :, sw_q0:sw_q0 + SW_WIDTH], 2), w_in[:, :, sw_q0 + SW_WIDTH:]],
        axis=2).astype(BF16)
    sw_o0 = HG_WIDTH + NA_WIDTH
    w_out_p = jnp.concatenate([w_out[:, :sw_o0], _permute_heads(w_out[:, sw_o0:], 1)], axis=1).astype(BF16)
    sink_p = jnp.take(sw_sink.astype(F32), jnp.asarray(SW_PERM), axis=1)
    w1 = ffn_w1.astype(BF16)
    w3 = ffn_w3.astype(BF16)
    w2 = ffn_w2.astype(BF16)
    cos_t, sin_t = _rope_tables()
    tri, ones_blk = _hgrn_consts()
    final_g2 = final_g.reshape(1, D_MODEL)

    c8 = jnp.concatenate([c, c_ctx[None, :], jnp.zeros((8 - BATCH - 1, D_MODEL), F32)], axis=0)
    mods_all = _ada_call(c8, ada_w, ada_b).reshape(DEPTH, 8, N_MOD, D_MODEL)

    xs = jnp.concatenate([x.reshape(N_LAT, D_MODEL), ctx.reshape(N_CTX, D_MODEL)], axis=0)
    for l in range(DEPTH):
        need_ctx = l < DEPTH - 1
        mods = mods_all[l]
        xs = _ffn_call(xs, mods, norm_g[l, 0:1], w1[l, 0], w3[l, 0], w2[l, 0], final_g2,
                       mod0=0, latent_only=False, final=False)
        qv, ff, fb, sg, at = _inproj_call(xs, mods, norm_g[l, 1:2], w_in_p[l], lower_bounds[l], cos_t, sin_t)
        o_f, o_b = _hgrn_call(qv, ff, fb, tri, ones_blk)
        o_na = _na_call(at, _na_bias_table(na_rpb[l]), need_ctx)
        o_sw = _sw_call(at, sink_p[l], need_ctx)
        xs = _outproj_call(xs, mods, o_f, o_b, sg, o_na, o_sw, w_out_p[l], hg_norm_g[l:l + 1], ones_blk,
                           latent_only=not need_ctx)
        xs = _ffn_call(xs, mods, norm_g[l, 2:3], w1[l, 1], w3[l, 1], w2[l, 1], final_g2,
                       mod0=6, latent_only=not need_ctx, final=not need_ctx)
    return xs.reshape(BATCH, SEQ, D_MODEL)
```

```python
import functools

import jax
import jax.numpy as jnp
import numpy as np
from jax import lax
from jax.experimental import pallas as pl
from jax.experimental.pallas import tpu as pltpu

F32 = jnp.float32
BF16 = jnp.bfloat16

D_MODEL = 1024
BATCH = 4
SEQ = 4096
DEPTH = 4
GRID_W = 64
CTX_LEN = 256
HEAD_DIM = 64
EPS = 1e-6
MASK_VALUE = -1e30
ROPE_THETA = 10000.0
N_MOD = 9
D_FF = 2816
HG_WIDTH = 256
HG_HEADS = 4
NA_WIDTH = 384
NA_HEADS = 6
NA_ROWS = 8
NA_COLS = 16
SW_WIDTH = 384
SW_HEADS = 6
SW_KV_WIDTH = 128
SW_WINDOW = 128
SW_BLOCK = 128
IN_WIDTH = 3072

LANES = 128
VMEM_LIMIT_BYTES = 56 * 1024 * 1024

N_LAT = BATCH * SEQ
N_CTX = BATCH * CTX_LEN
N_TOK = N_LAT + N_CTX
SEQ_ALL = SEQ + CTX_LEN
TM_PROJ = 256
TM_FFN = 512
TF_FFN = D_FF // 2
ROWS_PER_STEP = TM_PROJ // GRID_W
ATT_STEPS = SEQ // TM_PROJ
HG_CHUNK = 128
HG_SUB = 16
N_CHUNK_LAT = SEQ // HG_CHUNK
N_CHUNK_CTX = CTX_LEN // HG_CHUNK
N_CHUNK = N_CHUNK_LAT + N_CHUNK_CTX
SW_PERM = (0, 3, 1, 4, 2, 5)
NA_KEYS = NA_ROWS * GRID_W
NA_CASES = 8
SW_KEYS = 3 * SW_BLOCK


def _cparams(sem):
    return pltpu.CompilerParams(dimension_semantics=sem, vmem_limit_bytes=VMEM_LIMIT_BYTES)


def _silu(a):
    return a * jax.nn.sigmoid(a)


def _dot(a, b):
    return jnp.dot(a, b, preferred_element_type=F32)


def _dot_nt(a, b):
    return lax.dot_general(a, b, (((1,), (1,)), ((), ())), preferred_element_type=F32)


def _modulated_norm(x, g, shift, scale):
    ms = jnp.mean(x * x, axis=-1, keepdims=True)
    return (x * lax.rsqrt(ms + EPS)) * g * (1.0 + scale) + shift


ADA_TN = 1536


def _ada_kernel(c_ref, w_ref, b_ref, o_ref):
    s = _silu(c_ref[...]).astype(BF16)
    o_ref[...] = _dot(s, w_ref[...].astype(BF16)) + b_ref[...]


def _ada_call(c8, ada_w, ada_b):
    n_out = N_MOD * D_MODEL
    return pl.pallas_call(
        _ada_kernel,
        grid=(DEPTH, n_out // ADA_TN),
        in_specs=[
            pl.BlockSpec((8, D_MODEL), lambda l, j: (0, 0)),
            pl.BlockSpec((None, D_MODEL, ADA_TN), lambda l, j: (l, 0, j)),
            pl.BlockSpec((None, 1, ADA_TN), lambda l, j: (l, 0, j)),
        ],
        out_specs=pl.BlockSpec((None, 8, ADA_TN), lambda l, j: (l, 0, j)),
        out_shape=jax.ShapeDtypeStruct((DEPTH, 8, n_out), F32),
        compiler_params=_cparams(("parallel", "parallel")),
        name="ada_mod",
    )(c8, ada_w, ada_b.reshape(DEPTH, 1, n_out))


def _ffn_kernel(x_ref, mod_ref, g_ref, w1_ref, w3_ref, w2_ref, fg_ref, o_ref, h_scr, acc_scr, *, mod0, final):
    k = pl.program_id(1)

    @pl.when(k == 0)
    def _():
        h = _modulated_norm(x_ref[...], g_ref[...], mod_ref[mod0:mod0 + 1, :], mod_ref[mod0 + 1:mod0 + 2, :])
        h_scr[...] = h.astype(BF16)

    h = h_scr[...]
    a = _silu(_dot(h, w1_ref[...])) * _dot(h, w3_ref[...])
    part = _dot(a.astype(BF16), w2_ref[...])

    @pl.when(k == 0)
    def _():
        acc_scr[...] = part

    @pl.when(k > 0)
    def _():
        acc_scr[...] += part

    @pl.when(k == pl.num_programs(1) - 1)
    def _():
        y = x_ref[...] + (0.5 * mod_ref[mod0 + 2:mod0 + 3, :]) * acc_scr[...]
        if final:
            ms = jnp.mean(y * y, axis=-1, keepdims=True)
            y = (y * lax.rsqrt(ms + EPS)) * fg_ref[...]
        o_ref[...] = y


def _ffn_call(x, mods, g, w1, w3, w2, final_g, *, mod0, latent_only, final):
    n_rows = N_LAT if latent_only else N_TOK
    tiles_per_batch = SEQ // TM_FFN

    def mod_idx(i, k):
        return (jnp.where(i < BATCH * tiles_per_batch, i // tiles_per_batch, BATCH), 0, 0)

    return pl.pallas_call(
        functools.partial(_ffn_kernel, mod0=mod0, final=final),
        grid=(n_rows // TM_FFN, D_FF // TF_FFN),
        in_specs=[
            pl.BlockSpec((TM_FFN, D_MODEL), lambda i, k: (i, 0)),
            pl.BlockSpec((None, N_MOD, D_MODEL), mod_idx),
            pl.BlockSpec((1, D_MODEL), lambda i, k: (0, 0)),
            pl.BlockSpec((D_MODEL, TF_FFN), lambda i, k: (0, k)),
            pl.BlockSpec((D_MODEL, TF_FFN), lambda i, k: (0, k)),
            pl.BlockSpec((TF_FFN, D_MODEL), lambda i, k: (k, 0)),
            pl.BlockSpec((1, D_MODEL), lambda i, k: (0, 0)),
        ],
        out_specs=pl.BlockSpec((TM_FFN, D_MODEL), lambda i, k: (i, 0)),
        out_shape=jax.ShapeDtypeStruct((n_rows, D_MODEL), F32),
        scratch_shapes=[pltpu.VMEM((TM_FFN, D_MODEL), BF16), pltpu.VMEM((TM_FFN, D_MODEL), F32)],
        compiler_params=_cparams(("parallel", "arbitrary")),
        name="ffn",
    )(x, mods, g, w1, w3, w2, final_g)


def _tile_batch(i):
    tiles_per_batch = SEQ // TM_PROJ
    return jnp.where(i < BATCH * tiles_per_batch, i // tiles_per_batch, i - BATCH * tiles_per_batch)


def _tile_pos(i):
    tiles_per_batch = SEQ // TM_PROJ
    return jnp.where(i < BATCH * tiles_per_batch, i % tiles_per_batch, tiles_per_batch)


def _tile_mod(i):
    tiles_per_batch = SEQ // TM_PROJ
    return jnp.where(i < BATCH * tiles_per_batch, i // tiles_per_batch, BATCH)


def _rope(z, cos, sin_signed, first_of_pair):
    partner = jnp.where(first_of_pair, pltpu.roll(z, LANES - 16, 1), pltpu.roll(z, 16, 1))
    return z * cos + partner * sin_signed


def _inproj_kernel(x_ref, mod_ref, g_ref, w_ref, lb_ref, cos_ref, sin_ref,
                   qv_ref, ff_ref, fb_ref, sg_ref, at_ref):
    h = _modulated_norm(x_ref[...], g_ref[...], mod_ref[3:4, :], mod_ref[4:5, :]).astype(BF16)
    W = HG_WIDTH

    def proj(lo, hi):
        return _dot(h, w_ref[:, lo:hi])

    qv_ref[:, 0:W] = _silu(proj(0, W))
    qv_ref[:, W:2 * W] = proj(3 * W, 4 * W)
    for d, dst in ((0, ff_ref), (1, fb_ref)):
        z = proj((1 + d) * W, (2 + d) * W)
        lb = lb_ref[d:d + 1, :]
        dst[:, 0:W] = jnp.log(lb + (1.0 - lb) * jax.nn.sigmoid(z))
        dst[:, W:2 * W] = (1.0 - lb) * jax.nn.sigmoid(-z)
    sg_ref[...] = _silu(proj(4 * W, 5 * W))

    scale = HEAD_DIM ** -0.5
    na0 = 5 * W
    at_ref[:, 0:NA_WIDTH] = (proj(na0, na0 + NA_WIDTH) * scale).astype(BF16)
    at_ref[:, NA_WIDTH:3 * NA_WIDTH] = proj(na0 + NA_WIDTH, na0 + 3 * NA_WIDTH).astype(BF16)
    sw0 = na0 + 3 * NA_WIDTH
    cos = cos_ref[...]
    sin = sin_ref[...]
    lane = lax.broadcasted_iota(jnp.int32, (TM_PROJ, LANES), 1)
    first = (lane % 32) < 16
    for j in range((SW_WIDTH + SW_KV_WIDTH) // LANES):
        z = _rope(proj(sw0 + j * LANES, sw0 + (j + 1) * LANES), cos, sin, first)
        if j < SW_WIDTH // LANES:
            z = z * scale
        at_ref[:, 3 * NA_WIDTH + j * LANES:3 * NA_WIDTH + (j + 1) * LANES] = z.astype(BF16)
    v0 = sw0 + SW_WIDTH + SW_KV_WIDTH
    at_ref[:, 3 * NA_WIDTH + SW_WIDTH + SW_KV_WIDTH:] = proj(v0, v0 + SW_KV_WIDTH).astype(BF16)


AT_WIDTH = 3 * NA_WIDTH + SW_WIDTH + 2 * SW_KV_WIDTH


def _inproj_call(x, mods, g, w_in, lb, cos_t, sin_t):
    def out_spec(c):
        return pl.BlockSpec((None, TM_PROJ, c), lambda i: (_tile_batch(i), _tile_pos(i), 0))

    def out_shape(c, dt):
        return jax.ShapeDtypeStruct((BATCH, SEQ_ALL, c), dt)

    return pl.pallas_call(
        _inproj_kernel,
        grid=(N_TOK // TM_PROJ,),
        in_specs=[
            pl.BlockSpec((TM_PROJ, D_MODEL), lambda i: (i, 0)),
            pl.BlockSpec((None, N_MOD, D_MODEL), lambda i: (_tile_mod(i), 0, 0)),
            pl.BlockSpec((1, D_MODEL), lambda i: (0, 0)),
            pl.BlockSpec((D_MODEL, IN_WIDTH), lambda i: (0, 0)),
            pl.BlockSpec((2, HG_WIDTH), lambda i: (0, 0)),
            pl.BlockSpec((TM_PROJ, LANES), lambda i: (_tile_pos(i), 0)),
            pl.BlockSpec((TM_PROJ, LANES), lambda i: (_tile_pos(i), 0)),
        ],
        out_specs=[out_spec(2 * HG_WIDTH), out_spec(2 * HG_WIDTH), out_spec(2 * HG_WIDTH),
                   out_spec(HG_WIDTH), out_spec(AT_WIDTH)],
        out_shape=[out_shape(2 * HG_WIDTH, F32), out_shape(2 * HG_WIDTH, F32), out_shape(2 * HG_WIDTH, F32),
                   out_shape(HG_WIDTH, F32), out_shape(AT_WIDTH, BF16)],
        compiler_params=_cparams(("parallel",)),
        name="in_proj",
    )(x, mods, g, w_in, lb, cos_t, sin_t)


N_SUB = HG_CHUNK // HG_SUB
N_PAIR = HG_HEADS // 2
HG_MINI = HG_SUB // 2


def _hgrn_consts():
    t = np.arange(HG_CHUNK)
    same = (t[:, None] // HG_SUB) == (t[None, :] // HG_SUB)
    lower = same & (t[None, :] <= t[:, None])
    upper = same & (t[None, :] >= t[:, None])
    tri = np.stack([np.concatenate([lower, same], 0), np.concatenate([upper, same], 0)]).astype(np.float32)
    d = np.arange(LANES)
    head_blocks = ((d[:, None] // HEAD_DIM) == (d[None, :] // HEAD_DIM)).astype(np.float32)
    return jnp.asarray(tri, BF16), jnp.asarray(head_blocks, BF16)


def _split3(a):
    hi = a.astype(BF16)
    r1 = a - hi.astype(F32)
    mid = r1.astype(BF16)
    lo = (r1 - mid.astype(F32)).astype(BF16)
    return jnp.concatenate([hi, mid, lo], axis=1)


def _bcast_rows(a, s, block):
    n = HG_CHUNK // block
    a3 = a.reshape(n, block, LANES)
    return jnp.broadcast_to(a3[:, s:s + 1, :], (n, block, LANES)).reshape(HG_CHUNK, LANES)


def _hgrn_group(qs, v, gl, kk, tri, ones_blk, st, direction):
    fwd = direction == 0
    c = _dot(tri, _split3(gl))
    c = c[:, 0:LANES] + c[:, LANES:2 * LANES] + c[:, 2 * LANES:]
    cum = c[0:HG_CHUNK]
    tot = c[HG_CHUNK:]
    row = lax.broadcasted_iota(jnp.int32, (HG_CHUNK, LANES), 0)
    t_mini = row % HG_MINI
    blk = row // HG_SUB
    v16 = v.astype(BF16)

    o = jnp.zeros((HG_CHUNK, LANES), F32)
    for s in range(HG_MINI):
        keep = (t_mini >= s) if fwd else (t_mini <= s)
        w = jnp.where(keep, qs * jnp.exp(cum - _bcast_rows(cum, s, HG_MINI)) * _bcast_rows(kk, s, HG_MINI), 0.0)
        o = o + _dot(w.astype(BF16), ones_blk) * _bcast_rows(v, s, HG_MINI)

    later = ((row % HG_SUB) >= HG_MINI) if fwd else ((row % HG_SUB) < HG_MINI)
    edge = _bcast_rows(cum, HG_MINI - 1 if fwd else HG_MINI, HG_SUB)
    q_edge = jnp.where(later, qs * jnp.exp(jnp.minimum(cum - edge, 0.0)), 0.0)
    k_edge = jnp.where(later, 0.0, kk * jnp.exp(jnp.minimum(edge - cum, 0.0)))
    a = _dot_nt(_pair_queries(q_edge.astype(BF16)), k_edge.astype(BF16))
    q_blk = lax.broadcasted_iota(jnp.int32, (2 * HG_CHUNK, HG_CHUNK), 0) % HG_CHUNK // HG_SUB
    k_blk = lax.broadcasted_iota(jnp.int32, (2 * HG_CHUNK, HG_CHUNK), 1) // HG_SUB
    a = jnp.where(q_blk == k_blk, a, 0.0)
    o = o + _pair_merge(_dot(a.astype(BF16), v16))

    qd = qs * jnp.exp(cum)
    kd = kk * jnp.exp(tot - cum)
    dec = jnp.exp(tot)
    k_exp = jnp.concatenate([jnp.where(blk == j, kd, 0.0).astype(BF16) for j in range(N_SUB)], axis=1)
    upd = _dot(v.T.astype(BF16), k_exp)
    head_mask = ones_blk.astype(F32)
    before = [None] * N_SUB
    for j in (range(N_SUB) if fwd else range(N_SUB - 1, -1, -1)):
        before[j] = st.astype(BF16)
        st = st * dec[j * HG_SUB:j * HG_SUB + 1, :] + upd[:, j * LANES:(j + 1) * LANES] * head_mask
    q_exp = jnp.concatenate([jnp.where(blk == j, qd, 0.0).astype(BF16) for j in range(N_SUB)], axis=1)
    o = o + _dot_nt(q_exp, jnp.concatenate(before, axis=1))
    return o, st


def _hgrn_kernel(qvf_ref, ff_ref, qvb_ref, fb_ref, tri_ref, ones_ref, of_ref, ob_ref, st_scr):
    @pl.when(pl.program_id(0) == 0)
    def _():
        st_scr[...] = jnp.zeros_like(st_scr)

    ones_blk = ones_ref[...]
    W = HG_WIDTH
    for direction, (qv_ref, f_ref, o_ref) in enumerate(((qvf_ref, ff_ref, of_ref), (qvb_ref, fb_ref, ob_ref))):
        tri = tri_ref[direction]

        def body(b, carry, qv_ref=qv_ref, f_ref=f_ref, o_ref=o_ref, tri=tri, direction=direction):
            for hp in range(N_PAIR):
                c0 = hp * LANES
                g = (direction * N_PAIR + hp) * BATCH + b
                o, st = _hgrn_group(
                    qv_ref[b, :, c0:c0 + LANES], qv_ref[b, :, W + c0:W + c0 + LANES],
                    f_ref[b, :, c0:c0 + LANES], f_ref[b, :, W + c0:W + c0 + LANES],
                    tri, ones_blk, st_scr[g], direction)
                o_ref[b, :, c0:c0 + LANES] = o
                st_scr[g] = st
            return carry

        lax.fori_loop(0, BATCH, body, 0)


def _hgrn_call(qv, ff, fb, tri, ones_blk):
    def chunked(a):
        return a.reshape(BATCH, N_CHUNK, HG_CHUNK, a.shape[-1])

    def fwd_idx(s):
        return jnp.where(s < N_CHUNK_CTX, N_CHUNK_LAT + s, s - N_CHUNK_CTX)

    def bwd_idx(s):
        return jnp.where(s < N_CHUNK_CTX, N_CHUNK - 1 - s, N_CHUNK - 1 - s)

    def spec(c, idx):
        return pl.BlockSpec((BATCH, None, HG_CHUNK, c), lambda s: (0, idx(s), 0, 0))

    out_sds = jax.ShapeDtypeStruct((BATCH, N_CHUNK, HG_CHUNK, HG_WIDTH), F32)
    o_f, o_b = pl.pallas_call(
        _hgrn_kernel,
        grid=(N_CHUNK,),
        in_specs=[
            spec(2 * HG_WIDTH, fwd_idx), spec(2 * HG_WIDTH, fwd_idx),
            spec(2 * HG_WIDTH, bwd_idx), spec(2 * HG_WIDTH, bwd_idx),
            pl.BlockSpec((2, 2 * HG_CHUNK, HG_CHUNK), lambda s: (0, 0, 0)),
            pl.BlockSpec((LANES, LANES), lambda s: (0, 0)),
        ],
        out_specs=[spec(HG_WIDTH, fwd_idx), spec(HG_WIDTH, bwd_idx)],
        out_shape=[out_sds, out_sds],
        scratch_shapes=[pltpu.VMEM((2 * N_PAIR * BATCH, LANES, LANES), F32)],
        compiler_params=_cparams(("arbitrary",)),
        name="hgrn2",
    )(chunked(qv), chunked(ff), chunked(qv), chunked(fb), tri, ones_blk)
    return o_f.reshape(BATCH, SEQ_ALL, HG_WIDTH), o_b.reshape(BATCH, SEQ_ALL, HG_WIDTH)


def _pair_queries(q):
    lane = lax.broadcasted_iota(jnp.int32, q.shape, 1)
    zero = jnp.zeros_like(q)
    return jnp.concatenate([jnp.where(lane < HEAD_DIM, q, zero), jnp.where(lane >= HEAD_DIM, q, zero)], axis=0)


def _pair_merge(o):
    m = o.shape[0] // 2
    lane = lax.broadcasted_iota(jnp.int32, (m, LANES), 1)
    return jnp.where(lane < HEAD_DIM, o[0:m], o[m:])


def _softmax_pv(scores, values, extra=None):
    m = None
    for s in scores:
        mx = jnp.max(s, axis=-1, keepdims=True)
        m = mx if m is None else jnp.maximum(m, mx)
    if extra is not None:
        m = jnp.maximum(m, extra)
    denom = jnp.exp(extra - m) if extra is not None else 0.0
    acc = None
    for s, v in zip(scores, values):
        e = jnp.exp(s - m)
        denom = denom + jnp.sum(e, axis=-1, keepdims=True)
        pv = _dot(e.astype(BF16), v)
        acc = pv if acc is None else acc + pv
    return acc / denom


def _na_kernel(q_ref, k_ref, v_ref, kc_ref, vc_ref, bias_ref, o_ref):
    j = pl.program_id(1)

    @pl.when(j < ATT_STEPS)
    def _():
        def body(rr, carry):
            r = j * ROWS_PER_STEP + rr
            start = jnp.clip(r - NA_ROWS // 2, 0, GRID_W - NA_ROWS)
            case = jnp.where(r < NA_ROWS // 2, r,
                             jnp.where(r <= GRID_W - NA_ROWS // 2, NA_ROWS // 2, r - (GRID_W - NA_ROWS)))
            k0 = pl.multiple_of(start * GRID_W, GRID_W)
            q0 = pl.multiple_of(rr * GRID_W, GRID_W)
            for p in range(NA_HEADS // 2):
                c0 = p * LANES
                q2 = _pair_queries(q_ref[pl.ds(q0, GRID_W), c0:c0 + LANES])
                bias = jnp.concatenate([bias_ref[case, 2 * p], bias_ref[case, 2 * p + 1]], axis=0)
                s_lat = _dot_nt(q2, k_ref[pl.ds(k0, NA_KEYS), c0:c0 + LANES]) + bias
                s_ctx = _dot_nt(q2, kc_ref[:, c0:c0 + LANES])
                o = _softmax_pv([s_lat, s_ctx], [v_ref[pl.ds(k0, NA_KEYS), c0:c0 + LANES], vc_ref[:, c0:c0 + LANES]])
                o_ref[pl.ds(q0, GRID_W), c0:c0 + LANES] = _pair_merge(o).astype(BF16)
            return carry

        lax.fori_loop(0, ROWS_PER_STEP, body, 0)

    @pl.when(j == ATT_STEPS)
    def _():
        for p in range(NA_HEADS // 2):
            c0 = p * LANES
            q2 = _pair_queries(q_ref[:, c0:c0 + LANES])
            o = _softmax_pv([_dot_nt(q2, kc_ref[:, c0:c0 + LANES])], [vc_ref[:, c0:c0 + LANES]])
            o_ref[:, c0:c0 + LANES] = _pair_merge(o).astype(BF16)


def _att_steps(need_ctx):
    return ATT_STEPS + 1 if need_ctx else ATT_STEPS


def _na_call(at, bias, need_ctx):
    w = NA_WIDTH
    return pl.pallas_call(
        _na_kernel,
        grid=(BATCH, _att_steps(need_ctx)),
        in_specs=[
            pl.BlockSpec((None, TM_PROJ, w), lambda b, j: (b, j, 0)),
            pl.BlockSpec((None, SEQ, w), lambda b, j: (b, 0, 1)),
            pl.BlockSpec((None, SEQ, w), lambda b, j: (b, 0, 2)),
            pl.BlockSpec((None, CTX_LEN, w), lambda b, j: (b, SEQ // CTX_LEN, 1)),
            pl.BlockSpec((None, CTX_LEN, w), lambda b, j: (b, SEQ // CTX_LEN, 2)),
            pl.BlockSpec((NA_CASES, NA_HEADS, GRID_W, NA_KEYS), lambda b, j: (0, 0, 0, 0)),
        ],
        out_specs=pl.BlockSpec((None, TM_PROJ, w), lambda b, j: (b, j, 0)),
        out_shape=jax.ShapeDtypeStruct((BATCH, SEQ_ALL, w), BF16),
        compiler_params=_cparams(("parallel", "arbitrary")),
        name="nbr_attn",
    )(at, at, at, at, at, bias)


def _na_bias_table(rpb):
    half = NA_ROWS // 2
    case_row = np.concatenate([np.arange(half), [half], np.arange(GRID_W - half + 1, GRID_W)])
    start = np.clip(case_row - half, 0, GRID_W - NA_ROWS)
    d_row = start[:, None] + np.arange(NA_ROWS)[None, :] - case_row[:, None]
    col = np.arange(GRID_W)
    c0 = np.clip(col - NA_COLS // 2, 0, GRID_W - NA_COLS)
    col_ok = (col[None, :] >= c0[:, None]) & (col[None, :] < c0[:, None] + NA_COLS)
    d_col = np.clip(col[None, :] - col[:, None], 1 - NA_COLS, NA_COLS - 1)
    row_sel = (d_row[:, :, None] + NA_ROWS - 1 == np.arange(2 * NA_ROWS - 1)).astype(np.float32)
    col_sel = (d_col[None, :, :] + NA_COLS - 1 == np.arange(2 * NA_COLS - 1)[:, None, None]).astype(np.float32)
    rows = jnp.einsum('cir,hrd->chid', row_sel, rpb.astype(F32), precision=lax.Precision.HIGHEST)
    b = jnp.einsum('chid,dqk->chqik', rows, col_sel, precision=lax.Precision.HIGHEST)
    b = jnp.where(col_ok[None, None, :, None, :], b, MASK_VALUE)
    return b.reshape(NA_CASES, NA_HEADS, GRID_W, NA_KEYS)


def _sw_kernel(sink_ref, q_ref, k_ref, v_ref, kc_ref, vc_ref, o_ref):
    j = pl.program_id(1)
    n_pair = SW_HEADS // 2

    def sink_col(p, m):
        row = lax.broadcasted_iota(jnp.int32, (2 * m, 1), 0)
        return jnp.where(row < m, sink_ref[2 * p], sink_ref[2 * p + 1])

    @pl.when(j < ATT_STEPS)
    def _():
        for u in range(TM_PROJ // SW_BLOCK):
            n = j * (TM_PROJ // SW_BLOCK) + u
            start = jnp.clip(n * SW_BLOCK - SW_BLOCK, 0, SEQ - SW_KEYS)
            k0 = pl.multiple_of(start, SW_BLOCK)
            rel = (n * SW_BLOCK - start
                   + lax.broadcasted_iota(jnp.int32, (SW_BLOCK, SW_KEYS), 0)
                   - lax.broadcasted_iota(jnp.int32, (SW_BLOCK, SW_KEYS), 1))
            band = jnp.where(jnp.abs(rel) <= SW_WINDOW, 0.0, MASK_VALUE).astype(F32)
            band2 = jnp.concatenate([band, band], axis=0)
            kw = k_ref[pl.ds(k0, SW_KEYS), :]
            vw = v_ref[pl.ds(k0, SW_KEYS), :]
            for p in range(n_pair):
                c0 = p * LANES
                q2 = _pair_queries(q_ref[u * SW_BLOCK:(u + 1) * SW_BLOCK, c0:c0 + LANES])
                s_lat = _dot_nt(q2, kw) + band2
                s_ctx = _dot_nt(q2, kc_ref[...])
                o = _softmax_pv([s_lat, s_ctx], [vw, vc_ref[...]], extra=sink_col(p, SW_BLOCK))
                o_ref[u * SW_BLOCK:(u + 1) * SW_BLOCK, c0:c0 + LANES] = _pair_merge(o).astype(BF16)

    @pl.when(j == ATT_STEPS)
    def _():
        for p in range(n_pair):
            c0 = p * LANES
            q2 = _pair_queries(q_ref[:, c0:c0 + LANES])
            o = _softmax_pv([_dot_nt(q2, kc_ref[...])], [vc_ref[...]], extra=sink_col(p, TM_PROJ))
            o_ref[:, c0:c0 + LANES] = _pair_merge(o).astype(BF16)


def _sw_call(at, sink_perm, need_ctx):
    q_blk = 3 * NA_WIDTH // SW_WIDTH
    k_blk = (3 * NA_WIDTH + SW_WIDTH) // SW_KV_WIDTH
    grid_spec = pltpu.PrefetchScalarGridSpec(
        num_scalar_prefetch=1,
        grid=(BATCH, _att_steps(need_ctx)),
        in_specs=[
            pl.BlockSpec((None, TM_PROJ, SW_WIDTH), lambda b, j, s: (b, j, q_blk)),
            pl.BlockSpec((None, SEQ, SW_KV_WIDTH), lambda b, j, s: (b, 0, k_blk)),
            pl.BlockSpec((None, SEQ, SW_KV_WIDTH), lambda b, j, s: (b, 0, k_blk + 1)),
            pl.BlockSpec((None, CTX_LEN, SW_KV_WIDTH), lambda b, j, s: (b, SEQ // CTX_LEN, k_blk)),
            pl.BlockSpec((None, CTX_LEN, SW_KV_WIDTH), lambda b, j, s: (b, SEQ // CTX_LEN, k_blk + 1)),
        ],
        out_specs=pl.BlockSpec((None, TM_PROJ, SW_WIDTH), lambda b, j, s: (b, j, 0)),
    )
    return pl.pallas_call(
        _sw_kernel,
        grid_spec=grid_spec,
        out_shape=jax.ShapeDtypeStruct((BATCH, SEQ_ALL, SW_WIDTH), BF16),
        compiler_params=_cparams(("parallel", "arbitrary")),
        name="win_attn",
    )(sink_perm, at, at, at, at, at)


def _outproj_kernel(x_ref, mod_ref, of_ref, ob_ref, sg_ref, na_ref, sw_ref, w_ref, ng_ref, ones_ref, o_ref):
    o = of_ref[...] + ob_ref[...]
    ms = jnp.concatenate(
        [_dot((o[:, c:c + LANES] * o[:, c:c + LANES]).astype(BF16), ones_ref[...]) for c in range(0, HG_WIDTH, LANES)],
        axis=1) * (1.0 / HEAD_DIM)
    hg = (o * lax.rsqrt(ms + EPS)) * ng_ref[...] * sg_ref[...]
    y = _dot(hg.astype(BF16), w_ref[0:HG_WIDTH, :])
    y = y + _dot(na_ref[...], w_ref[HG_WIDTH:HG_WIDTH + NA_WIDTH, :])
    y = y + _dot(sw_ref[...], w_ref[HG_WIDTH + NA_WIDTH:, :])
    o_ref[...] = x_ref[...] + mod_ref[5:6, :] * y


def _outproj_call(x, mods, o_f, o_b, sg, o_na, o_sw, w_out, ng, ones_blk, latent_only):
    n_rows = N_LAT if latent_only else N_TOK

    def per_batch(c):
        return pl.BlockSpec((None, TM_PROJ, c), lambda i: (_tile_batch(i), _tile_pos(i), 0))

    return pl.pallas_call(
        _outproj_kernel,
        grid=(n_rows // TM_PROJ,),
        in_specs=[
            pl.BlockSpec((TM_PROJ, D_MODEL), lambda i: (i, 0)),
            pl.BlockSpec((None, N_MOD, D_MODEL), lambda i: (_tile_mod(i), 0, 0)),
            per_batch(HG_WIDTH), per_batch(HG_WIDTH), per_batch(HG_WIDTH),
            per_batch(NA_WIDTH), per_batch(SW_WIDTH),
            pl.BlockSpec((D_MODEL, D_MODEL), lambda i: (0, 0)),
            pl.BlockSpec((1, HG_WIDTH), lambda i: (0, 0)),
            pl.BlockSpec((LANES, LANES), lambda i: (0, 0)),
        ],
        out_specs=pl.BlockSpec((TM_PROJ, D_MODEL), lambda i: (i, 0)),
        out_shape=jax.ShapeDtypeStruct((n_rows, D_MODEL), F32),
        compiler_params=_cparams(("parallel",)),
        name="out_proj",
    )(x, mods, o_f, o_b, sg, o_na, o_sw, w_out, ng, ones_blk)


def _rope_tables():
    pos = jnp.arange(SEQ)
    pos = jnp.stack([pos // GRID_W, pos % GRID_W], axis=-1).astype(F32)
    nf = HEAD_DIM // 4
    inv = ROPE_THETA ** (-jnp.arange(nf, dtype=F32) / nf)
    ang = pos[:, :, None] * inv
    cos, sin = jnp.cos(ang), jnp.sin(ang)
    cos_h = jnp.stack([cos, cos], axis=2).reshape(SEQ, HEAD_DIM)
    sin_h = jnp.stack([-sin, sin], axis=2).reshape(SEQ, HEAD_DIM)
    reps = LANES // HEAD_DIM
    cos_t = jnp.concatenate([jnp.tile(cos_h, (1, reps)), jnp.ones((CTX_LEN, LANES), F32)], axis=0)
    sin_t = jnp.concatenate([jnp.tile(sin_h, (1, reps)), jnp.zeros((CTX_LEN, LANES), F32)], axis=0)
    return cos_t, sin_t


def _permute_heads(a, axis):
    shape = a.shape
    a = a.reshape(shape[:axis] + (SW_HEADS, HEAD_DIM) + shape[axis + 1:])
    a = jnp.take(a, jnp.asarray(SW_PERM), axis=axis)
    return a.reshape(shape)


def kernel(x, c, ctx, c_ctx, ada_w, ada_b, norm_g, ffn_w1, ffn_w3, ffn_w2, w_in, w_out,
           hg_lb_logits, hg_norm_g, na_rpb, sw_sink, final_g):
    lb_soft = jax.nn.softmax(hg_lb_logits.astype(F32), axis=0)
    lower_bounds = jnp.cumsum(lb_soft, axis=0) - lb_soft[0]
    sw_q0 = 5 * HG_WIDTH + 3 * NA_WIDTH
    w_in_p = jnp.concatenate(
        [w_in[:, :, :sw_q0], _permute_heads(w_in[:, :, sw_q0:sw_q0 + SW_WIDTH], 2), w_in[:, :, sw_q0 + SW_WIDTH:]],
        axis=2).astype(BF16)
    sw_o0 = HG_WIDTH + NA_WIDTH
    w_out_p = jnp.concatenate([w_out[:, :sw_o0], _permute_heads(w_out[:, sw_o0:], 1)], axis=1).astype(BF16)
    sink_p = jnp.take(sw_sink.astype(F32), jnp.asarray(SW_PERM), axis=1)
    w1 = ffn_w1.astype(BF16)
    w3 = ffn_w3.astype(BF16)
    w2 = ffn_w2.astype(BF16)
    cos_t, sin_t = _rope_tables()
    tri, ones_blk = _hgrn_consts()
    final_g2 = final_g.reshape(1, D_MODEL)

    c8 = jnp.concatenate([c, c_ctx[None, :], jnp.zeros((8 - BATCH - 1, D_MODEL), F32)], axis=0)
    mods_all = _ada_call(c8, ada_w, ada_b).reshape(DEPTH, 8, N_MOD, D_MODEL)

    xs = jnp.concatenate([x.reshape(N_LAT, D_MODEL), ctx.reshape(N_CTX, D_MODEL)], axis=0)
    for l in range(DEPTH):
        need_ctx = l < DEPTH - 1
        mods = mods_all[l]
        xs = _ffn_call(xs, mods, norm_g[l, 0:1], w1[l, 0], w3[l, 0], w2[l, 0], final_g2,
                       mod0=0, latent_only=False, final=False)
        qv, ff, fb, sg, at = _inproj_call(xs, mods, norm_g[l, 1:2], w_in_p[l], lower_bounds[l], cos_t, sin_t)
        o_f, o_b = _hgrn_call(qv, ff, fb, tri, ones_blk)
        o_na = _na_call(at, _na_bias_table(na_rpb[l]), need_ctx)
        o_sw = _sw_call(at, sink_p[l], need_ctx)
        xs = _outproj_call(xs, mods, o_f, o_b, sg, o_na, o_sw, w_out_p[l], hg_norm_g[l:l + 1], ones_blk,
                           latent_only=not need_ctx)
        xs = _ffn_call(xs, mods, norm_g[l, 2:3], w1[l, 1], w3[l, 1], w2[l, 1], final_g2,
                       mod0=6, latent_only=not need_ctx, final=not need_ctx)
    return xs.reshape(BATCH, SEQ, D_MODEL)
```

```python
import functools

import jax
import jax.numpy as jnp
import numpy as np
from jax import lax
from jax.experimental import pallas as pl
from jax.experimental.pallas import tpu as pltpu

F32 = jnp.float32
BF16 = jnp.bfloat16

D_MODEL = 1024
BATCH = 4
SEQ = 4096
DEPTH = 4
GRID_W = 64
CTX_LEN = 256
HEAD_DIM = 64
EPS = 1e-6
MASK_VALUE = -1e30
LOG2_E = 1.4426950408889634
ROPE_THETA = 10000.0
N_MOD = 9
D_FF = 2816
HG_WIDTH = 256
HG_HEADS = 4
NA_WIDTH = 384
NA_HEADS = 6
NA_ROWS = 8
NA_COLS = 16
SW_WIDTH = 384
SW_HEADS = 6
SW_KV_WIDTH = 128
SW_WINDOW = 128
SW_BLOCK = 128
IN_WIDTH = 3072

LANES = 128
VMEM_LIMIT_BYTES = 56 * 1024 * 1024

N_LAT = BATCH * SEQ
N_CTX = BATCH * CTX_LEN
N_TOK = N_LAT + N_CTX
SEQ_ALL = SEQ + CTX_LEN
TM_PROJ = 256
TM_FFN = 512
TF_FFN = D_FF
ROWS_PER_STEP = TM_PROJ // GRID_W
ATT_STEPS = SEQ // TM_PROJ
HG_CHUNK = 128
HG_SUB = 16
N_CHUNK_LAT = SEQ // HG_CHUNK
N_CHUNK_CTX = CTX_LEN // HG_CHUNK
N_CHUNK = N_CHUNK_LAT + N_CHUNK_CTX
SW_PERM = (0, 3, 1, 4, 2, 5)
NA_WIN_ROWS = ROWS_PER_STEP + NA_ROWS
NA_WIN_KEYS = NA_WIN_ROWS * GRID_W
NA_CASES = 3
SW_KEYS = 3 * SW_BLOCK


def _cparams(sem):
    return pltpu.CompilerParams(dimension_semantics=sem, vmem_limit_bytes=VMEM_LIMIT_BYTES)


def _silu(a):
    return a * jax.nn.sigmoid(a)


def _dot(a, b):
    return jnp.dot(a, b, preferred_element_type=F32)


def _dot_nt(a, b):
    return lax.dot_general(a, b, (((1,), (1,)), ((), ())), preferred_element_type=F32)


def _modulated_norm(x, g, shift, scale):
    ms = jnp.mean(x * x, axis=-1, keepdims=True)
    return (x * lax.rsqrt(ms + EPS)) * g * (1.0 + scale) + shift


ADA_TN = 1536


def _ada_kernel(c_ref, w_ref, b_ref, o_ref):
    s = _silu(c_ref[...]).astype(BF16)
    o_ref[...] = _dot(s, w_ref[...].astype(BF16)) + b_ref[...]


def _ada_call(c8, ada_w, ada_b):
    n_out = N_MOD * D_MODEL
    return pl.pallas_call(
        _ada_kernel,
        grid=(DEPTH, n_out // ADA_TN),
        in_specs=[
            pl.BlockSpec((8, D_MODEL), lambda l, j: (0, 0)),
            pl.BlockSpec((None, D_MODEL, ADA_TN), lambda l, j: (l, 0, j)),
            pl.BlockSpec((None, 1, ADA_TN), lambda l, j: (l, 0, j)),
        ],
        out_specs=pl.BlockSpec((None, 8, ADA_TN), lambda l, j: (l, 0, j)),
        out_shape=jax.ShapeDtypeStruct((DEPTH, 8, n_out), F32),
        compiler_params=_cparams(("parallel", "parallel")),
        name="ada_mod",
    )(c8, ada_w, ada_b.reshape(DEPTH, 1, n_out))


def _ffn_kernel(x_ref, mod_ref, g_ref, w1_ref, w3_ref, w2_ref, fg_ref, o_ref, *, mod0, final):
    x = x_ref[...]
    h = _modulated_norm(x, g_ref[...], mod_ref[mod0:mod0 + 1, :], mod_ref[mod0 + 1:mod0 + 2, :]).astype(BF16)
    acc = None
    for c in range(D_FF // TF_FFN):
        lo = c * TF_FFN
        a = _silu(_dot(h, w1_ref[:, lo:lo + TF_FFN])) * _dot(h, w3_ref[:, lo:lo + TF_FFN])
        part = _dot(a.astype(BF16), w2_ref[lo:lo + TF_FFN, :])
        acc = part if acc is None else acc + part
    y = x + (0.5 * mod_ref[mod0 + 2:mod0 + 3, :]) * acc
    if final:
        ms = jnp.mean(y * y, axis=-1, keepdims=True)
        y = (y * lax.rsqrt(ms + EPS)) * fg_ref[...]
    o_ref[...] = y


def _resident(shape):
    return pl.BlockSpec(shape, lambda i: (0,) * len(shape), pipeline_mode=pl.Buffered(1))


def _ffn_call(x, mods, g, w1, w3, w2, final_g, *, mod0, latent_only, final):
    n_rows = N_LAT if latent_only else N_TOK
    tiles_per_batch = SEQ // TM_FFN

    def mod_idx(i):
        return (jnp.where(i < BATCH * tiles_per_batch, i // tiles_per_batch, BATCH), 0, 0)

    return pl.pallas_call(
        functools.partial(_ffn_kernel, mod0=mod0, final=final),
        grid=(n_rows // TM_FFN,),
        in_specs=[
            pl.BlockSpec((TM_FFN, D_MODEL), lambda i: (i, 0)),
            pl.BlockSpec((None, N_MOD, D_MODEL), mod_idx),
            _resident((1, D_MODEL)),
            _resident((D_MODEL, D_FF)),
            _resident((D_MODEL, D_FF)),
            _resident((D_FF, D_MODEL)),
            _resident((1, D_MODEL)),
        ],
        out_specs=pl.BlockSpec((TM_FFN, D_MODEL), lambda i: (i, 0)),
        out_shape=jax.ShapeDtypeStruct((n_rows, D_MODEL), F32),
        compiler_params=_cparams(("parallel",)),
        name="ffn",
    )(x, mods, g, w1, w3, w2, final_g)


def _tile_batch(i):
    tiles_per_batch = SEQ // TM_PROJ
    return jnp.where(i < BATCH * tiles_per_batch, i // tiles_per_batch, i - BATCH * tiles_per_batch)


def _tile_pos(i):
    tiles_per_batch = SEQ // TM_PROJ
    return jnp.where(i < BATCH * tiles_per_batch, i % tiles_per_batch, tiles_per_batch)


def _tile_mod(i):
    tiles_per_batch = SEQ // TM_PROJ
    return jnp.where(i < BATCH * tiles_per_batch, i // tiles_per_batch, BATCH)


def _rope(z, cos, sin_signed, first_of_pair):
    partner = jnp.where(first_of_pair, pltpu.roll(z, LANES - 16, 1), pltpu.roll(z, 16, 1))
    return z * cos + partner * sin_signed


def _inproj_kernel(x_ref, mod_ref, g_ref, w_ref, lb_ref, cos_ref, sin_ref,
                   qv_ref, ff_ref, fb_ref, sg_ref, at_ref):
    h = _modulated_norm(x_ref[...], g_ref[...], mod_ref[3:4, :], mod_ref[4:5, :]).astype(BF16)
    W = HG_WIDTH

    def proj(lo, hi):
        return _dot(h, w_ref[:, lo:hi])

    qv_ref[:, 0:W] = _silu(proj(0, W))
    qv_ref[:, W:2 * W] = proj(3 * W, 4 * W)
    for d, dst in ((0, ff_ref), (1, fb_ref)):
        z = proj((1 + d) * W, (2 + d) * W)
        lb = lb_ref[d:d + 1, :]
        dst[:, 0:W] = jnp.log(lb + (1.0 - lb) * jax.nn.sigmoid(z)) * LOG2_E
        dst[:, W:2 * W] = jnp.log((1.0 - lb) * jax.nn.sigmoid(-z)) * LOG2_E
    sg_ref[...] = _silu(proj(4 * W, 5 * W))

    scale = HEAD_DIM ** -0.5
    na0 = 5 * W
    at_ref[:, 0:NA_WIDTH] = (proj(na0, na0 + NA_WIDTH) * scale).astype(BF16)
    at_ref[:, NA_WIDTH:3 * NA_WIDTH] = proj(na0 + NA_WIDTH, na0 + 3 * NA_WIDTH).astype(BF16)
    sw0 = na0 + 3 * NA_WIDTH
    cos = cos_ref[...]
    sin = sin_ref[...]
    lane = lax.broadcasted_iota(jnp.int32, (TM_PROJ, LANES), 1)
    first = (lane % 32) < 16
    for j in range((SW_WIDTH + SW_KV_WIDTH) // LANES):
        z = _rope(proj(sw0 + j * LANES, sw0 + (j + 1) * LANES), cos, sin, first)
        if j < SW_WIDTH // LANES:
            z = z * scale
        at_ref[:, 3 * NA_WIDTH + j * LANES:3 * NA_WIDTH + (j + 1) * LANES] = z.astype(BF16)
    v0 = sw0 + SW_WIDTH + SW_KV_WIDTH
    at_ref[:, 3 * NA_WIDTH + SW_WIDTH + SW_KV_WIDTH:] = proj(v0, v0 + SW_KV_WIDTH).astype(BF16)


AT_WIDTH = 3 * NA_WIDTH + SW_WIDTH + 2 * SW_KV_WIDTH


def _inproj_call(x, mods, g, w_in, lb, cos_t, sin_t):
    def out_spec(c):
        return pl.BlockSpec((None, TM_PROJ, c), lambda i: (_tile_batch(i), _tile_pos(i), 0))

    def out_shape(c, dt):
        return jax.ShapeDtypeStruct((BATCH, SEQ_ALL, c), dt)

    return pl.pallas_call(
        _inproj_kernel,
        grid=(N_TOK // TM_PROJ,),
        in_specs=[
            pl.BlockSpec((TM_PROJ, D_MODEL), lambda i: (i, 0)),
            pl.BlockSpec((None, N_MOD, D_MODEL), lambda i: (_tile_mod(i), 0, 0)),
            pl.BlockSpec((1, D_MODEL), lambda i: (0, 0)),
            pl.BlockSpec((D_MODEL, IN_WIDTH), lambda i: (0, 0)),
            pl.BlockSpec((2, HG_WIDTH), lambda i: (0, 0)),
            pl.BlockSpec((TM_PROJ, LANES), lambda i: (_tile_pos(i), 0)),
            pl.BlockSpec((TM_PROJ, LANES), lambda i: (_tile_pos(i), 0)),
        ],
        out_specs=[out_spec(2 * HG_WIDTH), out_spec(2 * HG_WIDTH), out_spec(2 * HG_WIDTH),
                   out_spec(HG_WIDTH), out_spec(AT_WIDTH)],
        out_shape=[out_shape(2 * HG_WIDTH, F32), out_shape(2 * HG_WIDTH, F32), out_shape(2 * HG_WIDTH, F32),
                   out_shape(HG_WIDTH, F32), out_shape(AT_WIDTH, BF16)],
        compiler_params=_cparams(("parallel",)),
        name="in_proj",
    )(x, mods, g, w_in, lb, cos_t, sin_t)


N_SUB = HG_CHUNK // HG_SUB
N_PAIR = HG_HEADS // 2
HG_MINI = HG_SUB // 2


def _hgrn_consts():
    t = np.arange(HG_CHUNK)
    same = (t[:, None] // HG_SUB) == (t[None, :] // HG_SUB)
    lower = same & (t[None, :] <= t[:, None])
    upper = same & (t[None, :] >= t[:, None])
    tri = np.stack([np.concatenate([lower, same], 0), np.concatenate([upper, same], 0)]).astype(np.float32)
    d = np.arange(LANES)
    head_blocks = ((d[:, None] // HEAD_DIM) == (d[None, :] // HEAD_DIM)).astype(np.float32)
    return jnp.asarray(tri, BF16), jnp.asarray(head_blocks, BF16)


def _split3(a):
    hi = a.astype(BF16)
    r1 = a - hi.astype(F32)
    mid = r1.astype(BF16)
    lo = (r1 - mid.astype(F32)).astype(BF16)
    return jnp.concatenate([hi, mid, lo], axis=1)


def _bcast_rows(a, s, block):
    n = HG_CHUNK // block
    a3 = a.reshape(n, block, LANES)
    return jnp.broadcast_to(a3[:, s:s + 1, :], (n, block, LANES)).reshape(HG_CHUNK, LANES)


def _hgrn_group(qs, v, gl, lk, tri, ones_blk, st, direction):
    fwd = direction == 0
    c = _dot(tri, _split3(gl))
    c = c[:, 0:LANES] + c[:, LANES:2 * LANES] + c[:, 2 * LANES:]
    cum = c[0:HG_CHUNK]
    tot = c[HG_CHUNK:]
    row = lax.broadcasted_iota(jnp.int32, (HG_CHUNK, LANES), 0)
    t_mini = row % HG_MINI
    blk = row // HG_SUB
    v16 = v.astype(BF16)
    ck = cum - lk

    o = jnp.zeros((HG_CHUNK, LANES), F32)
    for s in range(HG_MINI):
        keep = (t_mini >= s) if fwd else (t_mini <= s)
        w = jnp.where(keep, qs * jnp.exp2(cum - _bcast_rows(ck, s, HG_MINI)), 0.0)
        o = o + _dot(w.astype(BF16), ones_blk) * _bcast_rows(v, s, HG_MINI)

    later = ((row % HG_SUB) >= HG_MINI) if fwd else ((row % HG_SUB) < HG_MINI)
    edge = _bcast_rows(cum, HG_MINI - 1 if fwd else HG_MINI, HG_SUB)
    q_edge = jnp.where(later, qs * jnp.exp2(jnp.minimum(cum - edge, 0.0)), 0.0)
    k_edge = jnp.where(later, 0.0, jnp.exp2(jnp.minimum(edge - ck, 0.0)))
    a = _dot_nt(_pair_queries(q_edge.astype(BF16)), k_edge.astype(BF16))
    q_blk = lax.broadcasted_iota(jnp.int32, (2 * HG_CHUNK, HG_CHUNK), 0) % HG_CHUNK // HG_SUB
    k_blk = lax.broadcasted_iota(jnp.int32, (2 * HG_CHUNK, HG_CHUNK), 1) // HG_SUB
    a = jnp.where(q_blk == k_blk, a, 0.0)
    o = o + _pair_merge(_dot(a.astype(BF16), v16))

    qd = qs * jnp.exp2(cum)
    kd = jnp.exp2(tot - ck)
    dec = jnp.exp2(tot)
    k_exp = jnp.concatenate([jnp.where(blk == j, kd, 0.0).astype(BF16) for j in range(N_SUB)], axis=1)
    upd = _dot(v.T.astype(BF16), k_exp)
    head_mask = ones_blk.astype(F32)
    before = [None] * N_SUB
    for j in (range(N_SUB) if fwd else range(N_SUB - 1, -1, -1)):
        before[j] = st.astype(BF16)
        st = st * dec[j * HG_SUB:j * HG_SUB + 1, :] + upd[:, j * LANES:(j + 1) * LANES] * head_mask
    q_exp = jnp.concatenate([jnp.where(blk == j, qd, 0.0).astype(BF16) for j in range(N_SUB)], axis=1)
    o = o + _dot_nt(q_exp, jnp.concatenate(before, axis=1))
    return o, st


def _hgrn_kernel(qvf_ref, ff_ref, qvb_ref, fb_ref, tri_ref, ones_ref, of_ref, ob_ref, st_scr):
    @pl.when(pl.program_id(0) == 0)
    def _():
        st_scr[...] = jnp.zeros_like(st_scr)

    ones_blk = ones_ref[...]
    W = HG_WIDTH
    for direction, (qv_ref, f_ref, o_ref) in enumerate(((qvf_ref, ff_ref, of_ref), (qvb_ref, fb_ref, ob_ref))):
        tri = tri_ref[direction]

        def body(b, carry, qv_ref=qv_ref, f_ref=f_ref, o_ref=o_ref, tri=tri, direction=direction):
            for hp in range(N_PAIR):
                c0 = hp * LANES
                g = (direction * N_PAIR + hp) * BATCH + b
                o, st = _hgrn_group(
                    qv_ref[b, :, c0:c0 + LANES], qv_ref[b, :, W + c0:W + c0 + LANES],
                    f_ref[b, :, c0:c0 + LANES], f_ref[b, :, W + c0:W + c0 + LANES],
                    tri, ones_blk, st_scr[g], direction)
                o_ref[b, :, c0:c0 + LANES] = o
                st_scr[g] = st
            return carry

        lax.fori_loop(0, BATCH, body, 0)


def _hgrn_call(qv, ff, fb, tri, ones_blk):
    def chunked(a):
        return a.reshape(BATCH, N_CHUNK, HG_CHUNK, a.shape[-1])

    def fwd_idx(s):
        return jnp.where(s < N_CHUNK_CTX, N_CHUNK_LAT + s, s - N_CHUNK_CTX)

    def bwd_idx(s):
        return jnp.where(s < N_CHUNK_CTX, N_CHUNK - 1 - s, N_CHUNK - 1 - s)

    def spec(c, idx):
        return pl.BlockSpec((BATCH, None, HG_CHUNK, c), lambda s: (0, idx(s), 0, 0))

    out_sds = jax.ShapeDtypeStruct((BATCH, N_CHUNK, HG_CHUNK, HG_WIDTH), F32)
    o_f, o_b = pl.pallas_call(
        _hgrn_kernel,
        grid=(N_CHUNK,),
        in_specs=[
            spec(2 * HG_WIDTH, fwd_idx), spec(2 * HG_WIDTH, fwd_idx),
            spec(2 * HG_WIDTH, bwd_idx), spec(2 * HG_WIDTH, bwd_idx),
            pl.BlockSpec((2, 2 * HG_CHUNK, HG_CHUNK), lambda s: (0, 0, 0)),
            pl.BlockSpec((LANES, LANES), lambda s: (0, 0)),
        ],
        out_specs=[spec(HG_WIDTH, fwd_idx), spec(HG_WIDTH, bwd_idx)],
        out_shape=[out_sds, out_sds],
        scratch_shapes=[pltpu.VMEM((2 * N_PAIR * BATCH, LANES, LANES), F32)],
        compiler_params=_cparams(("arbitrary",)),
        name="hgrn2",
    )(chunked(qv), chunked(ff), chunked(qv), chunked(fb), tri, ones_blk)
    return o_f.reshape(BATCH, SEQ_ALL, HG_WIDTH), o_b.reshape(BATCH, SEQ_ALL, HG_WIDTH)


def _pair_queries(q):
    lane = lax.broadcasted_iota(jnp.int32, q.shape, 1)
    zero = jnp.zeros_like(q)
    return jnp.concatenate([jnp.where(lane < HEAD_DIM, q, zero), jnp.where(lane >= HEAD_DIM, q, zero)], axis=0)


def _pair_merge(o):
    m = o.shape[0] // 2
    lane = lax.broadcasted_iota(jnp.int32, (m, LANES), 1)
    return jnp.where(lane < HEAD_DIM, o[0:m], o[m:])


def _softmax_pv(scores, values, extra=None):
    def lane_chunks(s):
        return [s[:, c:c + LANES] for c in range(0, s.shape[1], LANES)]

    m = functools.reduce(jnp.maximum, [c for s in scores for c in lane_chunks(s)])
    m = jnp.max(m, axis=-1, keepdims=True)
    if extra is not None:
        m = jnp.maximum(m, extra)
    acc = None
    esum = None
    for s, v in zip(scores, values):
        e = jnp.exp(s - m)
        part = functools.reduce(jnp.add, lane_chunks(e))
        esum = part if esum is None else esum + part
        pv = _dot(e.astype(BF16), v)
        acc = pv if acc is None else acc + pv
    denom = jnp.sum(esum, axis=-1, keepdims=True)
    if extra is not None:
        denom = denom + jnp.exp(extra - m)
    return acc / denom


def _na_kernel(q_ref, k_ref, v_ref, kc_ref, vc_ref, bias_ref, o_ref):
    j = pl.program_id(1)

    @pl.when(j < ATT_STEPS)
    def _():
        r0 = j * ROWS_PER_STEP
        ws = jnp.clip(r0 - NA_ROWS // 2, 0, GRID_W - NA_WIN_ROWS)
        k0 = pl.multiple_of(ws * GRID_W, ROWS_PER_STEP * GRID_W)
        for p in range(NA_HEADS // 2):
            c0 = p * LANES
            q2 = _pair_queries(q_ref[:, c0:c0 + LANES])
            bias = jnp.concatenate([bias_ref[2 * p], bias_ref[2 * p + 1]], axis=0)
            s_lat = _dot_nt(q2, k_ref[pl.ds(k0, NA_WIN_KEYS), c0:c0 + LANES]) + bias
            s_ctx = _dot_nt(q2, kc_ref[:, c0:c0 + LANES])
            o = _softmax_pv([s_lat, s_ctx], [v_ref[pl.ds(k0, NA_WIN_KEYS), c0:c0 + LANES], vc_ref[:, c0:c0 + LANES]])
            o_ref[:, c0:c0 + LANES] = _pair_merge(o).astype(BF16)

    @pl.when(j == ATT_STEPS)
    def _():
        for p in range(NA_HEADS // 2):
            c0 = p * LANES
            q2 = _pair_queries(q_ref[:, c0:c0 + LANES])
            o = _softmax_pv([_dot_nt(q2, kc_ref[:, c0:c0 + LANES])], [vc_ref[:, c0:c0 + LANES]])
            o_ref[:, c0:c0 + LANES] = _pair_merge(o).astype(BF16)


def _att_steps(need_ctx):
    return ATT_STEPS + 1 if need_ctx else ATT_STEPS


def _na_call(at, bias, need_ctx):
    w = NA_WIDTH

    def bias_idx(b, j):
        return (jnp.where(j == 0, 0, jnp.where(j >= ATT_STEPS - 1, 2, 1)), 0, 0, 0)

    return pl.pallas_call(
        _na_kernel,
        grid=(BATCH, _att_steps(need_ctx)),
        in_specs=[
            pl.BlockSpec((None, TM_PROJ, w), lambda b, j: (b, j, 0)),
            pl.BlockSpec((None, SEQ, w), lambda b, j: (b, 0, 1)),
            pl.BlockSpec((None, SEQ, w), lambda b, j: (b, 0, 2)),
            pl.BlockSpec((None, CTX_LEN, w), lambda b, j: (b, SEQ // CTX_LEN, 1)),
            pl.BlockSpec((None, CTX_LEN, w), lambda b, j: (b, SEQ // CTX_LEN, 2)),
            pl.BlockSpec((None, NA_HEADS, TM_PROJ, NA_WIN_KEYS), bias_idx),
        ],
        out_specs=pl.BlockSpec((None, TM_PROJ, w), lambda b, j: (b, j, 0)),
        out_shape=jax.ShapeDtypeStruct((BATCH, SEQ_ALL, w), BF16),
        compiler_params=_cparams(("parallel", "arbitrary")),
        name="nbr_attn",
    )(at, at, at, at, at, bias)


def _na_bias_table(rpb):
    half = NA_ROWS // 2
    r0 = np.array([0, ROWS_PER_STEP, GRID_W - ROWS_PER_STEP])
    ws = np.clip(r0 - half, 0, GRID_W - NA_WIN_ROWS)
    r = r0[:, None] + np.arange(ROWS_PER_STEP)[None, :]
    start = np.clip(r - half, 0, GRID_W - NA_ROWS)
    krow = ws[:, None] + np.arange(NA_WIN_ROWS)[None, :]
    d_row = krow[:, None, :] - r[:, :, None]
    row_ok = (krow[:, None, :] >= start[:, :, None]) & (krow[:, None, :] < start[:, :, None] + NA_ROWS)
    col = np.arange(GRID_W)
    c0 = np.clip(col - NA_COLS // 2, 0, GRID_W - NA_COLS)
    col_ok = (col[None, :] >= c0[:, None]) & (col[None, :] < c0[:, None] + NA_COLS)
    d_col = np.clip(col[None, :] - col[:, None], 1 - NA_COLS, NA_COLS - 1)
    row_sel = ((d_row[..., None] + NA_ROWS - 1 == np.arange(2 * NA_ROWS - 1)) & row_ok[..., None]).astype(np.float32)
    col_sel = (d_col[None, :, :] + NA_COLS - 1 == np.arange(2 * NA_COLS - 1)[:, None, None]).astype(np.float32)
    rows = jnp.einsum('cair,hrd->chaid', row_sel, rpb.astype(F32), precision=lax.Precision.HIGHEST)
    b = jnp.einsum('chaid,dqk->chaqik', rows, col_sel, precision=lax.Precision.HIGHEST)
    ok = row_ok[:, None, :, None, :, None] & col_ok[None, None, None, :, None, :]
    b = jnp.where(ok, b, MASK_VALUE)
    return b.reshape(NA_CASES, NA_HEADS, TM_PROJ, NA_WIN_KEYS)


def _sw_kernel(sink_ref, q_ref, k_ref, v_ref, kc_ref, vc_ref, o_ref):
    j = pl.program_id(1)
    n_pair = SW_HEADS // 2

    def sink_col(p, m):
        row = lax.broadcasted_iota(jnp.int32, (2 * m, 1), 0)
        return jnp.where(row < m, sink_ref[2 * p], sink_ref[2 * p + 1])

    @pl.when(j < ATT_STEPS)
    def _():
        for u in range(TM_PROJ // SW_BLOCK):
            n = j * (TM_PROJ // SW_BLOCK) + u
            start = jnp.clip(n * SW_BLOCK - SW_BLOCK, 0, SEQ - SW_KEYS)
            k0 = pl.multiple_of(start, SW_BLOCK)
            rel = (n * SW_BLOCK - start
                   + lax.broadcasted_iota(jnp.int32, (SW_BLOCK, SW_KEYS), 0)
                   - lax.broadcasted_iota(jnp.int32, (SW_BLOCK, SW_KEYS), 1))
            band = jnp.where(jnp.abs(rel) <= SW_WINDOW, 0.0, MASK_VALUE).astype(F32)
            band2 = jnp.concatenate([band, band], axis=0)
            kw = k_ref[pl.ds(k0, SW_KEYS), :]
            vw = v_ref[pl.ds(k0, SW_KEYS), :]
            for p in range(n_pair):
                c0 = p * LANES
                q2 = _pair_queries(q_ref[u * SW_BLOCK:(u + 1) * SW_BLOCK, c0:c0 + LANES])
                s_lat = _dot_nt(q2, kw) + band2
                s_ctx = _dot_nt(q2, kc_ref[...])
                o = _softmax_pv([s_lat, s_ctx], [vw, vc_ref[...]], extra=sink_col(p, SW_BLOCK))
                o_ref[u * SW_BLOCK:(u + 1) * SW_BLOCK, c0:c0 + LANES] = _pair_merge(o).astype(BF16)

    @pl.when(j == ATT_STEPS)
    def _():
        for p in range(n_pair):
            c0 = p * LANES
            q2 = _pair_queries(q_ref[:, c0:c0 + LANES])
            o = _softmax_pv([_dot_nt(q2, kc_ref[...])], [vc_ref[...]], extra=sink_col(p, TM_PROJ))
            o_ref[:, c0:c0 + LANES] = _pair_merge(o).astype(BF16)


def _sw_call(at, sink_perm, need_ctx):
    q_blk = 3 * NA_WIDTH // SW_WIDTH
    k_blk = (3 * NA_WIDTH + SW_WIDTH) // SW_KV_WIDTH
    grid_spec = pltpu.PrefetchScalarGridSpec(
        num_scalar_prefetch=1,
        grid=(BATCH, _att_steps(need_ctx)),
        in_specs=[
            pl.BlockSpec((None, TM_PROJ, SW_WIDTH), lambda b, j, s: (b, j, q_blk)),
            pl.BlockSpec((None, SEQ, SW_KV_WIDTH), lambda b, j, s: (b, 0, k_blk)),
            pl.BlockSpec((None, SEQ, SW_KV_WIDTH), lambda b, j, s: (b, 0, k_blk + 1)),
            pl.BlockSpec((None, CTX_LEN, SW_KV_WIDTH), lambda b, j, s: (b, SEQ // CTX_LEN, k_blk)),
            pl.BlockSpec((None, CTX_LEN, SW_KV_WIDTH), lambda b, j, s: (b, SEQ // CTX_LEN, k_blk + 1)),
        ],
        out_specs=pl.BlockSpec((None, TM_PROJ, SW_WIDTH), lambda b, j, s: (b, j, 0)),
    )
    return pl.pallas_call(
        _sw_kernel,
        grid_spec=grid_spec,
        out_shape=jax.ShapeDtypeStruct((BATCH, SEQ_ALL, SW_WIDTH), BF16),
        compiler_params=_cparams(("parallel", "arbitrary")),
        name="win_attn",
    )(sink_perm, at, at, at, at, at)


def _outproj_kernel(x_ref, mod_ref, of_ref, ob_ref, sg_ref, na_ref, sw_ref, w_ref, ng_ref, ones_ref, o_ref):
    o = of_ref[...] + ob_ref[...]
    ms = jnp.concatenate(
        [_dot((o[:, c:c + LANES] * o[:, c:c + LANES]).astype(BF16), ones_ref[...]) for c in range(0, HG_WIDTH, LANES)],
        axis=1) * (1.0 / HEAD_DIM)
    hg = (o * lax.rsqrt(ms + EPS)) * ng_ref[...] * sg_ref[...]
    y = _dot(hg.astype(BF16), w_ref[0:HG_WIDTH, :])
    y = y + _dot(na_ref[...], w_ref[HG_WIDTH:HG_WIDTH + NA_WIDTH, :])
    y = y + _dot(sw_ref[...], w_ref[HG_WIDTH + NA_WIDTH:, :])
    o_ref[...] = x_ref[...] + mod_ref[5:6, :] * y


def _outproj_call(x, mods, o_f, o_b, sg, o_na, o_sw, w_out, ng, ones_blk, latent_only):
    n_rows = N_LAT if latent_only else N_TOK

    def per_batch(c):
        return pl.BlockSpec((None, TM_PROJ, c), lambda i: (_tile_batch(i), _tile_pos(i), 0))

    return pl.pallas_call(
        _outproj_kernel,
        grid=(n_rows // TM_PROJ,),
        in_specs=[
            pl.BlockSpec((TM_PROJ, D_MODEL), lambda i: (i, 0)),
            pl.BlockSpec((None, N_MOD, D_MODEL), lambda i: (_tile_mod(i), 0, 0)),
            per_batch(HG_WIDTH), per_batch(HG_WIDTH), per_batch(HG_WIDTH),
            per_batch(NA_WIDTH), per_batch(SW_WIDTH),
            pl.BlockSpec((D_MODEL, D_MODEL), lambda i: (0, 0)),
            pl.BlockSpec((1, HG_WIDTH), lambda i: (0, 0)),
            pl.BlockSpec((LANES, LANES), lambda i: (0, 0)),
        ],
        out_specs=pl.BlockSpec((TM_PROJ, D_MODEL), lambda i: (i, 0)),
        out_shape=jax.ShapeDtypeStruct((n_rows, D_MODEL), F32),
        compiler_params=_cparams(("parallel",)),
        name="out_proj",
    )(x, mods, o_f, o_b, sg, o_na, o_sw, w_out, ng, ones_blk)


def _rope_tables():
    pos = jnp.arange(SEQ)
    pos = jnp.stack([pos // GRID_W, pos % GRID_W], axis=-1).astype(F32)
    nf = HEAD_DIM // 4
    inv = ROPE_THETA ** (-jnp.arange(nf, dtype=F32) / nf)
    ang = pos[:, :, None] * inv
    cos, sin = jnp.cos(ang), jnp.sin(ang)
    cos_h = jnp.stack([cos, cos], axis=2).reshape(SEQ, HEAD_DIM)
    sin_h = jnp.stack([-sin, sin], axis=2).reshape(SEQ, HEAD_DIM)
    reps = LANES // HEAD_DIM
    cos_t = jnp.concatenate([jnp.tile(cos_h, (1, reps)), jnp.ones((CTX_LEN, LANES), F32)], axis=0)
    sin_t = jnp.concatenate([jnp.tile(sin_h, (1, reps)), jnp.zeros((CTX_LEN, LANES), F32)], axis=0)
    return cos_t, sin_t


def _permute_heads(a, axis):
    shape = a.shape
    a = a.reshape(shape[:axis] + (SW_HEADS, HEAD_DIM) + shape[axis + 1:])
    a = jnp.take(a, jnp.asarray(SW_PERM), axis=axis)
    return a.reshape(shape)


def kernel(x, c, ctx, c_ctx, ada_w, ada_b, norm_g, ffn_w1, ffn_w3, ffn_w2, w_in, w_out,
           hg_lb_logits, hg_norm_g, na_rpb, sw_sink, final_g):
    lb_soft = jax.nn.softmax(hg_lb_logits.astype(F32), axis=0)
    lower_bounds = jnp.cumsum(lb_soft, axis=0) - lb_soft[0]
    sw_q0 = 5 * HG_WIDTH + 3 * NA_WIDTH
    w_in_p = jnp.concatenate(
        [w_in[:, :, :sw_q0], _permute_heads(w_in[:, :, sw_q0:sw_q0 + SW_WIDTH], 2), w_in[:, :, sw_q0 + SW_WIDTH:]],
        axis=2).astype(BF16)
    sw_o0 = HG_WIDTH + NA_WIDTH
    w_out_p = jnp.concatenate([w_out[:, :sw_o0], _permute_heads(w_out[:, sw_o0:], 1)], axis=1).astype(BF16)
    sink_p = jnp.take(sw_sink.astype(F32), jnp.asarray(SW_PERM), axis=1)
    w1 = ffn_w1.astype(BF16)
    w3 = ffn_w3.astype(BF16)
    w2 = ffn_w2.astype(BF16)
    cos_t, sin_t = _rope_tables()
    tri, ones_blk = _hgrn_consts()
    final_g2 = final_g.reshape(1, D_MODEL)

    c8 = jnp.concatenate([c, c_ctx[None, :], jnp.zeros((8 - BATCH - 1, D_MODEL), F32)], axis=0)
    mods_all = _ada_call(c8, ada_w, ada_b).reshape(DEPTH, 8, N_MOD, D_MODEL)

    xs = jnp.concatenate([x.reshape(N_LAT, D_MODEL), ctx.reshape(N_CTX, D_MODEL)], axis=0)
    for l in range(DEPTH):
        need_ctx = l < DEPTH - 1
        mods = mods_all[l]
        xs = _ffn_call(xs, mods, norm_g[l, 0:1], w1[l, 0], w3[l, 0], w2[l, 0], final_g2,
                       mod0=0, latent_only=False, final=False)
        qv, ff, fb, sg, at = _inproj_call(xs, mods, norm_g[l, 1:2], w_in_p[l], lower_bounds[l], cos_t, sin_t)
        o_f, o_b = _hgrn_call(qv, ff, fb, tri, ones_blk)
        o_na = _na_call(at, _na_bias_table(na_rpb[l]), need_ctx)
        o_sw = _sw_call(at, sink_p[l], need_ctx)
        xs = _outproj_call(xs, mods, o_f, o_b, sg, o_na, o_sw, w_out_p[l], hg_norm_g[l:l + 1], ones_blk,
                           latent_only=not need_ctx)
        xs = _ffn_call(xs, mods, norm_g[l, 2:3], w1[l, 1], w3[l, 1], w2[l, 1], final_g2,
                       mod0=6, latent_only=not need_ctx, final=not need_ctx)
    return xs.reshape(BATCH, SEQ, D_MODEL)
```

```python
import functools

import jax
import jax.numpy as jnp
import numpy as np
from jax import lax
from jax.experimental import pallas as pl
from jax.experimental.pallas import tpu as pltpu

F32 = jnp.float32
BF16 = jnp.bfloat16

D_MODEL = 1024
BATCH = 4
SEQ = 4096
DEPTH = 4
GRID_W = 64
CTX_LEN = 256
HEAD_DIM = 64
EPS = 1e-6
MASK_VALUE = -1e30
LOG2_E = 1.4426950408889634
ROPE_THETA = 10000.0
N_MOD = 9
D_FF = 2816
HG_WIDTH = 256
HG_HEADS = 4
NA_WIDTH = 384
NA_HEADS = 6
NA_ROWS = 8
NA_COLS = 16
SW_WIDTH = 384
SW_HEADS = 6
SW_KV_WIDTH = 128
SW_WINDOW = 128
SW_BLOCK = 128
IN_WIDTH = 3072

LANES = 128
VMEM_LIMIT_BYTES = 56 * 1024 * 1024

N_LAT = BATCH * SEQ
N_CTX = BATCH * CTX_LEN
N_TOK = N_LAT + N_CTX
SEQ_ALL = SEQ + CTX_LEN
TM_PROJ = 256
TM_FFN = 512
TF_FFN = D_FF
ROWS_PER_STEP = TM_PROJ // GRID_W
ATT_STEPS = SEQ // TM_PROJ
HG_CHUNK = 128
HG_SUB = 16
N_CHUNK_LAT = SEQ // HG_CHUNK
N_CHUNK_CTX = CTX_LEN // HG_CHUNK
N_CHUNK = N_CHUNK_LAT + N_CHUNK_CTX
NA_KEYS = NA_ROWS * GRID_W
NA_CASES = 8
SW_KEYS = 3 * SW_BLOCK


def _cparams(sem):
    return pltpu.CompilerParams(dimension_semantics=sem, vmem_limit_bytes=VMEM_LIMIT_BYTES)


def _silu(a):
    return a * jax.nn.sigmoid(a)


def _dot(a, b):
    return jnp.dot(a, b, preferred_element_type=F32)


def _dot_nt(a, b):
    return lax.dot_general(a, b, (((1,), (1,)), ((), ())), preferred_element_type=F32)


def _modulated_norm(x, g, shift, scale):
    ms = jnp.mean(x * x, axis=-1, keepdims=True)
    return (x * lax.rsqrt(ms + EPS)) * g * (1.0 + scale) + shift


ADA_TN = 1536


def _ada_kernel(c_ref, w_ref, b_ref, o_ref):
    s = _silu(c_ref[...]).astype(BF16)
    o_ref[...] = _dot(s, w_ref[...].astype(BF16)) + b_ref[...]


def _ada_call(c8, ada_w, ada_b):
    n_out = N_MOD * D_MODEL
    return pl.pallas_call(
        _ada_kernel,
        grid=(DEPTH, n_out // ADA_TN),
        in_specs=[
            pl.BlockSpec((8, D_MODEL), lambda l, j: (0, 0)),
            pl.BlockSpec((None, D_MODEL, ADA_TN), lambda l, j: (l, 0, j)),
            pl.BlockSpec((None, 1, ADA_TN), lambda l, j: (l, 0, j)),
        ],
        out_specs=pl.BlockSpec((None, 8, ADA_TN), lambda l, j: (l, 0, j)),
        out_shape=jax.ShapeDtypeStruct((DEPTH, 8, n_out), F32),
        compiler_params=_cparams(("parallel", "parallel")),
        name="ada_mod",
    )(c8, ada_w, ada_b.reshape(DEPTH, 1, n_out))


def _ffn_kernel(x_ref, mod_ref, g_ref, w1_ref, w3_ref, w2_ref, fg_ref, o_ref, *, mod0, final):
    x = x_ref[...]
    h = _modulated_norm(x, g_ref[...], mod_ref[mod0:mod0 + 1, :], mod_ref[mod0 + 1:mod0 + 2, :]).astype(BF16)
    acc = None
    for c in range(D_FF // TF_FFN):
        lo = c * TF_FFN
        a = _silu(_dot(h, w1_ref[:, lo:lo + TF_FFN])) * _dot(h, w3_ref[:, lo:lo + TF_FFN])
        part = _dot(a.astype(BF16), w2_ref[lo:lo + TF_FFN, :])
        acc = part if acc is None else acc + part
    y = x + (0.5 * mod_ref[mod0 + 2:mod0 + 3, :]) * acc
    if final:
        ms = jnp.mean(y * y, axis=-1, keepdims=True)
        y = (y * lax.rsqrt(ms + EPS)) * fg_ref[...]
    o_ref[...] = y


def _resident(shape):
    return pl.BlockSpec(shape, lambda i: (0,) * len(shape), pipeline_mode=pl.Buffered(1))


def _ffn_call(x, mods, g, w1, w3, w2, final_g, *, mod0, latent_only, final):
    n_rows = N_LAT if latent_only else N_TOK
    tiles_per_batch = SEQ // TM_FFN

    def mod_idx(i):
        return (jnp.where(i < BATCH * tiles_per_batch, i // tiles_per_batch, BATCH), 0, 0)

    return pl.pallas_call(
        functools.partial(_ffn_kernel, mod0=mod0, final=final),
        grid=(n_rows // TM_FFN,),
        in_specs=[
            pl.BlockSpec((TM_FFN, D_MODEL), lambda i: (i, 0)),
            pl.BlockSpec((None, N_MOD, D_MODEL), mod_idx),
            _resident((1, D_MODEL)),
            _resident((D_MODEL, D_FF)),
            _resident((D_MODEL, D_FF)),
            _resident((D_FF, D_MODEL)),
            _resident((1, D_MODEL)),
        ],
        out_specs=pl.BlockSpec((TM_FFN, D_MODEL), lambda i: (i, 0)),
        out_shape=jax.ShapeDtypeStruct((n_rows, D_MODEL), F32),
        compiler_params=_cparams(("parallel",)),
        name="ffn",
    )(x, mods, g, w1, w3, w2, final_g)


def _tile_batch(i):
    tiles_per_batch = SEQ // TM_PROJ
    return jnp.where(i < BATCH * tiles_per_batch, i // tiles_per_batch, i - BATCH * tiles_per_batch)


def _tile_pos(i):
    tiles_per_batch = SEQ // TM_PROJ
    return jnp.where(i < BATCH * tiles_per_batch, i % tiles_per_batch, tiles_per_batch)


def _tile_mod(i):
    tiles_per_batch = SEQ // TM_PROJ
    return jnp.where(i < BATCH * tiles_per_batch, i // tiles_per_batch, BATCH)


def _rope(z, cos, sin_signed, first_of_pair):
    partner = jnp.where(first_of_pair, pltpu.roll(z, LANES - 16, 1), pltpu.roll(z, 16, 1))
    return z * cos + partner * sin_signed


def _inproj_kernel(x_ref, mod_ref, g_ref, w_ref, lb_ref, cos_ref, sin_ref,
                   qv_ref, ff_ref, fb_ref, sg_ref, at_ref):
    h = _modulated_norm(x_ref[...], g_ref[...], mod_ref[3:4, :], mod_ref[4:5, :]).astype(BF16)
    W = HG_WIDTH
    p_all = _dot(h, w_ref[...])

    def proj(lo, hi):
        return p_all[:, lo:hi]

    qv_ref[:, 0:W] = _silu(proj(0, W))
    for d, dst in ((0, ff_ref), (1, fb_ref)):
        z = proj((1 + d) * W, (2 + d) * W)
        lb = lb_ref[d:d + 1, :]
        dst[:, 0:W] = jnp.log(lb + (1.0 - lb) * jax.nn.sigmoid(z)) * LOG2_E
        dst[:, W:2 * W] = jnp.log((1.0 - lb) * jax.nn.sigmoid(-z)) * LOG2_E
    qv_ref[:, W:2 * W] = proj(3 * W, 4 * W)
    sg_ref[...] = _silu(proj(4 * W, 5 * W))

    scale = HEAD_DIM ** -0.5
    na0 = 5 * W
    at_ref[:, 0:NA_WIDTH] = (proj(na0, na0 + NA_WIDTH) * scale).astype(BF16)
    at_ref[:, NA_WIDTH:3 * NA_WIDTH] = proj(na0 + NA_WIDTH, na0 + 3 * NA_WIDTH).astype(BF16)
    sw0 = na0 + 3 * NA_WIDTH
    cos = cos_ref[...]
    sin = sin_ref[...]
    lane = lax.broadcasted_iota(jnp.int32, (TM_PROJ, LANES), 1)
    first = (lane % 32) < 16
    z = [_rope(proj(sw0 + j * LANES, sw0 + (j + 1) * LANES), cos, sin, first)
         for j in range((SW_WIDTH + SW_KV_WIDTH) // LANES)]
    low = lane < HEAD_DIM
    z[0], z[1], z[2] = (jnp.where(low, z[0], z[1]), pltpu.roll(jnp.where(low, z[2], z[0]), HEAD_DIM, 1),
                        jnp.where(low, z[1], z[2]))
    for j in range(len(z)):
        zj = z[j] * scale if j < SW_WIDTH // LANES else z[j]
        at_ref[:, 3 * NA_WIDTH + j * LANES:3 * NA_WIDTH + (j + 1) * LANES] = zj.astype(BF16)
    v0 = sw0 + SW_WIDTH + SW_KV_WIDTH
    at_ref[:, 3 * NA_WIDTH + SW_WIDTH + SW_KV_WIDTH:] = proj(v0, v0 + SW_KV_WIDTH).astype(BF16)


AT_WIDTH = 3 * NA_WIDTH + SW_WIDTH + 2 * SW_KV_WIDTH


def _inproj_call(x, mods, g, w_in, lb, cos_t, sin_t):
    def out_spec(c):
        return pl.BlockSpec((None, TM_PROJ, c), lambda i: (_tile_batch(i), _tile_pos(i), 0))

    def out_shape(c, dt):
        return jax.ShapeDtypeStruct((BATCH, SEQ_ALL, c), dt)

    return pl.pallas_call(
        _inproj_kernel,
        grid=(N_TOK // TM_PROJ,),
        in_specs=[
            pl.BlockSpec((TM_PROJ, D_MODEL), lambda i: (i, 0)),
            pl.BlockSpec((None, N_MOD, D_MODEL), lambda i: (_tile_mod(i), 0, 0)),
            pl.BlockSpec((1, D_MODEL), lambda i: (0, 0)),
            pl.BlockSpec((D_MODEL, IN_WIDTH), lambda i: (0, 0)),
            pl.BlockSpec((2, HG_WIDTH), lambda i: (0, 0)),
            pl.BlockSpec((TM_PROJ, LANES), lambda i: (_tile_pos(i), 0)),
            pl.BlockSpec((TM_PROJ, LANES), lambda i: (_tile_pos(i), 0)),
        ],
        out_specs=[out_spec(2 * HG_WIDTH), out_spec(2 * HG_WIDTH), out_spec(2 * HG_WIDTH),
                   out_spec(HG_WIDTH), out_spec(AT_WIDTH)],
        out_shape=[out_shape(2 * HG_WIDTH, F32), out_shape(2 * HG_WIDTH, F32), out_shape(2 * HG_WIDTH, F32),
                   out_shape(HG_WIDTH, F32), out_shape(AT_WIDTH, BF16)],
        compiler_params=_cparams(("parallel",)),
        name="in_proj",
    )(x, mods, g, w_in, lb, cos_t, sin_t)


N_SUB = HG_CHUNK // HG_SUB
N_PAIR = HG_HEADS // 2
HG_MINI = HG_SUB // 2


def _hgrn_consts():
    t = np.arange(HG_CHUNK)
    same = (t[:, None] // HG_SUB) == (t[None, :] // HG_SUB)
    lower = same & (t[None, :] <= t[:, None])
    upper = same & (t[None, :] >= t[:, None])
    tri = np.stack([lower, upper]).astype(np.float32)
    d = np.arange(LANES)
    head_blocks = ((d[:, None] // HEAD_DIM) == (d[None, :] // HEAD_DIM)).astype(np.float32)
    return jnp.asarray(tri, BF16), jnp.asarray(head_blocks, BF16)


def _split2(a):
    hi = a.astype(BF16)
    lo = (a - hi.astype(F32)).astype(BF16)
    return jnp.concatenate([hi, lo], axis=1)


def _bcast_rows(a, s, block):
    n = HG_CHUNK // block
    a3 = a.reshape(n, block, LANES)
    return jnp.broadcast_to(a3[:, s:s + 1, :], (n, block, LANES)).reshape(HG_CHUNK, LANES)


def _hgrn_group(qs, v, gl, lk, tri, ones_blk, st, direction):
    fwd = direction == 0
    c = _dot(tri, _split2(gl))
    yield
    cum = c[:, 0:LANES] + c[:, LANES:]
    tot = _bcast_rows(cum, HG_SUB - 1 if fwd else 0, HG_SUB)
    row = lax.broadcasted_iota(jnp.int32, (HG_CHUNK, LANES), 0)
    t_mini = row % HG_MINI
    blk = row // HG_SUB
    v16 = v.astype(BF16)
    ck = cum - lk

    o = jnp.zeros((HG_CHUNK, LANES), F32)
    for s in range(HG_MINI):
        keep = (t_mini >= s) if fwd else (t_mini <= s)
        w = jnp.where(keep, qs * jnp.exp2(cum - _bcast_rows(ck, s, HG_MINI)), 0.0)
        o = o + _dot(w.astype(BF16), ones_blk) * _bcast_rows(v, s, HG_MINI)
        yield

    later = ((row % HG_SUB) >= HG_MINI) if fwd else ((row % HG_SUB) < HG_MINI)
    edge = _bcast_rows(cum, HG_MINI - 1 if fwd else HG_MINI, HG_SUB)
    q_edge = jnp.where(later, qs * jnp.exp2(jnp.minimum(cum - edge, 0.0)), 0.0)
    k_edge = jnp.where(later, 0.0, jnp.exp2(jnp.minimum(edge - ck, 0.0)))
    a = _dot_nt(_pair_queries(q_edge.astype(BF16)), k_edge.astype(BF16))
    q_blk = lax.broadcasted_iota(jnp.int32, (2 * HG_CHUNK, HG_CHUNK), 0) % HG_CHUNK // HG_SUB
    k_blk = lax.broadcasted_iota(jnp.int32, (2 * HG_CHUNK, HG_CHUNK), 1) // HG_SUB
    yield
    a = jnp.where(q_blk == k_blk, a, 0.0)
    o = o + _pair_merge(_dot(a.astype(BF16), v16))
    yield

    qd = qs * jnp.exp2(cum)
    kd = jnp.exp2(tot - ck)
    dec = jnp.exp2(tot)
    k_exp = jnp.concatenate([jnp.where(blk == j, kd, 0.0).astype(BF16) for j in range(N_SUB)], axis=1)
    upd = _dot(v.T.astype(BF16), k_exp)
    yield
    head_mask = ones_blk.astype(F32)
    before = [None] * N_SUB
    for j in (range(N_SUB) if fwd else range(N_SUB - 1, -1, -1)):
        before[j] = st.astype(BF16)
        st = st * dec[j * HG_SUB:j * HG_SUB + 1, :] + upd[:, j * LANES:(j + 1) * LANES] * head_mask
    yield
    q_exp = jnp.concatenate([jnp.where(blk == j, qd, 0.0).astype(BF16) for j in range(N_SUB)], axis=1)
    o = o + _dot_nt(q_exp, jnp.concatenate(before, axis=1))
    return o, st


def _round_robin(generators):
    results = [None] * len(generators)
    live = list(range(len(generators)))
    while live:
        for k in list(live):
            try:
                next(generators[k])
            except StopIteration as done:
                results[k] = done.value
                live.remove(k)
    return results


def _hgrn_kernel(qvf_ref, ff_ref, qvb_ref, fb_ref, tri_ref, ones_ref, of_ref, ob_ref, st_scr):
    @pl.when(pl.program_id(0) == 0)
    def _():
        st_scr[...] = jnp.zeros_like(st_scr)

    ones_blk = ones_ref[...]
    W = HG_WIDTH
    def body(b, carry):
        chains = []
        for direction, (qv_ref, f_ref) in enumerate(((qvf_ref, ff_ref), (qvb_ref, fb_ref))):
            for hp in range(N_PAIR):
                c0 = hp * LANES
                chains.append(_hgrn_group(
                    qv_ref[b, :, c0:c0 + LANES], qv_ref[b, :, W + c0:W + c0 + LANES],
                    f_ref[b, :, c0:c0 + LANES], f_ref[b, :, W + c0:W + c0 + LANES],
                    tri_ref[direction], ones_blk, st_scr[b, direction * N_PAIR + hp], direction))
        results = _round_robin(chains)
        for direction, o_ref in enumerate((of_ref, ob_ref)):
            for hp in range(N_PAIR):
                o, st = results[direction * N_PAIR + hp]
                o_ref[b, :, hp * LANES:(hp + 1) * LANES] = o
                st_scr[b, direction * N_PAIR + hp] = st
        return carry

    lax.fori_loop(0, BATCH, body, 0)


def _hgrn_call(qv, ff, fb, tri, ones_blk):
    def chunked(a):
        return a.reshape(BATCH, N_CHUNK, HG_CHUNK, a.shape[-1])

    def fwd_idx(s):
        return jnp.where(s < N_CHUNK_CTX, N_CHUNK_LAT + s, s - N_CHUNK_CTX)

    def bwd_idx(s):
        return jnp.where(s < N_CHUNK_CTX, N_CHUNK - 1 - s, N_CHUNK - 1 - s)

    def spec(c, idx):
        return pl.BlockSpec((BATCH, None, HG_CHUNK, c), lambda s: (0, idx(s), 0, 0))

    out_sds = jax.ShapeDtypeStruct((BATCH, N_CHUNK, HG_CHUNK, HG_WIDTH), F32)
    o_f, o_b = pl.pallas_call(
        _hgrn_kernel,
        grid=(N_CHUNK,),
        in_specs=[
            spec(2 * HG_WIDTH, fwd_idx), spec(2 * HG_WIDTH, fwd_idx),
            spec(2 * HG_WIDTH, bwd_idx), spec(2 * HG_WIDTH, bwd_idx),
            pl.BlockSpec((2, HG_CHUNK, HG_CHUNK), lambda s: (0, 0, 0)),
            pl.BlockSpec((LANES, LANES), lambda s: (0, 0)),
        ],
        out_specs=[spec(HG_WIDTH, fwd_idx), spec(HG_WIDTH, bwd_idx)],
        out_shape=[out_sds, out_sds],
        scratch_shapes=[pltpu.VMEM((BATCH, 2 * N_PAIR, LANES, LANES), F32)],
        compiler_params=_cparams(("arbitrary",)),
        name="hgrn2",
    )(chunked(qv), chunked(ff), chunked(qv), chunked(fb), tri, ones_blk)
    return o_f.reshape(BATCH, SEQ_ALL, HG_WIDTH), o_b.reshape(BATCH, SEQ_ALL, HG_WIDTH)


def _pair_queries(q):
    lane = lax.broadcasted_iota(jnp.int32, q.shape, 1)
    zero = jnp.zeros_like(q)
    return jnp.concatenate([jnp.where(lane < HEAD_DIM, q, zero), jnp.where(lane >= HEAD_DIM, q, zero)], axis=0)


def _pair_merge(o):
    m = o.shape[0] // 2
    lane = lax.broadcasted_iota(jnp.int32, (m, LANES), 1)
    return jnp.where(lane < HEAD_DIM, o[0:m], o[m:])


def _attend(q, keys, biases, values, extra=None):
    def lane_chunks(s):
        return [s[:, c:c + LANES] for c in range(0, s.shape[1], LANES)]

    q2 = _pair_queries(q)
    scores = []
    for k, b in zip(keys, biases):
        s = _dot_nt(q2, k)
        scores.append(s if b is None else s + b)
        yield
    m = functools.reduce(jnp.maximum, [c for s in scores for c in lane_chunks(s)])
    m = jnp.max(m, axis=-1, keepdims=True)
    if extra is not None:
        m = jnp.maximum(m, extra)
    yield
    acc = None
    esum = None
    for s, v in zip(scores, values):
        e = jnp.exp(s - m)
        part = functools.reduce(jnp.add, lane_chunks(e))
        esum = part if esum is None else esum + part
        pv = _dot(e.astype(BF16), v)
        acc = pv if acc is None else acc + pv
        yield
    denom = jnp.sum(esum, axis=-1, keepdims=True)
    if extra is not None:
        denom = denom + jnp.exp(extra - m)
    return _pair_merge(acc / denom)


def _na_kernel(q_ref, k_ref, v_ref, kc_ref, vc_ref, bias_ref, o_ref):
    j = pl.program_id(1)

    @pl.when(j < ATT_STEPS)
    def _():
        units = []
        for rr in range(ROWS_PER_STEP):
            r = j * ROWS_PER_STEP + rr
            start = jnp.clip(r - NA_ROWS // 2, 0, GRID_W - NA_ROWS)
            case = jnp.where(r < NA_ROWS // 2, r,
                             jnp.where(r <= GRID_W - NA_ROWS // 2, NA_ROWS // 2, r - (GRID_W - NA_ROWS)))
            k0 = pl.multiple_of(start * GRID_W, GRID_W)
            for p in range(NA_HEADS // 2):
                c = slice(p * LANES, (p + 1) * LANES)
                bias = jnp.concatenate([bias_ref[case, 2 * p], bias_ref[case, 2 * p + 1]], axis=0)
                units.append(_attend(q_ref[rr * GRID_W:(rr + 1) * GRID_W, c],
                                     [k_ref[pl.ds(k0, NA_KEYS), c], kc_ref[:, c]], [bias, None],
                                     [v_ref[pl.ds(k0, NA_KEYS), c], vc_ref[:, c]]))
        for i, o in enumerate(_round_robin(units)):
            rr, p = divmod(i, NA_HEADS // 2)
            o_ref[rr * GRID_W:(rr + 1) * GRID_W, p * LANES:(p + 1) * LANES] = o.astype(BF16)

    @pl.when(j == ATT_STEPS)
    def _():
        units = [_attend(q_ref[:, p * LANES:(p + 1) * LANES], [kc_ref[:, p * LANES:(p + 1) * LANES]], [None],
                         [vc_ref[:, p * LANES:(p + 1) * LANES]]) for p in range(NA_HEADS // 2)]
        for p, o in enumerate(_round_robin(units)):
            o_ref[:, p * LANES:(p + 1) * LANES] = o.astype(BF16)


def _att_steps(need_ctx):
    return ATT_STEPS + 1 if need_ctx else ATT_STEPS


def _na_call(at, bias, need_ctx):
    w = NA_WIDTH
    return pl.pallas_call(
        _na_kernel,
        grid=(BATCH, _att_steps(need_ctx)),
        in_specs=[
            pl.BlockSpec((None, TM_PROJ, w), lambda b, j: (b, j, 0)),
            pl.BlockSpec((None, SEQ, w), lambda b, j: (b, 0, 1)),
            pl.BlockSpec((None, SEQ, w), lambda b, j: (b, 0, 2)),
            pl.BlockSpec((None, CTX_LEN, w), lambda b, j: (b, SEQ // CTX_LEN, 1)),
            pl.BlockSpec((None, CTX_LEN, w), lambda b, j: (b, SEQ // CTX_LEN, 2)),
            pl.BlockSpec((NA_CASES, NA_HEADS, GRID_W, NA_KEYS), lambda b, j: (0, 0, 0, 0), pipeline_mode=pl.Buffered(1)),
        ],
        out_specs=pl.BlockSpec((None, TM_PROJ, w), lambda b, j: (b, j, 0)),
        out_shape=jax.ShapeDtypeStruct((BATCH, SEQ_ALL, w), BF16),
        compiler_params=_cparams(("parallel", "arbitrary")),
        name="nbr_attn",
    )(at, at, at, at, at, bias)


def _na_bias_table(rpb):
    half = NA_ROWS // 2
    case_row = np.concatenate([np.arange(half), [half], np.arange(GRID_W - half + 1, GRID_W)])
    start = np.clip(case_row - half, 0, GRID_W - NA_ROWS)
    d_row = start[:, None] + np.arange(NA_ROWS)[None, :] - case_row[:, None]
    col = np.arange(GRID_W)
    c0 = np.clip(col - NA_COLS // 2, 0, GRID_W - NA_COLS)
    col_ok = (col[None, :] >= c0[:, None]) & (col[None, :] < c0[:, None] + NA_COLS)
    d_col = np.clip(col[None, :] - col[:, None], 1 - NA_COLS, NA_COLS - 1)
    row_sel = (d_row[:, :, None] + NA_ROWS - 1 == np.arange(2 * NA_ROWS - 1)).astype(np.float32)
    col_sel = (d_col[None, :, :] + NA_COLS - 1 == np.arange(2 * NA_COLS - 1)[:, None, None]).astype(np.float32)
    rows = jnp.einsum('cir,hrd->chid', row_sel, rpb.astype(F32), precision=lax.Precision.HIGHEST)
    b = jnp.einsum('chid,dqk->chqik', rows, col_sel, precision=lax.Precision.HIGHEST)
    b = jnp.where(col_ok[None, None, :, None, :], b, MASK_VALUE)
    return b.reshape(NA_CASES, NA_HEADS, GRID_W, NA_KEYS)


def _sw_kernel(sink_ref, q_ref, k_ref, v_ref, kc_ref, vc_ref, o_ref):
    j = pl.program_id(1)
    n_pair = SW_HEADS // 2

    def sink_col(p, m):
        row = lax.broadcasted_iota(jnp.int32, (2 * m, 1), 0)
        return jnp.where(row < m, sink_ref[p], sink_ref[p + n_pair])

    def store_pair(rows, p, o):
        ob = o.astype(BF16)
        o_ref[rows, p * HEAD_DIM:(p + 1) * HEAD_DIM] = ob[:, :HEAD_DIM]
        o_ref[rows, (p + n_pair) * HEAD_DIM:(p + n_pair + 1) * HEAD_DIM] = ob[:, HEAD_DIM:]

    @pl.when(j < ATT_STEPS)
    def _():
        units = []
        for u in range(TM_PROJ // SW_BLOCK):
            n = j * (TM_PROJ // SW_BLOCK) + u
            start = jnp.clip(n * SW_BLOCK - SW_BLOCK, 0, SEQ - SW_KEYS)
            k0 = pl.multiple_of(start, SW_BLOCK)
            rel = (n * SW_BLOCK - start
                   + lax.broadcasted_iota(jnp.int32, (SW_BLOCK, SW_KEYS), 0)
                   - lax.broadcasted_iota(jnp.int32, (SW_BLOCK, SW_KEYS), 1))
            band = jnp.where(jnp.abs(rel) <= SW_WINDOW, 0.0, MASK_VALUE).astype(F32)
            band2 = jnp.concatenate([band, band], axis=0)
            kw = k_ref[pl.ds(k0, SW_KEYS), :]
            vw = v_ref[pl.ds(k0, SW_KEYS), :]
            for p in range(n_pair):
                units.append(_attend(q_ref[u * SW_BLOCK:(u + 1) * SW_BLOCK, p * LANES:(p + 1) * LANES],
                                     [kw, kc_ref[...]], [band2, None], [vw, vc_ref[...]], extra=sink_col(p, SW_BLOCK)))
        for i, o in enumerate(_round_robin(units)):
            u, p = divmod(i, n_pair)
            store_pair(slice(u * SW_BLOCK, (u + 1) * SW_BLOCK), p, o)

    @pl.when(j == ATT_STEPS)
    def _():
        units = [_attend(q_ref[:, p * LANES:(p + 1) * LANES], [kc_ref[...]], [None], [vc_ref[...]],
                         extra=sink_col(p, TM_PROJ)) for p in range(n_pair)]
        for p, o in enumerate(_round_robin(units)):
            store_pair(slice(None), p, o)


def _sw_call(at, sink_perm, need_ctx):
    q_blk = 3 * NA_WIDTH // SW_WIDTH
    k_blk = (3 * NA_WIDTH + SW_WIDTH) // SW_KV_WIDTH
    grid_spec = pltpu.PrefetchScalarGridSpec(
        num_scalar_prefetch=1,
        grid=(BATCH, _att_steps(need_ctx)),
        in_specs=[
            pl.BlockSpec((None, TM_PROJ, SW_WIDTH), lambda b, j, s: (b, j, q_blk)),
            pl.BlockSpec((None, SEQ, SW_KV_WIDTH), lambda b, j, s: (b, 0, k_blk)),
            pl.BlockSpec((None, SEQ, SW_KV_WIDTH), lambda b, j, s: (b, 0, k_blk + 1)),
            pl.BlockSpec((None, CTX_LEN, SW_KV_WIDTH), lambda b, j, s: (b, SEQ // CTX_LEN, k_blk)),
            pl.BlockSpec((None, CTX_LEN, SW_KV_WIDTH), lambda b, j, s: (b, SEQ // CTX_LEN, k_blk + 1)),
        ],
        out_specs=pl.BlockSpec((None, TM_PROJ, SW_WIDTH), lambda b, j, s: (b, j, 0)),
    )
    return pl.pallas_call(
        _sw_kernel,
        grid_spec=grid_spec,
        out_shape=jax.ShapeDtypeStruct((BATCH, SEQ_ALL, SW_WIDTH), BF16),
        compiler_params=_cparams(("parallel", "arbitrary")),
        name="win_attn",
    )(sink_perm, at, at, at, at, at)


def _outproj_kernel(x_ref, mod_ref, of_ref, ob_ref, sg_ref, na_ref, sw_ref, w_ref, ng_ref, ones_ref, o_ref):
    o = of_ref[...] + ob_ref[...]
    ms = jnp.concatenate(
        [_dot((o[:, c:c + LANES] * o[:, c:c + LANES]).astype(BF16), ones_ref[...]) for c in range(0, HG_WIDTH, LANES)],
        axis=1) * (1.0 / HEAD_DIM)
    hg = (o * lax.rsqrt(ms + EPS)) * ng_ref[...] * sg_ref[...]
    y = _dot(hg.astype(BF16), w_ref[0:HG_WIDTH, :])
    y = y + _dot(na_ref[...], w_ref[HG_WIDTH:HG_WIDTH + NA_WIDTH, :])
    y = y + _dot(sw_ref[...], w_ref[HG_WIDTH + NA_WIDTH:, :])
    o_ref[...] = x_ref[...] + mod_ref[5:6, :] * y


def _outproj_call(x, mods, o_f, o_b, sg, o_na, o_sw, w_out, ng, ones_blk, latent_only):
    n_rows = N_LAT if latent_only else N_TOK

    def per_batch(c):
        return pl.BlockSpec((None, TM_PROJ, c), lambda i: (_tile_batch(i), _tile_pos(i), 0))

    return pl.pallas_call(
        _outproj_kernel,
        grid=(n_rows // TM_PROJ,),
        in_specs=[
            pl.BlockSpec((TM_PROJ, D_MODEL), lambda i: (i, 0)),
            pl.BlockSpec((None, N_MOD, D_MODEL), lambda i: (_tile_mod(i), 0, 0)),
            per_batch(HG_WIDTH), per_batch(HG_WIDTH), per_batch(HG_WIDTH),
            per_batch(NA_WIDTH), per_batch(SW_WIDTH),
            pl.BlockSpec((D_MODEL, D_MODEL), lambda i: (0, 0)),
            pl.BlockSpec((1, HG_WIDTH), lambda i: (0, 0)),
            pl.BlockSpec((LANES, LANES), lambda i: (0, 0)),
        ],
        out_specs=pl.BlockSpec((TM_PROJ, D_MODEL), lambda i: (i, 0)),
        out_shape=jax.ShapeDtypeStruct((n_rows, D_MODEL), F32),
        compiler_params=_cparams(("parallel",)),
        name="out_proj",
    )(x, mods, o_f, o_b, sg, o_na, o_sw, w_out, ng, ones_blk)


def _rope_tables():
    pos = jnp.arange(SEQ)
    pos = jnp.stack([pos // GRID_W, pos % GRID_W], axis=-1).astype(F32)
    nf = HEAD_DIM // 4
    inv = ROPE_THETA ** (-jnp.arange(nf, dtype=F32) / nf)
    ang = pos[:, :, None] * inv
    cos, sin = jnp.cos(ang), jnp.sin(ang)
    cos_h = jnp.stack([cos, cos], axis=2).reshape(SEQ, HEAD_DIM)
    sin_h = jnp.stack([-sin, sin], axis=2).reshape(SEQ, HEAD_DIM)
    reps = LANES // HEAD_DIM
    cos_t = jnp.concatenate([jnp.tile(cos_h, (1, reps)), jnp.ones((CTX_LEN, LANES), F32)], axis=0)
    sin_t = jnp.concatenate([jnp.tile(sin_h, (1, reps)), jnp.zeros((CTX_LEN, LANES), F32)], axis=0)
    return cos_t, sin_t


def kernel(x, c, ctx, c_ctx, ada_w, ada_b, norm_g, ffn_w1, ffn_w3, ffn_w2, w_in, w_out,
           hg_lb_logits, hg_norm_g, na_rpb, sw_sink, final_g):
    lb_soft = jax.nn.softmax(hg_lb_logits.astype(F32), axis=0)
    lower_bounds = jnp.cumsum(lb_soft, axis=0) - lb_soft[0]
    w_in_p = w_in.astype(BF16)
    w_out_p = w_out.astype(BF16)
    sink_p = sw_sink.astype(F32)
    w1 = ffn_w1.astype(BF16)
    w3 = ffn_w3.astype(BF16)
    w2 = ffn_w2.astype(BF16)
    cos_t, sin_t = _rope_tables()
    tri, ones_blk = _hgrn_consts()
    final_g2 = final_g.reshape(1, D_MODEL)

    c8 = jnp.concatenate([c, c_ctx[None, :], jnp.zeros((8 - BATCH - 1, D_MODEL), F32)], axis=0)
    mods_all = _ada_call(c8, ada_w, ada_b).reshape(DEPTH, 8, N_MOD, D_MODEL)

    xs = jnp.concatenate([x.reshape(N_LAT, D_MODEL), ctx.reshape(N_CTX, D_MODEL)], axis=0)
    for l in range(DEPTH):
        need_ctx = l < DEPTH - 1
        mods = mods_all[l]
        xs = _ffn_call(xs, mods, norm_g[l, 0:1], w1[l, 0], w3[l, 0], w2[l, 0], final_g2,
                       mod0=0, latent_only=False, final=False)
        qv, ff, fb, sg, at = _inproj_call(xs, mods, norm_g[l, 1:2], w_in_p[l], lower_bounds[l], cos_t, sin_t)
        o_f, o_b = _hgrn_call(qv, ff, fb, tri, ones_blk)
        o_na = _na_call(at, _na_bias_table(na_rpb[l]), need_ctx)
        o_sw = _sw_call(at, sink_p[l], need_ctx)
        xs = _outproj_call(xs, mods, o_f, o_b, sg, o_na, o_sw, w_out_p[l], hg_norm_g[l:l + 1], ones_blk,
                           latent_only=not need_ctx)
        xs = _ffn_call(xs, mods, norm_g[l, 2:3], w1[l, 1], w3[l, 1], w2[l, 1], final_g2,
                       mod0=6, latent_only=not need_ctx, final=not need_ctx)
    return xs.reshape(BATCH, SEQ, D_MODEL)
```

```python
import functools

import jax
import jax.numpy as jnp
import numpy as np
from jax import lax
from jax.experimental import pallas as pl
from jax.experimental.pallas import tpu as pltpu

F32 = jnp.float32
BF16 = jnp.bfloat16

D_MODEL = 1024
BATCH = 4
SEQ = 4096
DEPTH = 4
GRID_W = 64
CTX_LEN = 256
HEAD_DIM = 64
EPS = 1e-6
MASK_VALUE = -1e30
LOG2_E = 1.4426950408889634
ROPE_THETA = 10000.0
N_MOD = 9
D_FF = 2816
HG_WIDTH = 256
HG_HEADS = 4
NA_WIDTH = 384
NA_HEADS = 6
NA_ROWS = 8
NA_COLS = 16
SW_WIDTH = 384
SW_HEADS = 6
SW_KV_WIDTH = 128
SW_WINDOW = 128
SW_BLOCK = 128
IN_WIDTH = 3072

LANES = 128
VMEM_LIMIT_BYTES = 56 * 1024 * 1024

N_LAT = BATCH * SEQ
N_CTX = BATCH * CTX_LEN
N_TOK = N_LAT + N_CTX
SEQ_ALL = SEQ + CTX_LEN
TM_PROJ = 256
TM_FFN = 512
TM_LAT = 1024
TF_FFN = D_FF
ROWS_PER_STEP = TM_PROJ // GRID_W
ATT_STEPS = SEQ // TM_PROJ
HG_CHUNK = 128
HG_SUB = 16
N_CHUNK_LAT = SEQ // HG_CHUNK
N_CHUNK_CTX = CTX_LEN // HG_CHUNK
N_CHUNK = N_CHUNK_LAT + N_CHUNK_CTX
NA_KEYS = NA_ROWS * GRID_W
NA_CASES = 8
SW_KEYS = 3 * SW_BLOCK


def _cparams(sem):
    return pltpu.CompilerParams(dimension_semantics=sem, vmem_limit_bytes=VMEM_LIMIT_BYTES)


def _silu(a):
    return a * jax.nn.sigmoid(a)


def _dot(a, b):
    return jnp.dot(a, b, preferred_element_type=F32)


def _dot_nt(a, b):
    return lax.dot_general(a, b, (((1,), (1,)), ((), ())), preferred_element_type=F32)


def _modulated_norm(x, g, shift, scale):
    ms = jnp.mean(x * x, axis=-1, keepdims=True)
    return (x * lax.rsqrt(ms + EPS)) * g * (1.0 + scale) + shift


ADA_TN = 1536


def _ada_kernel(c_ref, w_ref, b_ref, o_ref):
    s = _silu(c_ref[...]).astype(BF16)
    o_ref[...] = _dot(s, w_ref[...].astype(BF16)) + b_ref[...]


def _ada_call(c8, ada_w, ada_b):
    n_out = N_MOD * D_MODEL
    return pl.pallas_call(
        _ada_kernel,
        grid=(DEPTH, n_out // ADA_TN),
        in_specs=[
            pl.BlockSpec((8, D_MODEL), lambda l, j: (0, 0)),
            pl.BlockSpec((None, D_MODEL, ADA_TN), lambda l, j: (l, 0, j)),
            pl.BlockSpec((None, 1, ADA_TN), lambda l, j: (l, 0, j)),
        ],
        out_specs=pl.BlockSpec((None, 8, ADA_TN), lambda l, j: (l, 0, j)),
        out_shape=jax.ShapeDtypeStruct((DEPTH, 8, n_out), F32),
        compiler_params=_cparams(("parallel", "parallel")),
        name="ada_mod",
    )(c8, ada_w, ada_b.reshape(DEPTH, 1, n_out))


def _ffn_kernel(x_ref, mod_ref, g_ref, w1_ref, w3_ref, w2_ref, fg_ref, o_ref, *, mod0, final):
    x = x_ref[...]
    h = _modulated_norm(x, g_ref[...], mod_ref[mod0:mod0 + 1, :], mod_ref[mod0 + 1:mod0 + 2, :]).astype(BF16)
    acc = None
    for c in range(D_FF // TF_FFN):
        lo = c * TF_FFN
        a = _silu(_dot(h, w1_ref[:, lo:lo + TF_FFN])) * _dot(h, w3_ref[:, lo:lo + TF_FFN])
        part = _dot(a.astype(BF16), w2_ref[lo:lo + TF_FFN, :])
        acc = part if acc is None else acc + part
    y = x + (0.5 * mod_ref[mod0 + 2:mod0 + 3, :]) * acc
    if final:
        ms = jnp.mean(y * y, axis=-1, keepdims=True)
        y = (y * lax.rsqrt(ms + EPS)) * fg_ref[...]
    o_ref[...] = y


def _resident(shape, lead=()):
    return pl.BlockSpec((None,) * len(lead) + tuple(shape), lambda *_: tuple(lead) + (0,) * len(shape),
                        pipeline_mode=pl.Buffered(1))


def _ffn_call(x, mods_all, norm_g, w1, w3, w2, final_g, *, layer, which, latent_only, final):
    n_rows = N_LAT if latent_only else N_TOK
    tiles_per_batch = SEQ // TM_FFN

    def mod_idx(i):
        return (layer, jnp.where(i < BATCH * tiles_per_batch, i // tiles_per_batch, BATCH), 0, 0)

    return pl.pallas_call(
        functools.partial(_ffn_kernel, mod0=6 * which, final=final),
        grid=(n_rows // TM_FFN,),
        in_specs=[
            pl.BlockSpec((TM_FFN, D_MODEL), lambda i: (i, 0)),
            pl.BlockSpec((None, None, N_MOD, D_MODEL), mod_idx),
            _resident((1, D_MODEL), (layer, 2 * which)),
            _resident((D_MODEL, D_FF), (layer, which)),
            _resident((D_MODEL, D_FF), (layer, which)),
            _resident((D_FF, D_MODEL), (layer, which)),
            _resident((1, D_MODEL)),
        ],
        out_specs=pl.BlockSpec((TM_FFN, D_MODEL), lambda i: (i, 0)),
        out_shape=jax.ShapeDtypeStruct((n_rows, D_MODEL), F32),
        compiler_params=_cparams(("parallel",)),
        name="ffn",
    )(x, mods_all, norm_g, w1, w3, w2, final_g)


def _tile_batch(i):
    tiles_per_batch = SEQ // TM_PROJ
    return jnp.where(i < BATCH * tiles_per_batch, i // tiles_per_batch, i - BATCH * tiles_per_batch)


def _tile_pos(i):
    tiles_per_batch = SEQ // TM_PROJ
    return jnp.where(i < BATCH * tiles_per_batch, i % tiles_per_batch, tiles_per_batch)


def _tile_mod(i):
    tiles_per_batch = SEQ // TM_PROJ
    return jnp.where(i < BATCH * tiles_per_batch, i // tiles_per_batch, BATCH)


def _rope(z, cos, sin_signed, first_of_pair):
    partner = jnp.where(first_of_pair, pltpu.roll(z, LANES - 16, 1), pltpu.roll(z, 16, 1))
    return z * cos + partner * sin_signed


def _inproj_kernel(x_ref, *refs):
    _round_robin([_inproj_rows(pl.ds(r0, TM_PROJ), x_ref, *refs) for r0 in range(0, x_ref.shape[0], TM_PROJ)])


def _inproj_rows(rows, x_ref, mod_ref, g_ref, w_ref, lb_ref, cos_ref, sin_ref,
                 qv_ref, ff_ref, fb_ref, sg_ref, at_ref):
    h = _modulated_norm(x_ref[rows, :], g_ref[...], mod_ref[3:4, :], mod_ref[4:5, :]).astype(BF16)
    W = HG_WIDTH
    p_all = _dot(h, w_ref[...])
    yield

    def proj(lo, hi):
        return p_all[:, lo:hi]

    qv_ref[rows, 0:W] = _silu(proj(0, W))
    for d, dst in ((0, ff_ref), (1, fb_ref)):
        z = proj((1 + d) * W, (2 + d) * W)
        lb = lb_ref[d:d + 1, :]
        dst[rows, 0:W] = jnp.log(lb + (1.0 - lb) * jax.nn.sigmoid(z)) * LOG2_E
        dst[rows, W:2 * W] = jnp.log((1.0 - lb) * jax.nn.sigmoid(-z)) * LOG2_E
    qv_ref[rows, W:2 * W] = proj(3 * W, 4 * W)
    sg_ref[rows, :] = _silu(proj(4 * W, 5 * W))
    yield

    scale = HEAD_DIM ** -0.5
    na0 = 5 * W
    at_ref[rows, 0:NA_WIDTH] = (proj(na0, na0 + NA_WIDTH) * scale).astype(BF16)
    at_ref[rows, NA_WIDTH:3 * NA_WIDTH] = proj(na0 + NA_WIDTH, na0 + 3 * NA_WIDTH).astype(BF16)
    yield
    sw0 = na0 + 3 * NA_WIDTH
    cos = cos_ref[rows, :]
    sin = sin_ref[rows, :]
    lane = lax.broadcasted_iota(jnp.int32, (TM_PROJ, LANES), 1)
    first = (lane % 32) < 16
    z = [_rope(proj(sw0 + j * LANES, sw0 + (j + 1) * LANES), cos, sin, first)
         for j in range((SW_WIDTH + SW_KV_WIDTH) // LANES)]
    low = lane < HEAD_DIM
    z[0], z[1], z[2] = (jnp.where(low, z[0], z[1]), pltpu.roll(jnp.where(low, z[2], z[0]), HEAD_DIM, 1),
                        jnp.where(low, z[1], z[2]))
    for j in range(len(z)):
        zj = z[j] * scale if j < SW_WIDTH // LANES else z[j]
        at_ref[rows, 3 * NA_WIDTH + j * LANES:3 * NA_WIDTH + (j + 1) * LANES] = zj.astype(BF16)
    v0 = sw0 + SW_WIDTH + SW_KV_WIDTH
    at_ref[rows, 3 * NA_WIDTH + SW_WIDTH + SW_KV_WIDTH:] = proj(v0, v0 + SW_KV_WIDTH).astype(BF16)


AT_WIDTH = 3 * NA_WIDTH + SW_WIDTH + 2 * SW_KV_WIDTH


N_INPROJ_IN = 7
INPROJ_WIDTHS = (2 * HG_WIDTH, 2 * HG_WIDTH, 2 * HG_WIDTH, HG_WIDTH, AT_WIDTH)
INPROJ_DTYPES = (F32, F32, F32, F32, BF16)


def _inproj_ctx_kernel(*refs):
    _inproj_kernel(*refs[:N_INPROJ_IN], *refs[N_INPROJ_IN + len(INPROJ_WIDTHS):])


def _inproj_call(x, mods_all, norm_g, w_in, lb, cos_t, sin_t, *, layer):
    widths = INPROJ_WIDTHS
    out_shape = [jax.ShapeDtypeStruct((BATCH, SEQ_ALL, c), dt) for c, dt in zip(widths, INPROJ_DTYPES)]
    params = [_resident((1, D_MODEL), (layer, 1)), _resident((D_MODEL, IN_WIDTH), (layer,)),
              _resident((2, HG_WIDTH), (layer,))]
    lat_tiles = SEQ // TM_LAT

    latent = pl.pallas_call(
        _inproj_kernel,
        grid=(N_LAT // TM_LAT,),
        in_specs=[
            pl.BlockSpec((TM_LAT, D_MODEL), lambda i: (i, 0)),
            pl.BlockSpec((None, None, N_MOD, D_MODEL), lambda i: (layer, i // lat_tiles, 0, 0)),
            *params,
            pl.BlockSpec((TM_LAT, LANES), lambda i: (i % lat_tiles, 0)),
            pl.BlockSpec((TM_LAT, LANES), lambda i: (i % lat_tiles, 0)),
        ],
        out_specs=[pl.BlockSpec((None, TM_LAT, c), lambda i: (i // lat_tiles, i % lat_tiles, 0)) for c in widths],
        out_shape=out_shape,
        compiler_params=_cparams(("parallel",)),
        name="in_proj",
    )(x, mods_all, norm_g, w_in, lb, cos_t, sin_t)

    ctx_pos = SEQ // TM_PROJ
    return pl.pallas_call(
        _inproj_ctx_kernel,
        grid=(BATCH,),
        in_specs=[
            pl.BlockSpec((TM_PROJ, D_MODEL), lambda b: (N_LAT // TM_PROJ + b, 0)),
            pl.BlockSpec((None, None, N_MOD, D_MODEL), lambda b: (layer, BATCH, 0, 0)),
            *params,
            pl.BlockSpec((TM_PROJ, LANES), lambda b: (ctx_pos, 0)),
            pl.BlockSpec((TM_PROJ, LANES), lambda b: (ctx_pos, 0)),
            *[pl.BlockSpec(memory_space=pl.ANY) for _ in widths],
        ],
        out_specs=[pl.BlockSpec((None, TM_PROJ, c), lambda b: (b, ctx_pos, 0)) for c in widths],
        out_shape=out_shape,
        input_output_aliases={N_INPROJ_IN + k: k for k in range(len(widths))},
        compiler_params=_cparams(("parallel",)),
        name="in_proj_ctx",
    )(x, mods_all, norm_g, w_in, lb, cos_t, sin_t, *latent)


N_SUB = HG_CHUNK // HG_SUB
N_PAIR = HG_HEADS // 2
HG_MINI = HG_SUB // 2


def _hgrn_consts():
    t = np.arange(HG_CHUNK)
    same = (t[:, None] // HG_SUB) == (t[None, :] // HG_SUB)
    lower = same & (t[None, :] <= t[:, None])
    upper = same & (t[None, :] >= t[:, None])
    tri = np.stack([lower, upper]).astype(np.float32)
    d = np.arange(LANES)
    head_blocks = ((d[:, None] // HEAD_DIM) == (d[None, :] // HEAD_DIM)).astype(np.float32)
    return jnp.asarray(tri, BF16), jnp.asarray(head_blocks, BF16)


def _split2(a):
    hi = a.astype(BF16)
    lo = (a - hi.astype(F32)).astype(BF16)
    return jnp.concatenate([hi, lo], axis=1)


def _bcast_rows(a, s, block):
    n = HG_CHUNK // block
    a3 = a.reshape(n, block, LANES)
    return jnp.broadcast_to(a3[:, s:s + 1, :], (n, block, LANES)).reshape(HG_CHUNK, LANES)


def _hgrn_group(qs, v, gl, lk, tri, ones_blk, st, direction):
    fwd = direction == 0
    c = _dot(tri, _split2(gl))
    yield
    cum = c[:, 0:LANES] + c[:, LANES:]
    tot = _bcast_rows(cum, HG_SUB - 1 if fwd else 0, HG_SUB)
    row = lax.broadcasted_iota(jnp.int32, (HG_CHUNK, LANES), 0)
    t_mini = row % HG_MINI
    blk = row // HG_SUB
    v16 = v.astype(BF16)
    ck = cum - lk

    o = jnp.zeros((HG_CHUNK, LANES), F32)
    for s in range(HG_MINI):
        keep = (t_mini >= s) if fwd else (t_mini <= s)
        w = jnp.where(keep, qs * jnp.exp2(cum - _bcast_rows(ck, s, HG_MINI)), 0.0)
        o = o + _dot(w.astype(BF16), ones_blk) * _bcast_rows(v, s, HG_MINI)
        yield

    later = ((row % HG_SUB) >= HG_MINI) if fwd else ((row % HG_SUB) < HG_MINI)
    edge = _bcast_rows(cum, HG_MINI - 1 if fwd else HG_MINI, HG_SUB)
    q_edge = jnp.where(later, qs * jnp.exp2(jnp.minimum(cum - edge, 0.0)), 0.0)
    k_edge = jnp.where(later, 0.0, jnp.exp2(jnp.minimum(edge - ck, 0.0)))
    a = _dot_nt(_pair_queries(q_edge.astype(BF16)), k_edge.astype(BF16))
    q_blk = lax.broadcasted_iota(jnp.int32, (2 * HG_CHUNK, HG_CHUNK), 0) % HG_CHUNK // HG_SUB
    k_blk = lax.broadcasted_iota(jnp.int32, (2 * HG_CHUNK, HG_CHUNK), 1) // HG_SUB
    yield
    a = jnp.where(q_blk == k_blk, a, 0.0)
    o = o + _pair_merge(_dot(a.astype(BF16), v16))
    yield

    qd = qs * jnp.exp2(cum)
    kd = jnp.exp2(tot - ck)
    dec = jnp.exp2(tot)
    k_exp = jnp.concatenate([jnp.where(blk == j, kd, 0.0).astype(BF16) for j in range(N_SUB)], axis=1)
    upd = _dot(v.T.astype(BF16), k_exp)
    yield
    head_mask = ones_blk.astype(F32)
    before = [None] * N_SUB
    for j in (range(N_SUB) if fwd else range(N_SUB - 1, -1, -1)):
        before[j] = st.astype(BF16)
        st = st * dec[j * HG_SUB:j * HG_SUB + 1, :] + upd[:, j * LANES:(j + 1) * LANES] * head_mask
    yield
    q_exp = jnp.concatenate([jnp.where(blk == j, qd, 0.0).astype(BF16) for j in range(N_SUB)], axis=1)
    o = o + _dot_nt(q_exp, jnp.concatenate(before, axis=1))
    return o, st


def _round_robin(generators):
    results = [None] * len(generators)
    live = list(range(len(generators)))
    while live:
        for k in list(live):
            try:
                next(generators[k])
            except StopIteration as done:
                results[k] = done.value
                live.remove(k)
    return results


def _hgrn_kernel(qvf_ref, ff_ref, qvb_ref, fb_ref, tri_ref, ones_ref, of_ref, ob_ref, st_scr):
    @pl.when(pl.program_id(0) == 0)
    def _():
        st_scr[...] = jnp.zeros_like(st_scr)

    ones_blk = ones_ref[...]
    W = HG_WIDTH
    def body(b, carry):
        chains = []
        for direction, (qv_ref, f_ref) in enumerate(((qvf_ref, ff_ref), (qvb_ref, fb_ref))):
            for hp in range(N_PAIR):
                c0 = hp * LANES
                chains.append(_hgrn_group(
                    qv_ref[b, :, c0:c0 + LANES], qv_ref[b, :, W + c0:W + c0 + LANES],
                    f_ref[b, :, c0:c0 + LANES], f_ref[b, :, W + c0:W + c0 + LANES],
                    tri_ref[direction], ones_blk, st_scr[b, direction * N_PAIR + hp], direction))
        results = _round_robin(chains)
        for direction, o_ref in enumerate((of_ref, ob_ref)):
            for hp in range(N_PAIR):
                o, st = results[direction * N_PAIR + hp]
                o_ref[b, :, hp * LANES:(hp + 1) * LANES] = o
                st_scr[b, direction * N_PAIR + hp] = st
        return carry

    lax.fori_loop(0, BATCH, body, 0)


def _hgrn_call(qv, ff, fb, tri, ones_blk):
    def chunked(a):
        return a.reshape(BATCH, N_CHUNK, HG_CHUNK, a.shape[-1])

    def fwd_idx(s):
        return jnp.where(s < N_CHUNK_CTX, N_CHUNK_LAT + s, s - N_CHUNK_CTX)

    def bwd_idx(s):
        return jnp.where(s < N_CHUNK_CTX, N_CHUNK - 1 - s, N_CHUNK - 1 - s)

    def spec(c, idx):
        return pl.BlockSpec((BATCH, None, HG_CHUNK, c), lambda s: (0, idx(s), 0, 0))

    out_sds = jax.ShapeDtypeStruct((BATCH, N_CHUNK, HG_CHUNK, HG_WIDTH), F32)
    o_f, o_b = pl.pallas_call(
        _hgrn_kernel,
        grid=(N_CHUNK,),
        in_specs=[
            spec(2 * HG_WIDTH, fwd_idx), spec(2 * HG_WIDTH, fwd_idx),
            spec(2 * HG_WIDTH, bwd_idx), spec(2 * HG_WIDTH, bwd_idx),
            pl.BlockSpec((2, HG_CHUNK, HG_CHUNK), lambda s: (0, 0, 0)),
            pl.BlockSpec((LANES, LANES), lambda s: (0, 0)),
        ],
        out_specs=[spec(HG_WIDTH, fwd_idx), spec(HG_WIDTH, bwd_idx)],
        out_shape=[out_sds, out_sds],
        scratch_shapes=[pltpu.VMEM((BATCH, 2 * N_PAIR, LANES, LANES), F32)],
        compiler_params=_cparams(("arbitrary",)),
        name="hgrn2",
    )(chunked(qv), chunked(ff), chunked(qv), chunked(fb), tri, ones_blk)
    return o_f.reshape(BATCH, SEQ_ALL, HG_WIDTH), o_b.reshape(BATCH, SEQ_ALL, HG_WIDTH)


def _pair_queries(q):
    lane = lax.broadcasted_iota(jnp.int32, q.shape, 1)
    zero = jnp.zeros_like(q)
    return jnp.concatenate([jnp.where(lane < HEAD_DIM, q, zero), jnp.where(lane >= HEAD_DIM, q, zero)], axis=0)


def _pair_merge(o):
    m = o.shape[0] // 2
    lane = lax.broadcasted_iota(jnp.int32, (m, LANES), 1)
    return jnp.where(lane < HEAD_DIM, o[0:m], o[m:])


def _attend(q, keys, biases, values, extra=None):
    def lane_chunks(s):
        return [s[:, c:c + LANES] for c in range(0, s.shape[1], LANES)]

    q2 = _pair_queries(q)
    scores = []
    for k, b in zip(keys, biases):
        s = _dot_nt(q2, k)
        scores.append(s if b is None else s + b)
        yield
    m = functools.reduce(jnp.maximum, [c for s in scores for c in lane_chunks(s)])
    m = jnp.max(m, axis=-1, keepdims=True)
    if extra is not None:
        m = jnp.maximum(m, extra)
    yield
    acc = None
    for s, v in zip(scores, values):
        e = jnp.exp((s - m).astype(BF16))
        pv = _dot(e, jnp.concatenate([v, jnp.ones_like(v)], axis=1))
        acc = pv if acc is None else acc + pv
        yield
    denom = acc[:, LANES:]
    if extra is not None:
        denom = denom + jnp.exp(extra - m)
    return _pair_merge(acc[:, :LANES] / denom)


def _na_kernel(q_ref, k_ref, v_ref, kc_ref, vc_ref, bias_ref, o_ref):
    j = pl.program_id(1)

    @pl.when(j < ATT_STEPS)
    def _():
        units = []
        for rr in range(ROWS_PER_STEP):
            r = j * ROWS_PER_STEP + rr
            start = jnp.clip(r - NA_ROWS // 2, 0, GRID_W - NA_ROWS)
            case = jnp.where(r < NA_ROWS // 2, r,
                             jnp.where(r <= GRID_W - NA_ROWS // 2, NA_ROWS // 2, r - (GRID_W - NA_ROWS)))
            k0 = pl.multiple_of(start * GRID_W, GRID_W)
            for p in range(NA_HEADS // 2):
                c = slice(p * LANES, (p + 1) * LANES)
                bias = jnp.concatenate([bias_ref[case, 2 * p], bias_ref[case, 2 * p + 1]], axis=0)
                units.append(_attend(q_ref[rr * GRID_W:(rr + 1) * GRID_W, c],
                                     [k_ref[pl.ds(k0, NA_KEYS), c], kc_ref[:, c]], [bias, None],
                                     [v_ref[pl.ds(k0, NA_KEYS), c], vc_ref[:, c]]))
        for i, o in enumerate(_round_robin(units)):
            rr, p = divmod(i, NA_HEADS // 2)
            o_ref[rr * GRID_W:(rr + 1) * GRID_W, p * LANES:(p + 1) * LANES] = o.astype(BF16)

    @pl.when(j == ATT_STEPS)
    def _():
        units = [_attend(q_ref[:, p * LANES:(p + 1) * LANES], [kc_ref[:, p * LANES:(p + 1) * LANES]], [None],
                         [vc_ref[:, p * LANES:(p + 1) * LANES]]) for p in range(NA_HEADS // 2)]
        for p, o in enumerate(_round_robin(units)):
            o_ref[:, p * LANES:(p + 1) * LANES] = o.astype(BF16)


def _att_steps(need_ctx):
    return ATT_STEPS + 1 if need_ctx else ATT_STEPS


def _na_call(at, bias, need_ctx):
    w = NA_WIDTH
    return pl.pallas_call(
        _na_kernel,
        grid=(BATCH, _att_steps(need_ctx)),
        in_specs=[
            pl.BlockSpec((None, TM_PROJ, w), lambda b, j: (b, j, 0)),
            pl.BlockSpec((None, SEQ, w), lambda b, j: (b, 0, 1)),
            pl.BlockSpec((None, SEQ, w), lambda b, j: (b, 0, 2)),
            pl.BlockSpec((None, CTX_LEN, w), lambda b, j: (b, SEQ // CTX_LEN, 1)),
            pl.BlockSpec((None, CTX_LEN, w), lambda b, j: (b, SEQ // CTX_LEN, 2)),
            pl.BlockSpec((NA_CASES, NA_HEADS, GRID_W, NA_KEYS), lambda b, j: (0, 0, 0, 0), pipeline_mode=pl.Buffered(1)),
        ],
        out_specs=pl.BlockSpec((None, TM_PROJ, w), lambda b, j: (b, j, 0)),
        out_shape=jax.ShapeDtypeStruct((BATCH, SEQ_ALL, w), BF16),
        compiler_params=_cparams(("parallel", "arbitrary")),
        name="nbr_attn",
    )(at, at, at, at, at, bias)


def _na_bias_table(rpb):
    half = NA_ROWS // 2
    case_row = np.concatenate([np.arange(half), [half], np.arange(GRID_W - half + 1, GRID_W)])
    start = np.clip(case_row - half, 0, GRID_W - NA_ROWS)
    d_row = start[:, None] + np.arange(NA_ROWS)[None, :] - case_row[:, None]
    col = np.arange(GRID_W)
    c0 = np.clip(col - NA_COLS // 2, 0, GRID_W - NA_COLS)
    col_ok = (col[None, :] >= c0[:, None]) & (col[None, :] < c0[:, None] + NA_COLS)
    d_col = np.clip(col[None, :] - col[:, None], 1 - NA_COLS, NA_COLS - 1)
    row_sel = (d_row[:, :, None] + NA_ROWS - 1 == np.arange(2 * NA_ROWS - 1)).astype(np.float32)
    col_sel = (d_col[None, :, :] + NA_COLS - 1 == np.arange(2 * NA_COLS - 1)[:, None, None]).astype(np.float32)
    rows = jnp.einsum('cir,hrd->chid', row_sel, rpb.astype(F32), precision=lax.Precision.HIGHEST)
    b = jnp.einsum('chid,dqk->chqik', rows, col_sel, precision=lax.Precision.HIGHEST)
    b = jnp.where(col_ok[None, None, :, None, :], b, MASK_VALUE)
    return b.reshape(NA_CASES, NA_HEADS, GRID_W, NA_KEYS)


def _sw_kernel(sink_ref, q_ref, k_ref, v_ref, kc_ref, vc_ref, o_ref):
    j = pl.program_id(1)
    n_pair = SW_HEADS // 2

    def sink_col(p, m):
        row = lax.broadcasted_iota(jnp.int32, (2 * m, 1), 0)
        return jnp.where(row < m, sink_ref[p], sink_ref[p + n_pair])

    def store_pair(rows, p, o):
        ob = o.astype(BF16)
        o_ref[rows, p * HEAD_DIM:(p + 1) * HEAD_DIM] = ob[:, :HEAD_DIM]
        o_ref[rows, (p + n_pair) * HEAD_DIM:(p + n_pair + 1) * HEAD_DIM] = ob[:, HEAD_DIM:]

    @pl.when(j < ATT_STEPS)
    def _():
        units = []
        for u in range(TM_PROJ // SW_BLOCK):
            n = j * (TM_PROJ // SW_BLOCK) + u
            start = jnp.clip(n * SW_BLOCK - SW_BLOCK, 0, SEQ - SW_KEYS)
            k0 = pl.multiple_of(start, SW_BLOCK)
            rel = (n * SW_BLOCK - start
                   + lax.broadcasted_iota(jnp.int32, (SW_BLOCK, SW_KEYS), 0)
                   - lax.broadcasted_iota(jnp.int32, (SW_BLOCK, SW_KEYS), 1))
            band = jnp.where(jnp.abs(rel) <= SW_WINDOW, 0.0, MASK_VALUE).astype(F32)
            band2 = jnp.concatenate([band, band], axis=0)
            kw = k_ref[pl.ds(k0, SW_KEYS), :]
            vw = v_ref[pl.ds(k0, SW_KEYS), :]
            for p in range(n_pair):
                units.append(_attend(q_ref[u * SW_BLOCK:(u + 1) * SW_BLOCK, p * LANES:(p + 1) * LANES],
                                     [kw, kc_ref[...]], [band2, None], [vw, vc_ref[...]], extra=sink_col(p, SW_BLOCK)))
        for i, o in enumerate(_round_robin(units)):
            u, p = divmod(i, n_pair)
            store_pair(slice(u * SW_BLOCK, (u + 1) * SW_BLOCK), p, o)

    @pl.when(j == ATT_STEPS)
    def _():
        units = [_attend(q_ref[:, p * LANES:(p + 1) * LANES], [kc_ref[...]], [None], [vc_ref[...]],
                         extra=sink_col(p, TM_PROJ)) for p in range(n_pair)]
        for p, o in enumerate(_round_robin(units)):
            store_pair(slice(None), p, o)


def _sw_call(at, sink_perm, need_ctx):
    q_blk = 3 * NA_WIDTH // SW_WIDTH
    k_blk = (3 * NA_WIDTH + SW_WIDTH) // SW_KV_WIDTH
    grid_spec = pltpu.PrefetchScalarGridSpec(
        num_scalar_prefetch=1,
        grid=(BATCH, _att_steps(need_ctx)),
        in_specs=[
            pl.BlockSpec((None, TM_PROJ, SW_WIDTH), lambda b, j, s: (b, j, q_blk)),
            pl.BlockSpec((None, SEQ, SW_KV_WIDTH), lambda b, j, s: (b, 0, k_blk)),
            pl.BlockSpec((None, SEQ, SW_KV_WIDTH), lambda b, j, s: (b, 0, k_blk + 1)),
            pl.BlockSpec((None, CTX_LEN, SW_KV_WIDTH), lambda b, j, s: (b, SEQ // CTX_LEN, k_blk)),
            pl.BlockSpec((None, CTX_LEN, SW_KV_WIDTH), lambda b, j, s: (b, SEQ // CTX_LEN, k_blk + 1)),
        ],
        out_specs=pl.BlockSpec((None, TM_PROJ, SW_WIDTH), lambda b, j, s: (b, j, 0)),
    )
    return pl.pallas_call(
        _sw_kernel,
        grid_spec=grid_spec,
        out_shape=jax.ShapeDtypeStruct((BATCH, SEQ_ALL, SW_WIDTH), BF16),
        compiler_params=_cparams(("parallel", "arbitrary")),
        name="win_attn",
    )(sink_perm, at, at, at, at, at)


def _outproj_kernel(x_ref, mod_ref, of_ref, ob_ref, sg_ref, na_ref, sw_ref, w_ref, ng_ref, ones_ref, o_ref):
    o = of_ref[...] + ob_ref[...]
    ms = jnp.concatenate(
        [_dot((o[:, c:c + LANES] * o[:, c:c + LANES]).astype(BF16), ones_ref[...]) for c in range(0, HG_WIDTH, LANES)],
        axis=1) * (1.0 / HEAD_DIM)
    hg = (o * lax.rsqrt(ms + EPS)) * ng_ref[...] * sg_ref[...]
    y = _dot(hg.astype(BF16), w_ref[0:HG_WIDTH, :])
    y = y + _dot(na_ref[...], w_ref[HG_WIDTH:HG_WIDTH + NA_WIDTH, :])
    y = y + _dot(sw_ref[...], w_ref[HG_WIDTH + NA_WIDTH:, :])
    o_ref[...] = x_ref[...] + mod_ref[5:6, :] * y


def _outproj_call(x, mods_all, o_f, o_b, sg, o_na, o_sw, w_out, ng, ones_blk, *, layer, latent_only):
    n_rows = N_LAT if latent_only else N_TOK

    def per_batch(c):
        return pl.BlockSpec((None, TM_PROJ, c), lambda i: (_tile_batch(i), _tile_pos(i), 0))

    return pl.pallas_call(
        _outproj_kernel,
        grid=(n_rows // TM_PROJ,),
        in_specs=[
            pl.BlockSpec((TM_PROJ, D_MODEL), lambda i: (i, 0)),
            pl.BlockSpec((None, None, N_MOD, D_MODEL), lambda i: (layer, _tile_mod(i), 0, 0)),
            per_batch(HG_WIDTH), per_batch(HG_WIDTH), per_batch(HG_WIDTH),
            per_batch(NA_WIDTH), per_batch(SW_WIDTH),
            _resident((D_MODEL, D_MODEL), (layer,)),
            _resident((1, HG_WIDTH), (layer,)),
            _resident((LANES, LANES)),
        ],
        out_specs=pl.BlockSpec((TM_PROJ, D_MODEL), lambda i: (i, 0)),
        out_shape=jax.ShapeDtypeStruct((n_rows, D_MODEL), F32),
        compiler_params=_cparams(("parallel",)),
        name="out_proj",
    )(x, mods_all, o_f, o_b, sg, o_na, o_sw, w_out, ng, ones_blk)


def _rope_tables():
    pos = jnp.arange(SEQ)
    pos = jnp.stack([pos // GRID_W, pos % GRID_W], axis=-1).astype(F32)
    nf = HEAD_DIM // 4
    inv = ROPE_THETA ** (-jnp.arange(nf, dtype=F32) / nf)
    ang = pos[:, :, None] * inv
    cos, sin = jnp.cos(ang), jnp.sin(ang)
    cos_h = jnp.stack([cos, cos], axis=2).reshape(SEQ, HEAD_DIM)
    sin_h = jnp.stack([-sin, sin], axis=2).reshape(SEQ, HEAD_DIM)
    reps = LANES // HEAD_DIM
    cos_t = jnp.concatenate([jnp.tile(cos_h, (1, reps)), jnp.ones((CTX_LEN, LANES), F32)], axis=0)
    sin_t = jnp.concatenate([jnp.tile(sin_h, (1, reps)), jnp.zeros((CTX_LEN, LANES), F32)], axis=0)
    return cos_t, sin_t


def kernel(x, c, ctx, c_ctx, ada_w, ada_b, norm_g, ffn_w1, ffn_w3, ffn_w2, w_in, w_out,
           hg_lb_logits, hg_norm_g, na_rpb, sw_sink, final_g):
    lb_soft = jax.nn.softmax(hg_lb_logits.astype(F32), axis=0)
    lower_bounds = jnp.cumsum(lb_soft, axis=0) - lb_soft[0]
    w_in_p = w_in.astype(BF16)
    w_out_p = w_out.astype(BF16)
    sink_p = sw_sink.astype(F32)
    w1 = ffn_w1.astype(BF16)
    w3 = ffn_w3.astype(BF16)
    w2 = ffn_w2.astype(BF16)
    cos_t, sin_t = _rope_tables()
    tri, ones_blk = _hgrn_consts()
    final_g2 = final_g.reshape(1, D_MODEL)
    norm_g4 = norm_g.reshape(DEPTH, 3, 1, D_MODEL)
    hg_norm_g3 = hg_norm_g.reshape(DEPTH, 1, HG_WIDTH)

    c8 = jnp.concatenate([c, c_ctx[None, :], jnp.zeros((8 - BATCH - 1, D_MODEL), F32)], axis=0)
    mods_all = _ada_call(c8, ada_w, ada_b).reshape(DEPTH, 8, N_MOD, D_MODEL)

    xs = jnp.concatenate([x.reshape(N_LAT, D_MODEL), ctx.reshape(N_CTX, D_MODEL)], axis=0)
    for l in range(DEPTH):
        need_ctx = l < DEPTH - 1
        xs = _ffn_call(xs, mods_all, norm_g4, w1, w3, w2, final_g2, layer=l, which=0, latent_only=False, final=False)
        qv, ff, fb, sg, at = _inproj_call(xs, mods_all, norm_g4, w_in_p, lower_bounds, cos_t, sin_t, layer=l)
        o_f, o_b = _hgrn_call(qv, ff, fb, tri, ones_blk)
        o_na = _na_call(at, _na_bias_table(na_rpb[l]), need_ctx)
        o_sw = _sw_call(at, sink_p[l], need_ctx)
        xs = _outproj_call(xs, mods_all, o_f, o_b, sg, o_na, o_sw, w_out_p, hg_norm_g3, ones_blk,
                           layer=l, latent_only=not need_ctx)
        xs = _ffn_call(xs, mods_all, norm_g4, w1, w3, w2, final_g2, layer=l, which=1,
                       latent_only=not need_ctx, final=not need_ctx)
    return xs.reshape(BATCH, SEQ, D_MODEL)
```

```python
import functools

import jax
import jax.numpy as jnp
import numpy as np
from jax import lax
from jax.experimental import pallas as pl
from jax.experimental.pallas import tpu as pltpu

F32 = jnp.float32
BF16 = jnp.bfloat16

D_MODEL = 1024
BATCH = 4
SEQ = 4096
DEPTH = 4
GRID_W = 64
CTX_LEN = 256
HEAD_DIM = 64
EPS = 1e-6
MASK_VALUE = -1e30
LOG2_E = 1.4426950408889634
ROPE_THETA = 10000.0
N_MOD = 9
D_FF = 2816
HG_WIDTH = 256
HG_HEADS = 4
NA_WIDTH = 384
NA_HEADS = 6
NA_ROWS = 8
NA_COLS = 16
SW_WIDTH = 384
SW_HEADS = 6
SW_KV_WIDTH = 128
SW_WINDOW = 128
SW_BLOCK = 128
IN_WIDTH = 3072

LANES = 128
VMEM_LIMIT_BYTES = 56 * 1024 * 1024

N_LAT = BATCH * SEQ
N_CTX = BATCH * CTX_LEN
N_TOK = N_LAT + N_CTX
SEQ_ALL = SEQ + CTX_LEN
TM_PROJ = 256
TM_FFN = 512
TM_LAT = 1024
ROWS_PER_STEP = TM_PROJ // GRID_W
ATT_STEPS = SEQ // TM_PROJ
HG_CHUNK = 128
HG_SUB = 16
N_CHUNK_LAT = SEQ // HG_CHUNK
N_CHUNK_CTX = CTX_LEN // HG_CHUNK
N_CHUNK = N_CHUNK_LAT + N_CHUNK_CTX
NA_KEYS = NA_ROWS * GRID_W
NA_CASES = 8
SW_KEYS = 3 * SW_BLOCK


def _cparams(sem):
    return pltpu.CompilerParams(dimension_semantics=sem, vmem_limit_bytes=VMEM_LIMIT_BYTES)


def _silu(a):
    return a * jax.nn.sigmoid(a)


def _dot(a, b):
    return jnp.dot(a, b, preferred_element_type=F32)


def _dot_nt(a, b):
    return lax.dot_general(a, b, (((1,), (1,)), ((), ())), preferred_element_type=F32)


def _modulated_norm(x, g, shift, scale):
    ms = jnp.mean(x * x, axis=-1, keepdims=True)
    return (x * lax.rsqrt(ms + EPS)) * g * (1.0 + scale) + shift


ADA_TN = 1536


def _ada_kernel(c_ref, w_ref, b_ref, o_ref):
    s = _silu(c_ref[...]).astype(BF16)
    o_ref[...] = _dot(s, w_ref[...].astype(BF16)) + b_ref[...]


def _ada_call(c8, ada_w, ada_b):
    n_out = N_MOD * D_MODEL
    return pl.pallas_call(
        _ada_kernel,
        grid=(DEPTH, n_out // ADA_TN),
        in_specs=[
            pl.BlockSpec((8, D_MODEL), lambda l, j: (0, 0)),
            pl.BlockSpec((None, D_MODEL, ADA_TN), lambda l, j: (l, 0, j)),
            pl.BlockSpec((None, 1, ADA_TN), lambda l, j: (l, 0, j)),
        ],
        out_specs=pl.BlockSpec((None, 8, ADA_TN), lambda l, j: (l, 0, j)),
        out_shape=jax.ShapeDtypeStruct((DEPTH, 8, n_out), F32),
        compiler_params=_cparams(("parallel", "parallel")),
        name="ada_mod",
    )(c8, ada_w, ada_b.reshape(DEPTH, 1, n_out))


def _swiglu_half_step(x, mod_ref, g_ref, w1_ref, w3_ref, w2_ref, *, mod0):
    h = _modulated_norm(x, g_ref[...], mod_ref[mod0:mod0 + 1, :], mod_ref[mod0 + 1:mod0 + 2, :]).astype(BF16)
    a = _silu(_dot(h, w1_ref[...])) * _dot(h, w3_ref[...])
    return x + (0.5 * mod_ref[mod0 + 2:mod0 + 3, :]) * _dot(a.astype(BF16), w2_ref[...])


def _ffn_kernel(x_ref, mod_ref, g_ref, w1_ref, w3_ref, w2_ref, o_ref):
    o_ref[...] = _swiglu_half_step(x_ref[...], mod_ref, g_ref, w1_ref, w3_ref, w2_ref, mod0=0)


def _resident(shape, lead=()):
    return pl.BlockSpec((None,) * len(lead) + tuple(shape), lambda *_: tuple(lead) + (0,) * len(shape),
                        pipeline_mode=pl.Buffered(1))


def _ffn_call(x, mods_all, norm_g, w1, w3, w2, *, layer):
    tiles_per_batch = SEQ // TM_FFN

    def mod_idx(i):
        return (layer, jnp.where(i < BATCH * tiles_per_batch, i // tiles_per_batch, BATCH), 0, 0)

    return pl.pallas_call(
        _ffn_kernel,
        grid=(N_TOK // TM_FFN,),
        in_specs=[
            pl.BlockSpec((TM_FFN, D_MODEL), lambda i: (i, 0)),
            pl.BlockSpec((None, None, N_MOD, D_MODEL), mod_idx),
            _resident((1, D_MODEL), (layer, 0)),
            _resident((D_MODEL, D_FF), (layer, 0)),
            _resident((D_MODEL, D_FF), (layer, 0)),
            _resident((D_FF, D_MODEL), (layer, 0)),
        ],
        out_specs=pl.BlockSpec((TM_FFN, D_MODEL), lambda i: (i, 0)),
        out_shape=jax.ShapeDtypeStruct((N_TOK, D_MODEL), F32),
        compiler_params=_cparams(("parallel",)),
        name="ffn",
    )(x, mods_all, norm_g, w1, w3, w2)


def _rope(z, cos, sin_signed, first_of_pair):
    partner = jnp.where(first_of_pair, pltpu.roll(z, LANES - 16, 1), pltpu.roll(z, 16, 1))
    return z * cos + partner * sin_signed


def _inproj_kernel(x_ref, *refs):
    _round_robin([_inproj_rows(pl.ds(r0, TM_PROJ), x_ref, *refs) for r0 in range(0, x_ref.shape[0], TM_PROJ)])


def _inproj_rows(rows, x_ref, mod_ref, g_ref, w_ref, lb_ref, cos_ref, sin_ref,
                 qv_ref, ff_ref, fb_ref, sg_ref, at_ref):
    h = _modulated_norm(x_ref[rows, :], g_ref[...], mod_ref[3:4, :], mod_ref[4:5, :]).astype(BF16)
    W = HG_WIDTH
    p_all = _dot(h, w_ref[...])
    yield

    def proj(lo, hi):
        return p_all[:, lo:hi]

    qv_ref[rows, 0:W] = _silu(proj(0, W))
    for d, dst in ((0, ff_ref), (1, fb_ref)):
        z = proj((1 + d) * W, (2 + d) * W)
        lb = lb_ref[d:d + 1, :]
        dst[rows, 0:W] = jnp.log(lb + (1.0 - lb) * jax.nn.sigmoid(z)) * LOG2_E
        dst[rows, W:2 * W] = jnp.log((1.0 - lb) * jax.nn.sigmoid(-z)) * LOG2_E
    qv_ref[rows, W:2 * W] = proj(3 * W, 4 * W)
    sg_ref[rows, :] = _silu(proj(4 * W, 5 * W))
    yield

    scale = HEAD_DIM ** -0.5
    na0 = 5 * W
    at_ref[rows, 0:NA_WIDTH] = (proj(na0, na0 + NA_WIDTH) * scale).astype(BF16)
    at_ref[rows, NA_WIDTH:3 * NA_WIDTH] = proj(na0 + NA_WIDTH, na0 + 3 * NA_WIDTH).astype(BF16)
    yield
    sw0 = na0 + 3 * NA_WIDTH
    cos = cos_ref[rows, :]
    sin = sin_ref[rows, :]
    lane = lax.broadcasted_iota(jnp.int32, (TM_PROJ, LANES), 1)
    first = (lane % 32) < 16
    z = [_rope(proj(sw0 + j * LANES, sw0 + (j + 1) * LANES), cos, sin, first)
         for j in range((SW_WIDTH + SW_KV_WIDTH) // LANES)]
    low = lane < HEAD_DIM
    z[0], z[1], z[2] = (jnp.where(low, z[0], z[1]), pltpu.roll(jnp.where(low, z[2], z[0]), HEAD_DIM, 1),
                        jnp.where(low, z[1], z[2]))
    for j in range(len(z)):
        zj = z[j] * scale if j < SW_WIDTH // LANES else z[j]
        at_ref[rows, 3 * NA_WIDTH + j * LANES:3 * NA_WIDTH + (j + 1) * LANES] = zj.astype(BF16)
    v0 = sw0 + SW_WIDTH + SW_KV_WIDTH
    at_ref[rows, 3 * NA_WIDTH + SW_WIDTH + SW_KV_WIDTH:] = proj(v0, v0 + SW_KV_WIDTH).astype(BF16)


AT_WIDTH = 3 * NA_WIDTH + SW_WIDTH + 2 * SW_KV_WIDTH


N_INPROJ_IN = 7
INPROJ_WIDTHS = (2 * HG_WIDTH, 2 * HG_WIDTH, 2 * HG_WIDTH, HG_WIDTH, AT_WIDTH)
INPROJ_DTYPES = (F32, F32, F32, F32, BF16)


def _inproj_ctx_kernel(*refs):
    _inproj_kernel(*refs[:N_INPROJ_IN], *refs[N_INPROJ_IN + len(INPROJ_WIDTHS):])


def _inproj_call(x, mods_all, norm_g, w_in, lb, cos_t, sin_t, *, layer):
    widths = INPROJ_WIDTHS
    out_shape = [jax.ShapeDtypeStruct((BATCH, SEQ_ALL, c), dt) for c, dt in zip(widths, INPROJ_DTYPES)]
    params = [_resident((1, D_MODEL), (layer, 1)), _resident((D_MODEL, IN_WIDTH), (layer,)),
              _resident((2, HG_WIDTH), (layer,))]
    lat_tiles = SEQ // TM_LAT

    latent = pl.pallas_call(
        _inproj_kernel,
        grid=(N_LAT // TM_LAT,),
        in_specs=[
            pl.BlockSpec((TM_LAT, D_MODEL), lambda i: (i, 0)),
            pl.BlockSpec((None, None, N_MOD, D_MODEL), lambda i: (layer, i // lat_tiles, 0, 0)),
            *params,
            pl.BlockSpec((TM_LAT, LANES), lambda i: (i % lat_tiles, 0)),
            pl.BlockSpec((TM_LAT, LANES), lambda i: (i % lat_tiles, 0)),
        ],
        out_specs=[pl.BlockSpec((None, TM_LAT, c), lambda i: (i // lat_tiles, i % lat_tiles, 0)) for c in widths],
        out_shape=out_shape,
        compiler_params=_cparams(("parallel",)),
        name="in_proj",
    )(x, mods_all, norm_g, w_in, lb, cos_t, sin_t)

    ctx_pos = SEQ // TM_PROJ
    return pl.pallas_call(
        _inproj_ctx_kernel,
        grid=(BATCH,),
        in_specs=[
            pl.BlockSpec((TM_PROJ, D_MODEL), lambda b: (N_LAT // TM_PROJ + b, 0)),
            pl.BlockSpec((None, None, N_MOD, D_MODEL), lambda b: (layer, BATCH, 0, 0)),
            *params,
            pl.BlockSpec((TM_PROJ, LANES), lambda b: (ctx_pos, 0)),
            pl.BlockSpec((TM_PROJ, LANES), lambda b: (ctx_pos, 0)),
            *[pl.BlockSpec(memory_space=pl.ANY) for _ in widths],
        ],
        out_specs=[pl.BlockSpec((None, TM_PROJ, c), lambda b: (b, ctx_pos, 0)) for c in widths],
        out_shape=out_shape,
        input_output_aliases={N_INPROJ_IN + k: k for k in range(len(widths))},
        compiler_params=_cparams(("parallel",)),
        name="in_proj_ctx",
    )(x, mods_all, norm_g, w_in, lb, cos_t, sin_t, *latent)


N_SUB = HG_CHUNK // HG_SUB
N_PAIR = HG_HEADS // 2
HG_MINI = HG_SUB // 2


def _hgrn_consts():
    t = np.arange(HG_CHUNK)
    same = (t[:, None] // HG_SUB) == (t[None, :] // HG_SUB)
    lower = same & (t[None, :] <= t[:, None])
    upper = same & (t[None, :] >= t[:, None])
    tri = np.stack([lower, upper]).astype(np.float32)
    d = np.arange(LANES)
    head_blocks = ((d[:, None] // HEAD_DIM) == (d[None, :] // HEAD_DIM)).astype(np.float32)
    return jnp.asarray(tri, BF16), jnp.asarray(head_blocks, BF16)


def _split2(a):
    hi = a.astype(BF16)
    lo = (a - hi.astype(F32)).astype(BF16)
    return jnp.concatenate([hi, lo], axis=1)


def _bcast_rows(a, s, block):
    n = HG_CHUNK // block
    a3 = a.reshape(n, block, LANES)
    return jnp.broadcast_to(a3[:, s:s + 1, :], (n, block, LANES)).reshape(HG_CHUNK, LANES)


def _hgrn_group(qs, v, gl, lk, tri, ones_blk, st, direction):
    fwd = direction == 0
    c = _dot(tri, _split2(gl))
    yield
    cum = c[:, 0:LANES] + c[:, LANES:]
    tot = _bcast_rows(cum, HG_SUB - 1 if fwd else 0, HG_SUB)
    row = lax.broadcasted_iota(jnp.int32, (HG_CHUNK, LANES), 0)
    t_mini = row % HG_MINI
    blk = row // HG_SUB
    v16 = v.astype(BF16)
    ck = cum - lk

    o = jnp.zeros((HG_CHUNK, LANES), F32)
    for s in range(HG_MINI):
        keep = (t_mini >= s) if fwd else (t_mini <= s)
        w = jnp.where(keep, qs * jnp.exp2(cum - _bcast_rows(ck, s, HG_MINI)), 0.0)
        o = o + _dot(w.astype(BF16), ones_blk) * _bcast_rows(v, s, HG_MINI)
        yield

    later = ((row % HG_SUB) >= HG_MINI) if fwd else ((row % HG_SUB) < HG_MINI)
    edge = _bcast_rows(cum, HG_MINI - 1 if fwd else HG_MINI, HG_SUB)
    q_edge = jnp.where(later, qs * jnp.exp2(jnp.minimum(cum - edge, 0.0)), 0.0)
    k_edge = jnp.where(later, 0.0, jnp.exp2(jnp.minimum(edge - ck, 0.0)))
    a = _dot_nt(_pair_queries(q_edge.astype(BF16)), k_edge.astype(BF16))
    q_blk = lax.broadcasted_iota(jnp.int32, (2 * HG_CHUNK, HG_CHUNK), 0) % HG_CHUNK // HG_SUB
    k_blk = lax.broadcasted_iota(jnp.int32, (2 * HG_CHUNK, HG_CHUNK), 1) // HG_SUB
    yield
    a = jnp.where(q_blk == k_blk, a, 0.0)
    o = o + _pair_merge(_dot(a.astype(BF16), v16))
    yield

    qd = qs * jnp.exp2(cum)
    kd = jnp.exp2(tot - ck)
    dec = jnp.exp2(tot)
    k_exp = jnp.concatenate([jnp.where(blk == j, kd, 0.0).astype(BF16) for j in range(N_SUB)], axis=1)
    upd = _dot(v.T.astype(BF16), k_exp)
    yield
    head_mask = ones_blk.astype(F32)
    before = [None] * N_SUB
    for j in (range(N_SUB) if fwd else range(N_SUB - 1, -1, -1)):
        before[j] = st.astype(BF16)
        st = st * dec[j * HG_SUB:j * HG_SUB + 1, :] + upd[:, j * LANES:(j + 1) * LANES] * head_mask
    yield
    q_exp = jnp.concatenate([jnp.where(blk == j, qd, 0.0).astype(BF16) for j in range(N_SUB)], axis=1)
    o = o + _dot_nt(q_exp, jnp.concatenate(before, axis=1))
    return o, st


def _round_robin(generators):
    results = [None] * len(generators)
    live = list(range(len(generators)))
    while live:
        for k in list(live):
            try:
                next(generators[k])
            except StopIteration as done:
                results[k] = done.value
                live.remove(k)
    return results


def _hgrn_kernel(qvf_ref, ff_ref, qvb_ref, fb_ref, tri_ref, ones_ref, of_ref, ob_ref, st_scr):
    @pl.when(pl.program_id(0) == 0)
    def _():
        st_scr[...] = jnp.zeros_like(st_scr)

    ones_blk = ones_ref[...]
    W = HG_WIDTH
    def body(b, carry):
        chains = []
        for direction, (qv_ref, f_ref) in enumerate(((qvf_ref, ff_ref), (qvb_ref, fb_ref))):
            for hp in range(N_PAIR):
                c0 = hp * LANES
                chains.append(_hgrn_group(
                    qv_ref[b, :, c0:c0 + LANES], qv_ref[b, :, W + c0:W + c0 + LANES],
                    f_ref[b, :, c0:c0 + LANES], f_ref[b, :, W + c0:W + c0 + LANES],
                    tri_ref[direction], ones_blk, st_scr[b, direction * N_PAIR + hp], direction))
        results = _round_robin(chains)
        for direction, o_ref in enumerate((of_ref, ob_ref)):
            for hp in range(N_PAIR):
                o, st = results[direction * N_PAIR + hp]
                o_ref[b, :, hp * LANES:(hp + 1) * LANES] = o
                st_scr[b, direction * N_PAIR + hp] = st
        return carry

    lax.fori_loop(0, BATCH, body, 0)


def _hgrn_call(qv, ff, fb, tri, ones_blk):
    def chunked(a):
        return a.reshape(BATCH, N_CHUNK, HG_CHUNK, a.shape[-1])

    def fwd_idx(s):
        return jnp.where(s < N_CHUNK_CTX, N_CHUNK_LAT + s, s - N_CHUNK_CTX)

    def bwd_idx(s):
        return jnp.where(s < N_CHUNK_CTX, N_CHUNK - 1 - s, N_CHUNK - 1 - s)

    def spec(c, idx):
        return pl.BlockSpec((BATCH, None, HG_CHUNK, c), lambda s: (0, idx(s), 0, 0))

    out_sds = jax.ShapeDtypeStruct((BATCH, N_CHUNK, HG_CHUNK, HG_WIDTH), F32)
    o_f, o_b = pl.pallas_call(
        _hgrn_kernel,
        grid=(N_CHUNK,),
        in_specs=[
            spec(2 * HG_WIDTH, fwd_idx), spec(2 * HG_WIDTH, fwd_idx),
            spec(2 * HG_WIDTH, bwd_idx), spec(2 * HG_WIDTH, bwd_idx),
            pl.BlockSpec((2, HG_CHUNK, HG_CHUNK), lambda s: (0, 0, 0)),
            pl.BlockSpec((LANES, LANES), lambda s: (0, 0)),
        ],
        out_specs=[spec(HG_WIDTH, fwd_idx), spec(HG_WIDTH, bwd_idx)],
        out_shape=[out_sds, out_sds],
        scratch_shapes=[pltpu.VMEM((BATCH, 2 * N_PAIR, LANES, LANES), F32)],
        compiler_params=_cparams(("arbitrary",)),
        name="hgrn2",
    )(chunked(qv), chunked(ff), chunked(qv), chunked(fb), tri, ones_blk)
    return o_f.reshape(BATCH, SEQ_ALL, HG_WIDTH), o_b.reshape(BATCH, SEQ_ALL, HG_WIDTH)


def _pair_queries(q):
    lane = lax.broadcasted_iota(jnp.int32, q.shape, 1)
    zero = jnp.zeros_like(q)
    return jnp.concatenate([jnp.where(lane < HEAD_DIM, q, zero), jnp.where(lane >= HEAD_DIM, q, zero)], axis=0)


def _pair_merge(o):
    m = o.shape[0] // 2
    lane = lax.broadcasted_iota(jnp.int32, (m, LANES), 1)
    return jnp.where(lane < HEAD_DIM, o[0:m], o[m:])


def _attend(q, keys, biases, values, extra=None):
    def lane_chunks(s):
        return [s[:, c:c + LANES] for c in range(0, s.shape[1], LANES)]

    q2 = _pair_queries(q)
    scores = []
    for k, b in zip(keys, biases):
        s = _dot_nt(q2, k)
        scores.append(s if b is None else s + b)
        yield
    m = functools.reduce(jnp.maximum, [c for s in scores for c in lane_chunks(s)])
    m = jnp.max(m, axis=-1, keepdims=True)
    if extra is not None:
        m = jnp.maximum(m, extra)
    yield
    acc = None
    for s, v in zip(scores, values):
        e = jnp.exp((s - m).astype(BF16))
        pv = _dot(e, jnp.concatenate([v, jnp.ones_like(v)], axis=1))
        acc = pv if acc is None else acc + pv
        yield
    denom = acc[:, LANES:]
    if extra is not None:
        denom = denom + jnp.exp(extra - m)
    return _pair_merge(acc[:, :LANES] / denom)


def _na_kernel(q_ref, k_ref, v_ref, kc_ref, vc_ref, bias_ref, o_ref):
    j = pl.program_id(1)

    @pl.when(j < ATT_STEPS)
    def _():
        units = []
        for rr in range(ROWS_PER_STEP):
            r = j * ROWS_PER_STEP + rr
            start = jnp.clip(r - NA_ROWS // 2, 0, GRID_W - NA_ROWS)
            case = jnp.where(r < NA_ROWS // 2, r,
                             jnp.where(r <= GRID_W - NA_ROWS // 2, NA_ROWS // 2, r - (GRID_W - NA_ROWS)))
            k0 = pl.multiple_of(start * GRID_W, GRID_W)
            for p in range(NA_HEADS // 2):
                c = slice(p * LANES, (p + 1) * LANES)
                bias = jnp.concatenate([bias_ref[case, 2 * p], bias_ref[case, 2 * p + 1]], axis=0)
                units.append(_attend(q_ref[rr * GRID_W:(rr + 1) * GRID_W, c],
                                     [k_ref[pl.ds(k0, NA_KEYS), c], kc_ref[:, c]], [bias, None],
                                     [v_ref[pl.ds(k0, NA_KEYS), c], vc_ref[:, c]]))
        for i, o in enumerate(_round_robin(units)):
            rr, p = divmod(i, NA_HEADS // 2)
            o_ref[rr * GRID_W:(rr + 1) * GRID_W, p * LANES:(p + 1) * LANES] = o.astype(BF16)

    @pl.when(j == ATT_STEPS)
    def _():
        units = [_attend(q_ref[:, p * LANES:(p + 1) * LANES], [kc_ref[:, p * LANES:(p + 1) * LANES]], [None],
                         [vc_ref[:, p * LANES:(p + 1) * LANES]]) for p in range(NA_HEADS // 2)]
        for p, o in enumerate(_round_robin(units)):
            o_ref[:, p * LANES:(p + 1) * LANES] = o.astype(BF16)


def _att_steps(need_ctx):
    return ATT_STEPS + 1 if need_ctx else ATT_STEPS


def _na_call(at, bias, need_ctx):
    w = NA_WIDTH
    return pl.pallas_call(
        _na_kernel,
        grid=(BATCH, _att_steps(need_ctx)),
        in_specs=[
            pl.BlockSpec((None, TM_PROJ, w), lambda b, j: (b, j, 0)),
            pl.BlockSpec((None, SEQ, w), lambda b, j: (b, 0, 1)),
            pl.BlockSpec((None, SEQ, w), lambda b, j: (b, 0, 2)),
            pl.BlockSpec((None, CTX_LEN, w), lambda b, j: (b, SEQ // CTX_LEN, 1)),
            pl.BlockSpec((None, CTX_LEN, w), lambda b, j: (b, SEQ // CTX_LEN, 2)),
            pl.BlockSpec((NA_CASES, NA_HEADS, GRID_W, NA_KEYS), lambda b, j: (0, 0, 0, 0), pipeline_mode=pl.Buffered(1)),
        ],
        out_specs=pl.BlockSpec((None, TM_PROJ, w), lambda b, j: (b, j, 0)),
        out_shape=jax.ShapeDtypeStruct((BATCH, SEQ_ALL, w), BF16),
        compiler_params=_cparams(("parallel", "arbitrary")),
        name="nbr_attn",
    )(at, at, at, at, at, bias)


def _na_bias_table(rpb):
    half = NA_ROWS // 2
    case_row = np.concatenate([np.arange(half), [half], np.arange(GRID_W - half + 1, GRID_W)])
    start = np.clip(case_row - half, 0, GRID_W - NA_ROWS)
    d_row = start[:, None] + np.arange(NA_ROWS)[None, :] - case_row[:, None]
    col = np.arange(GRID_W)
    c0 = np.clip(col - NA_COLS // 2, 0, GRID_W - NA_COLS)
    col_ok = (col[None, :] >= c0[:, None]) & (col[None, :] < c0[:, None] + NA_COLS)
    d_col = np.clip(col[None, :] - col[:, None], 1 - NA_COLS, NA_COLS - 1)
    row_sel = (d_row[:, :, None] + NA_ROWS - 1 == np.arange(2 * NA_ROWS - 1)).astype(np.float32)
    col_sel = (d_col[None, :, :] + NA_COLS - 1 == np.arange(2 * NA_COLS - 1)[:, None, None]).astype(np.float32)
    rows = jnp.einsum('cir,hrd->chid', row_sel, rpb.astype(F32), precision=lax.Precision.HIGHEST)
    b = jnp.einsum('chid,dqk->chqik', rows, col_sel, precision=lax.Precision.HIGHEST)
    b = jnp.where(col_ok[None, None, :, None, :], b, MASK_VALUE)
    return b.reshape(NA_CASES, NA_HEADS, GRID_W, NA_KEYS)


def _sw_kernel(sink_ref, q_ref, k_ref, v_ref, kc_ref, vc_ref, o_ref):
    j = pl.program_id(1)
    n_pair = SW_HEADS // 2

    def sink_col(p, m):
        row = lax.broadcasted_iota(jnp.int32, (2 * m, 1), 0)
        return jnp.where(row < m, sink_ref[p], sink_ref[p + n_pair])

    def store_pair(rows, p, o):
        ob = o.astype(BF16)
        o_ref[rows, p * HEAD_DIM:(p + 1) * HEAD_DIM] = ob[:, :HEAD_DIM]
        o_ref[rows, (p + n_pair) * HEAD_DIM:(p + n_pair + 1) * HEAD_DIM] = ob[:, HEAD_DIM:]

    @pl.when(j < ATT_STEPS)
    def _():
        units = []
        for u in range(TM_PROJ // SW_BLOCK):
            n = j * (TM_PROJ // SW_BLOCK) + u
            start = jnp.clip(n * SW_BLOCK - SW_BLOCK, 0, SEQ - SW_KEYS)
            k0 = pl.multiple_of(start, SW_BLOCK)
            rel = (n * SW_BLOCK - start
                   + lax.broadcasted_iota(jnp.int32, (SW_BLOCK, SW_KEYS), 0)
                   - lax.broadcasted_iota(jnp.int32, (SW_BLOCK, SW_KEYS), 1))
            band = jnp.where(jnp.abs(rel) <= SW_WINDOW, 0.0, MASK_VALUE).astype(F32)
            band2 = jnp.concatenate([band, band], axis=0)
            kw = k_ref[pl.ds(k0, SW_KEYS), :]
            vw = v_ref[pl.ds(k0, SW_KEYS), :]
            for p in range(n_pair):
                units.append(_attend(q_ref[u * SW_BLOCK:(u + 1) * SW_BLOCK, p * LANES:(p + 1) * LANES],
                                     [kw, kc_ref[...]], [band2, None], [vw, vc_ref[...]], extra=sink_col(p, SW_BLOCK)))
        for i, o in enumerate(_round_robin(units)):
            u, p = divmod(i, n_pair)
            store_pair(slice(u * SW_BLOCK, (u + 1) * SW_BLOCK), p, o)

    @pl.when(j == ATT_STEPS)
    def _():
        units = [_attend(q_ref[:, p * LANES:(p + 1) * LANES], [kc_ref[...]], [None], [vc_ref[...]],
                         extra=sink_col(p, TM_PROJ)) for p in range(n_pair)]
        for p, o in enumerate(_round_robin(units)):
            store_pair(slice(None), p, o)


def _sw_call(at, sink_perm, need_ctx):
    q_blk = 3 * NA_WIDTH // SW_WIDTH
    k_blk = (3 * NA_WIDTH + SW_WIDTH) // SW_KV_WIDTH
    grid_spec = pltpu.PrefetchScalarGridSpec(
        num_scalar_prefetch=1,
        grid=(BATCH, _att_steps(need_ctx)),
        in_specs=[
            pl.BlockSpec((None, TM_PROJ, SW_WIDTH), lambda b, j, s: (b, j, q_blk)),
            pl.BlockSpec((None, SEQ, SW_KV_WIDTH), lambda b, j, s: (b, 0, k_blk)),
            pl.BlockSpec((None, SEQ, SW_KV_WIDTH), lambda b, j, s: (b, 0, k_blk + 1)),
            pl.BlockSpec((None, CTX_LEN, SW_KV_WIDTH), lambda b, j, s: (b, SEQ // CTX_LEN, k_blk)),
            pl.BlockSpec((None, CTX_LEN, SW_KV_WIDTH), lambda b, j, s: (b, SEQ // CTX_LEN, k_blk + 1)),
        ],
        out_specs=pl.BlockSpec((None, TM_PROJ, SW_WIDTH), lambda b, j, s: (b, j, 0)),
    )
    return pl.pallas_call(
        _sw_kernel,
        grid_spec=grid_spec,
        out_shape=jax.ShapeDtypeStruct((BATCH, SEQ_ALL, SW_WIDTH), BF16),
        compiler_params=_cparams(("parallel", "arbitrary")),
        name="win_attn",
    )(sink_perm, at, at, at, at, at)


N_MIX_IN = 15


def _rows(ref):
    v = ref[...]
    return v.reshape(-1, v.shape[-1])


def _mix_ffn_kernel(x_ref, mod_ref, of_ref, ob_ref, sg_ref, na_ref, sw_ref, w_ref, ng_ref, ones_ref,
                    g_ref, w1_ref, w3_ref, w2_ref, fg_ref, o_ref, *, final):
    o = _rows(of_ref) + _rows(ob_ref)
    ms = jnp.concatenate(
        [_dot((o[:, c:c + LANES] * o[:, c:c + LANES]).astype(BF16), ones_ref[...]) for c in range(0, HG_WIDTH, LANES)],
        axis=1) * (1.0 / HEAD_DIM)
    hg = (o * lax.rsqrt(ms + EPS)) * ng_ref[...] * _rows(sg_ref)
    y = _dot(hg.astype(BF16), w_ref[0:HG_WIDTH, :])
    y = y + _dot(_rows(na_ref), w_ref[HG_WIDTH:HG_WIDTH + NA_WIDTH, :])
    y = y + _dot(_rows(sw_ref), w_ref[HG_WIDTH + NA_WIDTH:, :])
    x = x_ref[...] + mod_ref[5:6, :] * y
    y = _swiglu_half_step(x, mod_ref, g_ref, w1_ref, w3_ref, w2_ref, mod0=6)
    if final:
        ms = jnp.mean(y * y, axis=-1, keepdims=True)
        y = (y * lax.rsqrt(ms + EPS)) * fg_ref[...]
    o_ref[...] = y


def _mix_ffn_ctx_kernel(*refs):
    _mix_ffn_kernel(*refs[:N_MIX_IN], refs[-1], final=False)


def _mix_ffn_call(x, mods_all, o_f, o_b, sg, o_na, o_sw, w_out, ng, ones_blk, norm_g, w1, w3, w2, final_g,
                  *, layer, need_ctx):
    n_rows = N_TOK if need_ctx else N_LAT
    mixer_widths = (HG_WIDTH, HG_WIDTH, HG_WIDTH, NA_WIDTH, SW_WIDTH)
    params = [_resident((D_MODEL, D_MODEL), (layer,)), _resident((1, HG_WIDTH), (layer,)), _resident((LANES, LANES)),
              _resident((1, D_MODEL), (layer, 2)), _resident((D_MODEL, D_FF), (layer, 1)),
              _resident((D_MODEL, D_FF), (layer, 1)), _resident((D_FF, D_MODEL), (layer, 1)), _resident((1, D_MODEL))]
    operands = (x, mods_all, o_f, o_b, sg, o_na, o_sw, w_out, ng, ones_blk, norm_g, w1, w3, w2, final_g)
    assert len(operands) == N_MIX_IN
    out_shape = jax.ShapeDtypeStruct((n_rows, D_MODEL), F32)
    lat_tiles = SEQ // TM_FFN

    latent = pl.pallas_call(
        functools.partial(_mix_ffn_kernel, final=not need_ctx),
        grid=(N_LAT // TM_FFN,),
        in_specs=[
            pl.BlockSpec((TM_FFN, D_MODEL), lambda i: (i, 0)),
            pl.BlockSpec((None, None, N_MOD, D_MODEL), lambda i: (layer, i // lat_tiles, 0, 0)),
            *[pl.BlockSpec((None, TM_FFN, c), lambda i: (i // lat_tiles, i % lat_tiles, 0)) for c in mixer_widths],
            *params,
        ],
        out_specs=pl.BlockSpec((TM_FFN, D_MODEL), lambda i: (i, 0)),
        out_shape=out_shape,
        compiler_params=_cparams(("parallel",)),
        name="mix_ffn",
    )(*operands)
    if not need_ctx:
        return latent

    per_tile = TM_FFN // CTX_LEN
    ctx_pos = SEQ // CTX_LEN
    return pl.pallas_call(
        _mix_ffn_ctx_kernel,
        grid=(N_CTX // TM_FFN,),
        in_specs=[
            pl.BlockSpec((TM_FFN, D_MODEL), lambda k: (N_LAT // TM_FFN + k, 0)),
            pl.BlockSpec((None, None, N_MOD, D_MODEL), lambda k: (layer, BATCH, 0, 0)),
            *[pl.BlockSpec((per_tile, CTX_LEN, c), lambda k: (k, ctx_pos, 0)) for c in mixer_widths],
            *params,
            pl.BlockSpec(memory_space=pl.ANY),
        ],
        out_specs=pl.BlockSpec((TM_FFN, D_MODEL), lambda k: (N_LAT // TM_FFN + k, 0)),
        out_shape=out_shape,
        input_output_aliases={N_MIX_IN: 0},
        compiler_params=_cparams(("parallel",)),
        name="mix_ffn_ctx",
    )(*operands, latent)


def _rope_tables():
    pos = jnp.arange(SEQ)
    pos = jnp.stack([pos // GRID_W, pos % GRID_W], axis=-1).astype(F32)
    nf = HEAD_DIM // 4
    inv = ROPE_THETA ** (-jnp.arange(nf, dtype=F32) / nf)
    ang = pos[:, :, None] * inv
    cos, sin = jnp.cos(ang), jnp.sin(ang)
    cos_h = jnp.stack([cos, cos], axis=2).reshape(SEQ, HEAD_DIM)
    sin_h = jnp.stack([-sin, sin], axis=2).reshape(SEQ, HEAD_DIM)
    reps = LANES // HEAD_DIM
    cos_t = jnp.concatenate([jnp.tile(cos_h, (1, reps)), jnp.ones((CTX_LEN, LANES), F32)], axis=0)
    sin_t = jnp.concatenate([jnp.tile(sin_h, (1, reps)), jnp.zeros((CTX_LEN, LANES), F32)], axis=0)
    return cos_t, sin_t


def kernel(x, c, ctx, c_ctx, ada_w, ada_b, norm_g, ffn_w1, ffn_w3, ffn_w2, w_in, w_out,
           hg_lb_logits, hg_norm_g, na_rpb, sw_sink, final_g):
    lb_soft = jax.nn.softmax(hg_lb_logits.astype(F32), axis=0)
    lower_bounds = jnp.cumsum(lb_soft, axis=0) - lb_soft[0]
    w_in_p = w_in.astype(BF16)
    w_out_p = w_out.astype(BF16)
    sink_p = sw_sink.astype(F32)
    w1 = ffn_w1.astype(BF16)
    w3 = ffn_w3.astype(BF16)
    w2 = ffn_w2.astype(BF16)
    cos_t, sin_t = _rope_tables()
    tri, ones_blk = _hgrn_consts()
    final_g2 = final_g.reshape(1, D_MODEL)
    norm_g4 = norm_g.reshape(DEPTH, 3, 1, D_MODEL)
    hg_norm_g3 = hg_norm_g.reshape(DEPTH, 1, HG_WIDTH)

    c8 = jnp.concatenate([c, c_ctx[None, :], jnp.zeros((8 - BATCH - 1, D_MODEL), F32)], axis=0)
    mods_all = _ada_call(c8, ada_w, ada_b).reshape(DEPTH, 8, N_MOD, D_MODEL)

    xs = jnp.concatenate([x.reshape(N_LAT, D_MODEL), ctx.reshape(N_CTX, D_MODEL)], axis=0)
    for l in range(DEPTH):
        need_ctx = l < DEPTH - 1
        xs = _ffn_call(xs, mods_all, norm_g4, w1, w3, w2, layer=l)
        qv, ff, fb, sg, at = _inproj_call(xs, mods_all, norm_g4, w_in_p, lower_bounds, cos_t, sin_t, layer=l)
        o_f, o_b = _hgrn_call(qv, ff, fb, tri, ones_blk)
        o_na = _na_call(at, _na_bias_table(na_rpb[l]), need_ctx)
        o_sw = _sw_call(at, sink_p[l], need_ctx)
        xs = _mix_ffn_call(xs, mods_all, o_f, o_b, sg, o_na, o_sw, w_out_p, hg_norm_g3, ones_blk,
                           norm_g4, w1, w3, w2, final_g2, layer=l, need_ctx=need_ctx)
    return xs.reshape(BATCH, SEQ, D_MODEL)
```

```python
import functools

import jax
import jax.numpy as jnp
import numpy as np
from jax import lax
from jax.experimental import pallas as pl
from jax.experimental.pallas import tpu as pltpu

F32 = jnp.float32
BF16 = jnp.bfloat16

D_MODEL = 1024
BATCH = 4
SEQ = 4096
DEPTH = 4
GRID_W = 64
CTX_LEN = 256
HEAD_DIM = 64
EPS = 1e-6
MASK_VALUE = -1e30
LOG2_E = 1.4426950408889634
ROPE_THETA = 10000.0
N_MOD = 9
D_FF = 2816
HG_WIDTH = 256
HG_HEADS = 4
NA_WIDTH = 384
NA_HEADS = 6
NA_ROWS = 8
NA_COLS = 16
SW_WIDTH = 384
SW_HEADS = 6
SW_KV_WIDTH = 128
SW_WINDOW = 128
SW_BLOCK = 128
IN_WIDTH = 3072

LANES = 128
VMEM_LIMIT_BYTES = 56 * 1024 * 1024

N_LAT = BATCH * SEQ
N_CTX = BATCH * CTX_LEN
N_TOK = N_LAT + N_CTX
SEQ_ALL = SEQ + CTX_LEN
TM_PROJ = 256
TM_FFN = 512
TM_LAT = 1024
ROWS_PER_STEP = TM_PROJ // GRID_W
ATT_STEPS = SEQ // TM_PROJ
HG_CHUNK = 128
HG_SUB = 16
N_CHUNK_LAT = SEQ // HG_CHUNK
N_CHUNK_CTX = CTX_LEN // HG_CHUNK
N_CHUNK = N_CHUNK_LAT + N_CHUNK_CTX
NA_KEYS = NA_ROWS * GRID_W
SW_KEYS = 3 * SW_BLOCK


def _cparams(sem):
    return pltpu.CompilerParams(dimension_semantics=sem, vmem_limit_bytes=VMEM_LIMIT_BYTES)


def _silu(a):
    return a * jax.nn.sigmoid(a)


def _dot(a, b):
    return jnp.dot(a, b, preferred_element_type=F32)


def _dot_nt(a, b):
    return lax.dot_general(a, b, (((1,), (1,)), ((), ())), preferred_element_type=F32)


def _modulated_norm(x, g, shift, scale):
    ms = jnp.mean(x * x, axis=-1, keepdims=True)
    return (x * lax.rsqrt(ms + EPS)) * g * (1.0 + scale) + shift


ADA_TN = 1536


def _ada_kernel(c_ref, w_ref, b_ref, o_ref):
    s = _silu(c_ref[...]).astype(BF16)
    o_ref[...] = _dot(s, w_ref[...].astype(BF16)) + b_ref[...]


def _ada_call(c8, ada_w, ada_b):
    n_out = N_MOD * D_MODEL
    return pl.pallas_call(
        _ada_kernel,
        grid=(DEPTH, n_out // ADA_TN),
        in_specs=[
            pl.BlockSpec((8, D_MODEL), lambda l, j: (0, 0)),
            pl.BlockSpec((None, D_MODEL, ADA_TN), lambda l, j: (l, 0, j)),
            pl.BlockSpec((None, 1, ADA_TN), lambda l, j: (l, 0, j)),
        ],
        out_specs=pl.BlockSpec((None, 8, ADA_TN), lambda l, j: (l, 0, j)),
        out_shape=jax.ShapeDtypeStruct((DEPTH, 8, n_out), F32),
        compiler_params=_cparams(("parallel", "parallel")),
        name="ada_mod",
    )(c8, ada_w, ada_b.reshape(DEPTH, 1, n_out))


def _swiglu_half_step(x, mod_ref, g_ref, w1_ref, w3_ref, w2_ref, *, mod0):
    h = _modulated_norm(x, g_ref[...], mod_ref[mod0:mod0 + 1, :], mod_ref[mod0 + 1:mod0 + 2, :]).astype(BF16)
    a = _silu(_dot(h, w1_ref[...])) * _dot(h, w3_ref[...])
    return x + (0.5 * mod_ref[mod0 + 2:mod0 + 3, :]) * _dot(a.astype(BF16), w2_ref[...])


def _ffn_kernel(xl_ref, xc_ref, mod_ref, g_ref, w1_ref, w3_ref, w2_ref, o_ref):
    x = jnp.where(pl.program_id(0) < N_LAT // TM_FFN, xl_ref[...], xc_ref[...])
    o_ref[...] = _swiglu_half_step(x, mod_ref, g_ref, w1_ref, w3_ref, w2_ref, mod0=0)


def _resident(shape, lead=()):
    return pl.BlockSpec((None,) * len(lead) + tuple(shape), lambda *_: tuple(lead) + (0,) * len(shape),
                        pipeline_mode=pl.Buffered(1))


def _ffn_call(x_lat, x_ctx, ctx_tile0, mods_all, norm_g, w1, w3, w2, *, layer):
    tiles_per_batch = SEQ // TM_FFN
    lat_tiles = N_LAT // TM_FFN

    def mod_idx(i):
        return (layer, jnp.where(i < BATCH * tiles_per_batch, i // tiles_per_batch, BATCH), 0, 0)

    return pl.pallas_call(
        _ffn_kernel,
        grid=(N_TOK // TM_FFN,),
        in_specs=[
            pl.BlockSpec((TM_FFN, D_MODEL), lambda i: (jnp.minimum(i, lat_tiles - 1), 0)),
            pl.BlockSpec((TM_FFN, D_MODEL), lambda i: (ctx_tile0 + jnp.maximum(i - lat_tiles, 0), 0)),
            pl.BlockSpec((None, None, N_MOD, D_MODEL), mod_idx),
            _resident((1, D_MODEL), (layer, 0)),
            _resident((D_MODEL, D_FF), (layer, 0)),
            _resident((D_MODEL, D_FF), (layer, 0)),
            _resident((D_FF, D_MODEL), (layer, 0)),
        ],
        out_specs=pl.BlockSpec((TM_FFN, D_MODEL), lambda i: (i, 0)),
        out_shape=jax.ShapeDtypeStruct((N_TOK, D_MODEL), F32),
        compiler_params=_cparams(("parallel",)),
        name="ffn",
    )(x_lat, x_ctx, mods_all, norm_g, w1, w3, w2)


def _rope(z, cos, sin_signed, first_of_pair):
    partner = jnp.where(first_of_pair, pltpu.roll(z, LANES - 16, 1), pltpu.roll(z, 16, 1))
    return z * cos + partner * sin_signed


def _inproj_kernel(x_ref, *refs):
    _round_robin([_inproj_rows(pl.ds(r0, TM_PROJ), x_ref, *refs) for r0 in range(0, x_ref.shape[0], TM_PROJ)])


def _inproj_rows(rows, x_ref, mod_ref, g_ref, w_ref, lb_ref, cos_ref, sin_ref,
                 qv_ref, ff_ref, fb_ref, sg_ref, at_ref):
    h = _modulated_norm(x_ref[rows, :], g_ref[...], mod_ref[3:4, :], mod_ref[4:5, :]).astype(BF16)
    W = HG_WIDTH
    p_all = _dot(h, w_ref[...])
    yield

    def proj(lo, hi):
        return p_all[:, lo:hi]

    qv_ref[rows, 0:W] = _silu(proj(0, W))
    for d, dst in ((0, ff_ref), (1, fb_ref)):
        z = proj((1 + d) * W, (2 + d) * W)
        lb = lb_ref[d:d + 1, :]
        dst[rows, 0:W] = jnp.log(lb + (1.0 - lb) * jax.nn.sigmoid(z)) * LOG2_E
        dst[rows, W:2 * W] = jnp.log((1.0 - lb) * jax.nn.sigmoid(-z)) * LOG2_E
    qv_ref[rows, W:2 * W] = proj(3 * W, 4 * W)
    sg_ref[rows, :] = _silu(proj(4 * W, 5 * W))
    yield

    scale = HEAD_DIM ** -0.5
    na0 = 5 * W
    at_ref[rows, 0:NA_WIDTH] = (proj(na0, na0 + NA_WIDTH) * scale).astype(BF16)
    at_ref[rows, NA_WIDTH:3 * NA_WIDTH] = proj(na0 + NA_WIDTH, na0 + 3 * NA_WIDTH).astype(BF16)
    yield
    sw0 = na0 + 3 * NA_WIDTH
    cos = cos_ref[rows, :]
    sin = sin_ref[rows, :]
    lane = lax.broadcasted_iota(jnp.int32, (TM_PROJ, LANES), 1)
    first = (lane % 32) < 16
    z = [_rope(proj(sw0 + j * LANES, sw0 + (j + 1) * LANES), cos, sin, first)
         for j in range((SW_WIDTH + SW_KV_WIDTH) // LANES)]
    low = lane < HEAD_DIM
    z[0], z[1], z[2] = (jnp.where(low, z[0], z[1]), pltpu.roll(jnp.where(low, z[2], z[0]), HEAD_DIM, 1),
                        jnp.where(low, z[1], z[2]))
    for j in range(len(z)):
        zj = z[j] * scale if j < SW_WIDTH // LANES else z[j]
        at_ref[rows, 3 * NA_WIDTH + j * LANES:3 * NA_WIDTH + (j + 1) * LANES] = zj.astype(BF16)
    v0 = sw0 + SW_WIDTH + SW_KV_WIDTH
    at_ref[rows, 3 * NA_WIDTH + SW_WIDTH + SW_KV_WIDTH:] = proj(v0, v0 + SW_KV_WIDTH).astype(BF16)


AT_WIDTH = 3 * NA_WIDTH + SW_WIDTH + 2 * SW_KV_WIDTH


N_INPROJ_IN = 7
INPROJ_WIDTHS = (2 * HG_WIDTH, 2 * HG_WIDTH, 2 * HG_WIDTH, HG_WIDTH, AT_WIDTH)
INPROJ_DTYPES = (F32, F32, F32, F32, BF16)


def _inproj_ctx_kernel(*refs):
    _inproj_kernel(*refs[:N_INPROJ_IN], *refs[N_INPROJ_IN + len(INPROJ_WIDTHS):])


def _inproj_call(x, mods_all, norm_g, w_in, lb, cos_t, sin_t, *, layer):
    widths = INPROJ_WIDTHS
    out_shape = [jax.ShapeDtypeStruct((BATCH, SEQ_ALL, c), dt) for c, dt in zip(widths, INPROJ_DTYPES)]
    params = [_resident((1, D_MODEL), (layer, 1)), _resident((D_MODEL, IN_WIDTH), (layer,)),
              _resident((2, HG_WIDTH), (layer,))]
    lat_tiles = SEQ // TM_LAT

    latent = pl.pallas_call(
        _inproj_kernel,
        grid=(N_LAT // TM_LAT,),
        in_specs=[
            pl.BlockSpec((TM_LAT, D_MODEL), lambda i: (i, 0)),
            pl.BlockSpec((None, None, N_MOD, D_MODEL), lambda i: (layer, i // lat_tiles, 0, 0)),
            *params,
            pl.BlockSpec((TM_LAT, LANES), lambda i: (i % lat_tiles, 0)),
            pl.BlockSpec((TM_LAT, LANES), lambda i: (i % lat_tiles, 0)),
        ],
        out_specs=[pl.BlockSpec((None, TM_LAT, c), lambda i: (i // lat_tiles, i % lat_tiles, 0)) for c in widths],
        out_shape=out_shape,
        compiler_params=_cparams(("parallel",)),
        name="in_proj",
    )(x, mods_all, norm_g, w_in, lb, cos_t, sin_t)

    ctx_pos = SEQ // TM_PROJ
    return pl.pallas_call(
        _inproj_ctx_kernel,
        grid=(BATCH,),
        in_specs=[
            pl.BlockSpec((TM_PROJ, D_MODEL), lambda b: (N_LAT // TM_PROJ + b, 0)),
            pl.BlockSpec((None, None, N_MOD, D_MODEL), lambda b: (layer, BATCH, 0, 0)),
            *params,
            pl.BlockSpec((TM_PROJ, LANES), lambda b: (ctx_pos, 0)),
            pl.BlockSpec((TM_PROJ, LANES), lambda b: (ctx_pos, 0)),
            *[pl.BlockSpec(memory_space=pl.ANY) for _ in widths],
        ],
        out_specs=[pl.BlockSpec((None, TM_PROJ, c), lambda b: (b, ctx_pos, 0)) for c in widths],
        out_shape=out_shape,
        input_output_aliases={N_INPROJ_IN + k: k for k in range(len(widths))},
        compiler_params=_cparams(("parallel",)),
        name="in_proj_ctx",
    )(x, mods_all, norm_g, w_in, lb, cos_t, sin_t, *latent)


N_SUB = HG_CHUNK // HG_SUB
N_PAIR = HG_HEADS // 2
HG_MINI = HG_SUB // 2


def _hgrn_consts():
    t = np.arange(HG_CHUNK)
    same = (t[:, None] // HG_SUB) == (t[None, :] // HG_SUB)
    lower = same & (t[None, :] <= t[:, None])
    upper = same & (t[None, :] >= t[:, None])
    tri = np.stack([lower, upper]).astype(np.float32)
    d = np.arange(LANES)
    head_blocks = ((d[:, None] // HEAD_DIM) == (d[None, :] // HEAD_DIM)).astype(np.float32)
    return jnp.asarray(tri, BF16), jnp.asarray(head_blocks, BF16)


def _split2(a):
    hi = a.astype(BF16)
    lo = (a - hi.astype(F32)).astype(BF16)
    return jnp.concatenate([hi, lo], axis=1)


def _bcast_rows(a, s, block):
    n = HG_CHUNK // block
    a3 = a.reshape(n, block, LANES)
    return jnp.broadcast_to(a3[:, s:s + 1, :], (n, block, LANES)).reshape(HG_CHUNK, LANES)


def _hgrn_group(qs, v, gl, lk, tri, ones_blk, st, direction):
    fwd = direction == 0
    c = _dot(tri, _split2(gl))
    yield
    cum = c[:, 0:LANES] + c[:, LANES:]
    tot = _bcast_rows(cum, HG_SUB - 1 if fwd else 0, HG_SUB)
    row = lax.broadcasted_iota(jnp.int32, (HG_CHUNK, LANES), 0)
    t_mini = row % HG_MINI
    blk = row // HG_SUB
    v16 = v.astype(BF16)
    ck = cum - lk

    o = jnp.zeros((HG_CHUNK, LANES), F32)
    for s in range(HG_MINI):
        keep = (t_mini >= s) if fwd else (t_mini <= s)
        w = jnp.where(keep, qs * jnp.exp2(cum - _bcast_rows(ck, s, HG_MINI)), 0.0)
        o = o + _dot(w.astype(BF16), ones_blk) * _bcast_rows(v, s, HG_MINI)
        yield

    later = ((row % HG_SUB) >= HG_MINI) if fwd else ((row % HG_SUB) < HG_MINI)
    edge = _bcast_rows(cum, HG_MINI - 1 if fwd else HG_MINI, HG_SUB)
    q_edge = jnp.where(later, qs * jnp.exp2(jnp.minimum(cum - edge, 0.0)), 0.0)
    k_edge = jnp.where(later, 0.0, jnp.exp2(jnp.minimum(edge - ck, 0.0)))
    a = _dot_nt(_pair_queries(q_edge.astype(BF16)), k_edge.astype(BF16))
    q_blk = lax.broadcasted_iota(jnp.int32, (2 * HG_CHUNK, HG_CHUNK), 0) % HG_CHUNK // HG_SUB
    k_blk = lax.broadcasted_iota(jnp.int32, (2 * HG_CHUNK, HG_CHUNK), 1) // HG_SUB
    yield
    a = jnp.where(q_blk == k_blk, a, 0.0)
    o = o + _pair_merge(_dot(a.astype(BF16), v16))
    yield

    qd = qs * jnp.exp2(cum)
    kd = jnp.exp2(tot - ck)
    dec = jnp.exp2(tot)
    k_exp = jnp.concatenate([jnp.where(blk == j, kd, 0.0).astype(BF16) for j in range(N_SUB)], axis=1)
    upd = _dot(v.T.astype(BF16), k_exp)
    yield
    head_mask = ones_blk.astype(F32)
    before = [None] * N_SUB
    for j in (range(N_SUB) if fwd else range(N_SUB - 1, -1, -1)):
        before[j] = st.astype(BF16)
        st = st * dec[j * HG_SUB:j * HG_SUB + 1, :] + upd[:, j * LANES:(j + 1) * LANES] * head_mask
    yield
    q_exp = jnp.concatenate([jnp.where(blk == j, qd, 0.0).astype(BF16) for j in range(N_SUB)], axis=1)
    o = o + _dot_nt(q_exp, jnp.concatenate(before, axis=1))
    return o, st


def _round_robin(generators):
    results = [None] * len(generators)
    live = list(range(len(generators)))
    while live:
        for k in list(live):
            try:
                next(generators[k])
            except StopIteration as done:
                results[k] = done.value
                live.remove(k)
    return results


def _hgrn_kernel(qvf_ref, ff_ref, qvb_ref, fb_ref, tri_ref, ones_ref, of_ref, ob_ref, st_scr):
    @pl.when(pl.program_id(0) == 0)
    def _():
        st_scr[...] = jnp.zeros_like(st_scr)

    ones_blk = ones_ref[...]
    W = HG_WIDTH
    def body(b, carry):
        chains = []
        for direction, (qv_ref, f_ref) in enumerate(((qvf_ref, ff_ref), (qvb_ref, fb_ref))):
            for hp in range(N_PAIR):
                c0 = hp * LANES
                chains.append(_hgrn_group(
                    qv_ref[b, :, c0:c0 + LANES], qv_ref[b, :, W + c0:W + c0 + LANES],
                    f_ref[b, :, c0:c0 + LANES], f_ref[b, :, W + c0:W + c0 + LANES],
                    tri_ref[direction], ones_blk, st_scr[b, direction * N_PAIR + hp], direction))
        results = _round_robin(chains)
        for direction, o_ref in enumerate((of_ref, ob_ref)):
            for hp in range(N_PAIR):
                o, st = results[direction * N_PAIR + hp]
                o_ref[b, :, hp * LANES:(hp + 1) * LANES] = o
                st_scr[b, direction * N_PAIR + hp] = st
        return carry

    lax.fori_loop(0, BATCH, body, 0)


def _hgrn_call(qv, ff, fb, tri, ones_blk):
    def chunked(a):
        return a.reshape(BATCH, N_CHUNK, HG_CHUNK, a.shape[-1])

    def fwd_idx(s):
        return jnp.where(s < N_CHUNK_CTX, N_CHUNK_LAT + s, s - N_CHUNK_CTX)

    def bwd_idx(s):
        return jnp.where(s < N_CHUNK_CTX, N_CHUNK - 1 - s, N_CHUNK - 1 - s)

    def spec(c, idx):
        return pl.BlockSpec((BATCH, None, HG_CHUNK, c), lambda s: (0, idx(s), 0, 0))

    out_sds = jax.ShapeDtypeStruct((BATCH, N_CHUNK, HG_CHUNK, HG_WIDTH), F32)
    o_f, o_b = pl.pallas_call(
        _hgrn_kernel,
        grid=(N_CHUNK,),
        in_specs=[
            spec(2 * HG_WIDTH, fwd_idx), spec(2 * HG_WIDTH, fwd_idx),
            spec(2 * HG_WIDTH, bwd_idx), spec(2 * HG_WIDTH, bwd_idx),
            pl.BlockSpec((2, HG_CHUNK, HG_CHUNK), lambda s: (0, 0, 0)),
            pl.BlockSpec((LANES, LANES), lambda s: (0, 0)),
        ],
        out_specs=[spec(HG_WIDTH, fwd_idx), spec(HG_WIDTH, bwd_idx)],
        out_shape=[out_sds, out_sds],
        scratch_shapes=[pltpu.VMEM((BATCH, 2 * N_PAIR, LANES, LANES), F32)],
        compiler_params=_cparams(("arbitrary",)),
        name="hgrn2",
    )(chunked(qv), chunked(ff), chunked(qv), chunked(fb), tri, ones_blk)
    return o_f.reshape(BATCH, SEQ_ALL, HG_WIDTH), o_b.reshape(BATCH, SEQ_ALL, HG_WIDTH)


def _pair_queries(q):
    lane = lax.broadcasted_iota(jnp.int32, q.shape, 1)
    zero = jnp.zeros_like(q)
    return jnp.concatenate([jnp.where(lane < HEAD_DIM, q, zero), jnp.where(lane >= HEAD_DIM, q, zero)], axis=0)


def _pair_merge(o):
    m = o.shape[0] // 2
    lane = lax.broadcasted_iota(jnp.int32, (m, LANES), 1)
    return jnp.where(lane < HEAD_DIM, o[0:m], o[m:])


def _attend(q, keys, biases, values, extra=None):
    def lane_chunks(s):
        return [s[:, c:c + LANES] for c in range(0, s.shape[1], LANES)]

    q2 = _pair_queries(q)
    scores = []
    for k, b in zip(keys, biases):
        s = _dot_nt(q2, k)
        scores.append(s if b is None else s + b)
        yield
    m = functools.reduce(jnp.maximum, [c for s in scores for c in lane_chunks(s)])
    m = jnp.max(m, axis=-1, keepdims=True)
    if extra is not None:
        m = jnp.maximum(m, extra)
    yield
    acc = None
    for s, v in zip(scores, values):
        e = jnp.exp((s - m).astype(BF16))
        pv = _dot(e, jnp.concatenate([v, jnp.ones_like(v)], axis=1))
        acc = pv if acc is None else acc + pv
        yield
    denom = acc[:, LANES:]
    if extra is not None:
        denom = denom + jnp.exp(extra - m)
    return _pair_merge(acc[:, :LANES] / denom)


def _na_kernel(q_ref, k_ref, v_ref, kc_ref, vc_ref, bias_ref, o_ref):
    j = pl.program_id(1)

    @pl.when(j < ATT_STEPS)
    def _():
        units = []
        for rr in range(ROWS_PER_STEP):
            r = j * ROWS_PER_STEP + rr
            start = jnp.clip(r - NA_ROWS // 2, 0, GRID_W - NA_ROWS)
            k0 = pl.multiple_of(start * GRID_W, GRID_W)
            t0 = start - r + (NA_ROWS - 1)
            for p in range(NA_HEADS // 2):
                c = slice(p * LANES, (p + 1) * LANES)
                bias = jnp.concatenate(
                    [jnp.concatenate([bias_ref[h, t0 + i] for i in range(0, NA_ROWS, 2)], axis=1)
                     for h in (2 * p, 2 * p + 1)], axis=0)
                units.append(_attend(q_ref[rr * GRID_W:(rr + 1) * GRID_W, c],
                                     [k_ref[pl.ds(k0, NA_KEYS), c], kc_ref[:, c]], [bias, None],
                                     [v_ref[pl.ds(k0, NA_KEYS), c], vc_ref[:, c]]))
        for i, o in enumerate(_round_robin(units)):
            rr, p = divmod(i, NA_HEADS // 2)
            o_ref[rr * GRID_W:(rr + 1) * GRID_W, p * LANES:(p + 1) * LANES] = o.astype(BF16)

    @pl.when(j == ATT_STEPS)
    def _():
        units = [_attend(q_ref[:, p * LANES:(p + 1) * LANES], [kc_ref[:, p * LANES:(p + 1) * LANES]], [None],
                         [vc_ref[:, p * LANES:(p + 1) * LANES]]) for p in range(NA_HEADS // 2)]
        for p, o in enumerate(_round_robin(units)):
            o_ref[:, p * LANES:(p + 1) * LANES] = o.astype(BF16)


def _att_steps(need_ctx):
    return ATT_STEPS + 1 if need_ctx else ATT_STEPS


def _na_call(at, bias, *, layer, need_ctx):
    w = NA_WIDTH
    return pl.pallas_call(
        _na_kernel,
        grid=(BATCH, _att_steps(need_ctx)),
        in_specs=[
            pl.BlockSpec((None, TM_PROJ, w), lambda b, j: (b, j, 0)),
            pl.BlockSpec((None, SEQ, w), lambda b, j: (b, 0, 1)),
            pl.BlockSpec((None, SEQ, w), lambda b, j: (b, 0, 2)),
            pl.BlockSpec((None, CTX_LEN, w), lambda b, j: (b, SEQ // CTX_LEN, 1)),
            pl.BlockSpec((None, CTX_LEN, w), lambda b, j: (b, SEQ // CTX_LEN, 2)),
            _resident((NA_HEADS, 2 * NA_ROWS - 2, GRID_W, 2 * GRID_W), (layer,)),
        ],
        out_specs=pl.BlockSpec((None, TM_PROJ, w), lambda b, j: (b, j, 0)),
        out_shape=jax.ShapeDtypeStruct((BATCH, SEQ_ALL, w), BF16),
        compiler_params=_cparams(("parallel", "arbitrary")),
        name="nbr_attn",
    )(at, at, at, at, at, bias)


def _na_bias_table(rpb):
    col = np.arange(GRID_W)
    c0 = np.clip(col - NA_COLS // 2, 0, GRID_W - NA_COLS)
    col_ok = (col[None, :] >= c0[:, None]) & (col[None, :] < c0[:, None] + NA_COLS)
    d_col = np.clip(col[None, :] - col[:, None], 1 - NA_COLS, NA_COLS - 1)
    col_sel = (d_col[None, :, :] + NA_COLS - 1 == np.arange(2 * NA_COLS - 1)[:, None, None]).astype(np.float32)
    b = jnp.einsum('lhrd,dqk->lhrqk', rpb.astype(F32), col_sel, precision=lax.Precision.HIGHEST)
    b = jnp.where(col_ok, b, MASK_VALUE)
    return jnp.concatenate([b[:, :, :-1], b[:, :, 1:]], axis=-1)


def _sw_kernel(sink_ref, q_ref, k_ref, v_ref, kc_ref, vc_ref, o_ref):
    j = pl.program_id(1)
    n_pair = SW_HEADS // 2

    def sink_col(p, m):
        row = lax.broadcasted_iota(jnp.int32, (2 * m, 1), 0)
        return jnp.where(row < m, sink_ref[p], sink_ref[p + n_pair])

    def store_pair(rows, p, o):
        ob = o.astype(BF16)
        o_ref[rows, p * HEAD_DIM:(p + 1) * HEAD_DIM] = ob[:, :HEAD_DIM]
        o_ref[rows, (p + n_pair) * HEAD_DIM:(p + n_pair + 1) * HEAD_DIM] = ob[:, HEAD_DIM:]

    @pl.when(j < ATT_STEPS)
    def _():
        units = []
        for u in range(TM_PROJ // SW_BLOCK):
            n = j * (TM_PROJ // SW_BLOCK) + u
            start = jnp.clip(n * SW_BLOCK - SW_BLOCK, 0, SEQ - SW_KEYS)
            k0 = pl.multiple_of(start, SW_BLOCK)
            rel = (n * SW_BLOCK - start
                   + lax.broadcasted_iota(jnp.int32, (SW_BLOCK, SW_KEYS), 0)
                   - lax.broadcasted_iota(jnp.int32, (SW_BLOCK, SW_KEYS), 1))
            band = jnp.where(jnp.abs(rel) <= SW_WINDOW, 0.0, MASK_VALUE).astype(F32)
            band2 = jnp.concatenate([band, band], axis=0)
            kw = k_ref[pl.ds(k0, SW_KEYS), :]
            vw = v_ref[pl.ds(k0, SW_KEYS), :]
            for p in range(n_pair):
                units.append(_attend(q_ref[u * SW_BLOCK:(u + 1) * SW_BLOCK, p * LANES:(p + 1) * LANES],
                                     [kw, kc_ref[...]], [band2, None], [vw, vc_ref[...]], extra=sink_col(p, SW_BLOCK)))
        for i, o in enumerate(_round_robin(units)):
            u, p = divmod(i, n_pair)
            store_pair(slice(u * SW_BLOCK, (u + 1) * SW_BLOCK), p, o)

    @pl.when(j == ATT_STEPS)
    def _():
        units = [_attend(q_ref[:, p * LANES:(p + 1) * LANES], [kc_ref[...]], [None], [vc_ref[...]],
                         extra=sink_col(p, TM_PROJ)) for p in range(n_pair)]
        for p, o in enumerate(_round_robin(units)):
            store_pair(slice(None), p, o)


def _sw_call(at, sink_perm, need_ctx):
    q_blk = 3 * NA_WIDTH // SW_WIDTH
    k_blk = (3 * NA_WIDTH + SW_WIDTH) // SW_KV_WIDTH
    grid_spec = pltpu.PrefetchScalarGridSpec(
        num_scalar_prefetch=1,
        grid=(BATCH, _att_steps(need_ctx)),
        in_specs=[
            pl.BlockSpec((None, TM_PROJ, SW_WIDTH), lambda b, j, s: (b, j, q_blk)),
            pl.BlockSpec((None, SEQ, SW_KV_WIDTH), lambda b, j, s: (b, 0, k_blk)),
            pl.BlockSpec((None, SEQ, SW_KV_WIDTH), lambda b, j, s: (b, 0, k_blk + 1)),
            pl.BlockSpec((None, CTX_LEN, SW_KV_WIDTH), lambda b, j, s: (b, SEQ // CTX_LEN, k_blk)),
            pl.BlockSpec((None, CTX_LEN, SW_KV_WIDTH), lambda b, j, s: (b, SEQ // CTX_LEN, k_blk + 1)),
        ],
        out_specs=pl.BlockSpec((None, TM_PROJ, SW_WIDTH), lambda b, j, s: (b, j, 0)),
    )
    return pl.pallas_call(
        _sw_kernel,
        grid_spec=grid_spec,
        out_shape=jax.ShapeDtypeStruct((BATCH, SEQ_ALL, SW_WIDTH), BF16),
        compiler_params=_cparams(("parallel", "arbitrary")),
        name="win_attn",
    )(sink_perm, at, at, at, at, at)


N_MIX_IN = 15


def _rows(ref):
    v = ref[...]
    return v.reshape(-1, v.shape[-1])


def _mix_ffn_kernel(x_ref, mod_ref, of_ref, ob_ref, sg_ref, na_ref, sw_ref, w_ref, ng_ref, ones_ref,
                    g_ref, w1_ref, w3_ref, w2_ref, fg_ref, o_ref, *, final):
    o = _rows(of_ref) + _rows(ob_ref)
    ms = jnp.concatenate(
        [_dot((o[:, c:c + LANES] * o[:, c:c + LANES]).astype(BF16), ones_ref[...]) for c in range(0, HG_WIDTH, LANES)],
        axis=1) * (1.0 / HEAD_DIM)
    hg = (o * lax.rsqrt(ms + EPS)) * ng_ref[...] * _rows(sg_ref)
    y = _dot(hg.astype(BF16), w_ref[0:HG_WIDTH, :])
    y = y + _dot(_rows(na_ref), w_ref[HG_WIDTH:HG_WIDTH + NA_WIDTH, :])
    y = y + _dot(_rows(sw_ref), w_ref[HG_WIDTH + NA_WIDTH:, :])
    x = x_ref[...] + mod_ref[5:6, :] * y
    y = _swiglu_half_step(x, mod_ref, g_ref, w1_ref, w3_ref, w2_ref, mod0=6)
    if final:
        ms = jnp.mean(y * y, axis=-1, keepdims=True)
        y = (y * lax.rsqrt(ms + EPS)) * fg_ref[...]
    o_ref[...] = y


def _mix_ffn_ctx_kernel(*refs):
    _mix_ffn_kernel(*refs[:N_MIX_IN], refs[-1], final=False)


def _mix_ffn_call(x, mods_all, o_f, o_b, sg, o_na, o_sw, w_out, ng, ones_blk, norm_g, w1, w3, w2, final_g,
                  *, layer, need_ctx):
    n_rows = N_TOK if need_ctx else N_LAT
    mixer_widths = (HG_WIDTH, HG_WIDTH, HG_WIDTH, NA_WIDTH, SW_WIDTH)
    params = [_resident((D_MODEL, D_MODEL), (layer,)), _resident((1, HG_WIDTH), (layer,)), _resident((LANES, LANES)),
              _resident((1, D_MODEL), (layer, 2)), _resident((D_MODEL, D_FF), (layer, 1)),
              _resident((D_MODEL, D_FF), (layer, 1)), _resident((D_FF, D_MODEL), (layer, 1)), _resident((1, D_MODEL))]
    operands = (x, mods_all, o_f, o_b, sg, o_na, o_sw, w_out, ng, ones_blk, norm_g, w1, w3, w2, final_g)
    assert len(operands) == N_MIX_IN
    out_shape = jax.ShapeDtypeStruct((n_rows, D_MODEL), F32)
    lat_tiles = SEQ // TM_FFN

    latent = pl.pallas_call(
        functools.partial(_mix_ffn_kernel, final=not need_ctx),
        grid=(N_LAT // TM_FFN,),
        in_specs=[
            pl.BlockSpec((TM_FFN, D_MODEL), lambda i: (i, 0)),
            pl.BlockSpec((None, None, N_MOD, D_MODEL), lambda i: (layer, i // lat_tiles, 0, 0)),
            *[pl.BlockSpec((None, TM_FFN, c), lambda i: (i // lat_tiles, i % lat_tiles, 0)) for c in mixer_widths],
            *params,
        ],
        out_specs=pl.BlockSpec((TM_FFN, D_MODEL), lambda i: (i, 0)),
        out_shape=out_shape,
        compiler_params=_cparams(("parallel",)),
        name="mix_ffn",
    )(*operands)
    if not need_ctx:
        return latent

    per_tile = TM_FFN // CTX_LEN
    ctx_pos = SEQ // CTX_LEN
    return pl.pallas_call(
        _mix_ffn_ctx_kernel,
        grid=(N_CTX // TM_FFN,),
        in_specs=[
            pl.BlockSpec((TM_FFN, D_MODEL), lambda k: (N_LAT // TM_FFN + k, 0)),
            pl.BlockSpec((None, None, N_MOD, D_MODEL), lambda k: (layer, BATCH, 0, 0)),
            *[pl.BlockSpec((per_tile, CTX_LEN, c), lambda k: (k, ctx_pos, 0)) for c in mixer_widths],
            *params,
            pl.BlockSpec(memory_space=pl.ANY),
        ],
        out_specs=pl.BlockSpec((TM_FFN, D_MODEL), lambda k: (N_LAT // TM_FFN + k, 0)),
        out_shape=out_shape,
        input_output_aliases={N_MIX_IN: 0},
        compiler_params=_cparams(("parallel",)),
        name="mix_ffn_ctx",
    )(*operands, latent)


def _rope_tables():
    pos = jnp.arange(SEQ)
    pos = jnp.stack([pos // GRID_W, pos % GRID_W], axis=-1).astype(F32)
    nf = HEAD_DIM // 4
    inv = ROPE_THETA ** (-jnp.arange(nf, dtype=F32) / nf)
    ang = pos[:, :, None] * inv
    cos, sin = jnp.cos(ang), jnp.sin(ang)
    cos_h = jnp.stack([cos, cos], axis=2).reshape(SEQ, HEAD_DIM)
    sin_h = jnp.stack([-sin, sin], axis=2).reshape(SEQ, HEAD_DIM)
    reps = LANES // HEAD_DIM
    cos_t = jnp.concatenate([jnp.tile(cos_h, (1, reps)), jnp.ones((CTX_LEN, LANES), F32)], axis=0)
    sin_t = jnp.concatenate([jnp.tile(sin_h, (1, reps)), jnp.zeros((CTX_LEN, LANES), F32)], axis=0)
    return cos_t, sin_t


def kernel(x, c, ctx, c_ctx, ada_w, ada_b, norm_g, ffn_w1, ffn_w3, ffn_w2, w_in, w_out,
           hg_lb_logits, hg_norm_g, na_rpb, sw_sink, final_g):
    lb_soft = jax.nn.softmax(hg_lb_logits.astype(F32), axis=0)
    lower_bounds = jnp.cumsum(lb_soft, axis=0) - lb_soft[0]
    w_in_p = w_in.astype(BF16)
    w_out_p = w_out.astype(BF16)
    sink_p = sw_sink.astype(F32)
    w1 = ffn_w1.astype(BF16)
    w3 = ffn_w3.astype(BF16)
    w2 = ffn_w2.astype(BF16)
    cos_t, sin_t = _rope_tables()
    na_bias = _na_bias_table(na_rpb)
    tri, ones_blk = _hgrn_consts()
    final_g2 = final_g.reshape(1, D_MODEL)
    norm_g4 = norm_g.reshape(DEPTH, 3, 1, D_MODEL)
    hg_norm_g3 = hg_norm_g.reshape(DEPTH, 1, HG_WIDTH)

    c8 = jnp.concatenate([c, c_ctx[None, :], jnp.zeros((8 - BATCH - 1, D_MODEL), F32)], axis=0)
    mods_all = _ada_call(c8, ada_w, ada_b).reshape(DEPTH, 8, N_MOD, D_MODEL)

    xs = None
    for l in range(DEPTH):
        need_ctx = l < DEPTH - 1
        if l == 0:
            xs = _ffn_call(x.reshape(N_LAT, D_MODEL), ctx.reshape(N_CTX, D_MODEL), 0,
                           mods_all, norm_g4, w1, w3, w2, layer=l)
        else:
            xs = _ffn_call(xs, xs, N_LAT // TM_FFN, mods_all, norm_g4, w1, w3, w2, layer=l)
        qv, ff, fb, sg, at = _inproj_call(xs, mods_all, norm_g4, w_in_p, lower_bounds, cos_t, sin_t, layer=l)
        o_f, o_b = _hgrn_call(qv, ff, fb, tri, ones_blk)
        o_na = _na_call(at, na_bias, layer=l, need_ctx=need_ctx)
        o_sw = _sw_call(at, sink_p[l], need_ctx)
        xs = _mix_ffn_call(xs, mods_all, o_f, o_b, sg, o_na, o_sw, w_out_p, hg_norm_g3, ones_blk,
                           norm_g4, w1, w3, w2, final_g2, layer=l, need_ctx=need_ctx)
    return xs.reshape(BATCH, SEQ, D_MODEL)
```

```python
import functools

import jax
import jax.numpy as jnp
import numpy as np
from jax import lax
from jax.experimental import pallas as pl
from jax.experimental.pallas import tpu as pltpu

F32 = jnp.float32
BF16 = jnp.bfloat16

D_MODEL = 1024
BATCH = 4
SEQ = 4096
DEPTH = 4
GRID_W = 64
CTX_LEN = 256
HEAD_DIM = 64
EPS = 1e-6
MASK_VALUE = -1e30
LOG2_E = 1.4426950408889634
ROPE_THETA = 10000.0
N_MOD = 9
D_FF = 2816
HG_WIDTH = 256
HG_HEADS = 4
NA_WIDTH = 384
NA_HEADS = 6
NA_ROWS = 8
NA_COLS = 16
SW_WIDTH = 384
SW_HEADS = 6
SW_KV_WIDTH = 128
SW_WINDOW = 128
SW_BLOCK = 128
IN_WIDTH = 3072

LANES = 128
VMEM_LIMIT_BYTES = 56 * 1024 * 1024

N_LAT = BATCH * SEQ
N_CTX = BATCH * CTX_LEN
N_TOK = N_LAT + N_CTX
SEQ_ALL = SEQ + CTX_LEN
TM_PROJ = 256
TM_FFN = 512
TM_LAT = 1024
TM_ATT = 512
ROWS_PER_STEP = TM_ATT // GRID_W
ATT_STEPS = SEQ // TM_ATT
ATT_IN_FLIGHT = 12
HG_CHUNK = 128
HG_SUB = 16
N_CHUNK_LAT = SEQ // HG_CHUNK
N_CHUNK_CTX = CTX_LEN // HG_CHUNK
N_CHUNK = N_CHUNK_LAT + N_CHUNK_CTX
NA_KEYS = NA_ROWS * GRID_W
SW_KEYS = 3 * SW_BLOCK


def _cparams(sem):
    return pltpu.CompilerParams(dimension_semantics=sem, vmem_limit_bytes=VMEM_LIMIT_BYTES)


def _silu(a):
    return a * jax.nn.sigmoid(a)


def _dot(a, b):
    return jnp.dot(a, b, preferred_element_type=F32)


def _dot_nt(a, b):
    return lax.dot_general(a, b, (((1,), (1,)), ((), ())), preferred_element_type=F32)


def _modulated_norm(x, g, shift, scale):
    ms = jnp.mean(x * x, axis=-1, keepdims=True)
    return (x * lax.rsqrt(ms + EPS)) * g * (1.0 + scale) + shift


ADA_TN = 1536


def _ada_kernel(c_ref, w_ref, b_ref, o_ref):
    s = _silu(c_ref[...]).astype(BF16)
    o_ref[...] = _dot(s, w_ref[...].astype(BF16)) + b_ref[...]


def _ada_call(c8, ada_w, ada_b):
    n_out = N_MOD * D_MODEL
    return pl.pallas_call(
        _ada_kernel,
        grid=(DEPTH, n_out // ADA_TN),
        in_specs=[
            pl.BlockSpec((8, D_MODEL), lambda l, j: (0, 0)),
            pl.BlockSpec((None, D_MODEL, ADA_TN), lambda l, j: (l, 0, j)),
            pl.BlockSpec((None, 1, ADA_TN), lambda l, j: (l, 0, j)),
        ],
        out_specs=pl.BlockSpec((None, 8, ADA_TN), lambda l, j: (l, 0, j)),
        out_shape=jax.ShapeDtypeStruct((DEPTH, 8, n_out), F32),
        compiler_params=_cparams(("parallel", "parallel")),
        name="ada_mod",
    )(c8, ada_w, ada_b.reshape(DEPTH, 1, n_out))


def _swiglu_half_step(x, mod_ref, g_ref, w1_ref, w3_ref, w2_ref, *, mod0):
    h = _modulated_norm(x, g_ref[...], mod_ref[mod0:mod0 + 1, :], mod_ref[mod0 + 1:mod0 + 2, :]).astype(BF16)
    a = _silu(_dot(h, w1_ref[...])) * _dot(h, w3_ref[...])
    return x + (0.5 * mod_ref[mod0 + 2:mod0 + 3, :]) * _dot(a.astype(BF16), w2_ref[...])


def _ffn_kernel(xl_ref, xc_ref, mod_ref, g_ref, w1_ref, w3_ref, w2_ref, o_ref):
    x = jnp.where(pl.program_id(0) < N_LAT // TM_FFN, xl_ref[...], xc_ref[...])
    o_ref[...] = _swiglu_half_step(x, mod_ref, g_ref, w1_ref, w3_ref, w2_ref, mod0=0)


def _resident(shape, lead=()):
    return pl.BlockSpec((None,) * len(lead) + tuple(shape), lambda *_: tuple(lead) + (0,) * len(shape),
                        pipeline_mode=pl.Buffered(1))


def _ffn_call(x_lat, x_ctx, ctx_tile0, mods_all, norm_g, w1, w3, w2, *, layer):
    tiles_per_batch = SEQ // TM_FFN
    lat_tiles = N_LAT // TM_FFN

    def mod_idx(i):
        return (layer, jnp.where(i < BATCH * tiles_per_batch, i // tiles_per_batch, BATCH), 0, 0)

    return pl.pallas_call(
        _ffn_kernel,
        grid=(N_TOK // TM_FFN,),
        in_specs=[
            pl.BlockSpec((TM_FFN, D_MODEL), lambda i: (jnp.minimum(i, lat_tiles - 1), 0)),
            pl.BlockSpec((TM_FFN, D_MODEL), lambda i: (ctx_tile0 + jnp.maximum(i - lat_tiles, 0), 0)),
            pl.BlockSpec((None, None, N_MOD, D_MODEL), mod_idx),
            _resident((1, D_MODEL), (layer, 0)),
            _resident((D_MODEL, D_FF), (layer, 0)),
            _resident((D_MODEL, D_FF), (layer, 0)),
            _resident((D_FF, D_MODEL), (layer, 0)),
        ],
        out_specs=pl.BlockSpec((TM_FFN, D_MODEL), lambda i: (i, 0)),
        out_shape=jax.ShapeDtypeStruct((N_TOK, D_MODEL), F32),
        compiler_params=_cparams(("parallel",)),
        name="ffn",
    )(x_lat, x_ctx, mods_all, norm_g, w1, w3, w2)


def _rope(z, cos, sin_signed, first_of_pair):
    partner = jnp.where(first_of_pair, pltpu.roll(z, LANES - 16, 1), pltpu.roll(z, 16, 1))
    return z * cos + partner * sin_signed


def _inproj_kernel(x_ref, *refs):
    _round_robin([_inproj_rows(pl.ds(r0, TM_PROJ), x_ref, *refs) for r0 in range(0, x_ref.shape[0], TM_PROJ)])


def _inproj_rows(rows, x_ref, mod_ref, g_ref, w_ref, lb_ref, cos_ref, sin_ref,
                 qv_ref, ff_ref, fb_ref, sg_ref, at_ref):
    h = _modulated_norm(x_ref[rows, :], g_ref[...], mod_ref[3:4, :], mod_ref[4:5, :]).astype(BF16)
    W = HG_WIDTH
    p_all = _dot(h, w_ref[...])
    yield

    def proj(lo, hi):
        return p_all[:, lo:hi]

    qv_ref[rows, 0:W] = _silu(proj(0, W))
    for d, dst in ((0, ff_ref), (1, fb_ref)):
        z = proj((1 + d) * W, (2 + d) * W)
        lb = lb_ref[d:d + 1, :]
        dst[rows, 0:W] = jnp.log(lb + (1.0 - lb) * jax.nn.sigmoid(z)) * LOG2_E
        dst[rows, W:2 * W] = jnp.log((1.0 - lb) * jax.nn.sigmoid(-z)) * LOG2_E
    qv_ref[rows, W:2 * W] = proj(3 * W, 4 * W)
    sg_ref[rows, :] = _silu(proj(4 * W, 5 * W))
    yield

    scale = HEAD_DIM ** -0.5
    na0 = 5 * W
    at_ref[rows, 0:NA_WIDTH] = (proj(na0, na0 + NA_WIDTH) * scale).astype(BF16)
    at_ref[rows, NA_WIDTH:3 * NA_WIDTH] = proj(na0 + NA_WIDTH, na0 + 3 * NA_WIDTH).astype(BF16)
    yield
    sw0 = na0 + 3 * NA_WIDTH
    cos = cos_ref[rows, :]
    sin = sin_ref[rows, :]
    lane = lax.broadcasted_iota(jnp.int32, (TM_PROJ, LANES), 1)
    first = (lane % 32) < 16
    z = [_rope(proj(sw0 + j * LANES, sw0 + (j + 1) * LANES), cos, sin, first)
         for j in range((SW_WIDTH + SW_KV_WIDTH) // LANES)]
    low = lane < HEAD_DIM
    z[0], z[1], z[2] = (jnp.where(low, z[0], z[1]), pltpu.roll(jnp.where(low, z[2], z[0]), HEAD_DIM, 1),
                        jnp.where(low, z[1], z[2]))
    for j in range(len(z)):
        zj = z[j] * scale if j < SW_WIDTH // LANES else z[j]
        at_ref[rows, 3 * NA_WIDTH + j * LANES:3 * NA_WIDTH + (j + 1) * LANES] = zj.astype(BF16)
    v0 = sw0 + SW_WIDTH + SW_KV_WIDTH
    at_ref[rows, 3 * NA_WIDTH + SW_WIDTH + SW_KV_WIDTH:] = proj(v0, v0 + SW_KV_WIDTH).astype(BF16)


AT_WIDTH = 3 * NA_WIDTH + SW_WIDTH + 2 * SW_KV_WIDTH


N_INPROJ_IN = 7
INPROJ_WIDTHS = (2 * HG_WIDTH, 2 * HG_WIDTH, 2 * HG_WIDTH, HG_WIDTH, AT_WIDTH)
INPROJ_DTYPES = (F32, F32, F32, F32, BF16)


def _inproj_ctx_kernel(*refs):
    _inproj_kernel(*refs[:N_INPROJ_IN], *refs[N_INPROJ_IN + len(INPROJ_WIDTHS):])


def _inproj_call(x, mods_all, norm_g, w_in, lb, cos_t, sin_t, *, layer):
    widths = INPROJ_WIDTHS
    out_shape = [jax.ShapeDtypeStruct((BATCH, SEQ_ALL, c), dt) for c, dt in zip(widths, INPROJ_DTYPES)]
    params = [_resident((1, D_MODEL), (layer, 1)), _resident((D_MODEL, IN_WIDTH), (layer,)),
              _resident((2, HG_WIDTH), (layer,))]
    lat_tiles = SEQ // TM_LAT

    latent = pl.pallas_call(
        _inproj_kernel,
        grid=(N_LAT // TM_LAT,),
        in_specs=[
            pl.BlockSpec((TM_LAT, D_MODEL), lambda i: (i, 0)),
            pl.BlockSpec((None, None, N_MOD, D_MODEL), lambda i: (layer, i // lat_tiles, 0, 0)),
            *params,
            pl.BlockSpec((TM_LAT, LANES), lambda i: (i % lat_tiles, 0)),
            pl.BlockSpec((TM_LAT, LANES), lambda i: (i % lat_tiles, 0)),
        ],
        out_specs=[pl.BlockSpec((None, TM_LAT, c), lambda i: (i // lat_tiles, i % lat_tiles, 0)) for c in widths],
        out_shape=out_shape,
        compiler_params=_cparams(("parallel",)),
        name="in_proj",
    )(x, mods_all, norm_g, w_in, lb, cos_t, sin_t)

    ctx_pos = SEQ // TM_PROJ
    return pl.pallas_call(
        _inproj_ctx_kernel,
        grid=(BATCH,),
        in_specs=[
            pl.BlockSpec((TM_PROJ, D_MODEL), lambda b: (N_LAT // TM_PROJ + b, 0)),
            pl.BlockSpec((None, None, N_MOD, D_MODEL), lambda b: (layer, BATCH, 0, 0)),
            *params,
            pl.BlockSpec((TM_PROJ, LANES), lambda b: (ctx_pos, 0)),
            pl.BlockSpec((TM_PROJ, LANES), lambda b: (ctx_pos, 0)),
            *[pl.BlockSpec(memory_space=pl.ANY) for _ in widths],
        ],
        out_specs=[pl.BlockSpec((None, TM_PROJ, c), lambda b: (b, ctx_pos, 0)) for c in widths],
        out_shape=out_shape,
        input_output_aliases={N_INPROJ_IN + k: k for k in range(len(widths))},
        compiler_params=_cparams(("parallel",)),
        name="in_proj_ctx",
    )(x, mods_all, norm_g, w_in, lb, cos_t, sin_t, *latent)


N_SUB = HG_CHUNK // HG_SUB
N_PAIR = HG_HEADS // 2
HG_MINI = HG_SUB // 2


def _hgrn_consts():
    t = np.arange(HG_CHUNK)
    same = (t[:, None] // HG_SUB) == (t[None, :] // HG_SUB)
    lower = same & (t[None, :] <= t[:, None])
    upper = same & (t[None, :] >= t[:, None])
    tri = np.stack([lower, upper]).astype(np.float32)
    d = np.arange(LANES)
    head_blocks = ((d[:, None] // HEAD_DIM) == (d[None, :] // HEAD_DIM)).astype(np.float32)
    return jnp.asarray(tri, BF16), jnp.asarray(head_blocks, BF16)


def _split2(a):
    hi = a.astype(BF16)
    lo = (a - hi.astype(F32)).astype(BF16)
    return jnp.concatenate([hi, lo], axis=1)


def _bcast_rows(a, s, block):
    n = HG_CHUNK // block
    a3 = a.reshape(n, block, LANES)
    return jnp.broadcast_to(a3[:, s:s + 1, :], (n, block, LANES)).reshape(HG_CHUNK, LANES)


def _hgrn_group(qs, v, gl, lk, tri, ones_blk, st, direction):
    fwd = direction == 0
    c = _dot(tri, _split2(gl))
    yield
    cum = c[:, 0:LANES] + c[:, LANES:]
    tot = _bcast_rows(cum, HG_SUB - 1 if fwd else 0, HG_SUB)
    row = lax.broadcasted_iota(jnp.int32, (HG_CHUNK, LANES), 0)
    t_mini = row % HG_MINI
    blk = row // HG_SUB
    v16 = v.astype(BF16)
    ck = cum - lk

    o = jnp.zeros((HG_CHUNK, LANES), F32)
    ones2 = jnp.concatenate([jnp.concatenate([ones_blk, jnp.zeros_like(ones_blk)], axis=1),
                             jnp.concatenate([jnp.zeros_like(ones_blk), ones_blk], axis=1)], axis=0)
    for s0 in range(0, HG_MINI, 2):
        w = []
        for s in (s0, s0 + 1):
            keep = (t_mini >= s) if fwd else (t_mini <= s)
            w.append(jnp.where(keep, qs * jnp.exp2(cum - _bcast_rows(ck, s, HG_MINI)), 0.0).astype(BF16))
        r = _dot(jnp.concatenate(w, axis=1), ones2)
        o = o + r[:, :LANES] * _bcast_rows(v, s0, HG_MINI) + r[:, LANES:] * _bcast_rows(v, s0 + 1, HG_MINI)
        yield

    later = ((row % HG_SUB) >= HG_MINI) if fwd else ((row % HG_SUB) < HG_MINI)
    edge = _bcast_rows(cum, HG_MINI - 1 if fwd else HG_MINI, HG_SUB)
    q_edge = jnp.where(later, qs * jnp.exp2(jnp.minimum(cum - edge, 0.0)), 0.0)
    k_edge = jnp.where(later, 0.0, jnp.exp2(jnp.minimum(edge - ck, 0.0)))
    a = _dot_nt(_pair_queries(q_edge.astype(BF16)), k_edge.astype(BF16))
    q_blk = lax.broadcasted_iota(jnp.int32, (2 * HG_CHUNK, HG_CHUNK), 0) % HG_CHUNK // HG_SUB
    k_blk = lax.broadcasted_iota(jnp.int32, (2 * HG_CHUNK, HG_CHUNK), 1) // HG_SUB
    yield
    a = jnp.where(q_blk == k_blk, a, 0.0)
    o = o + _pair_merge(_dot(a.astype(BF16), v16))
    yield

    qd = qs * jnp.exp2(cum)
    kd = jnp.exp2(tot - ck)
    dec = jnp.exp2(tot)
    k_exp = jnp.concatenate([jnp.where(blk == j, kd, 0.0).astype(BF16) for j in range(N_SUB)], axis=1)
    upd = _dot(v.T.astype(BF16), k_exp)
    yield
    head_mask = ones_blk.astype(F32)
    before = [None] * N_SUB
    for j in (range(N_SUB) if fwd else range(N_SUB - 1, -1, -1)):
        before[j] = st.astype(BF16)
        st = st * dec[j * HG_SUB:j * HG_SUB + 1, :] + upd[:, j * LANES:(j + 1) * LANES] * head_mask
    yield
    q_exp = jnp.concatenate([jnp.where(blk == j, qd, 0.0).astype(BF16) for j in range(N_SUB)], axis=1)
    o = o + _dot_nt(q_exp, jnp.concatenate(before, axis=1))
    return o, st


def _round_robin(generators):
    results = [None] * len(generators)
    live = list(range(len(generators)))
    while live:
        for k in list(live):
            try:
                next(generators[k])
            except StopIteration as done:
                results[k] = done.value
                live.remove(k)
    return results


def _hgrn_kernel(qvf_ref, ff_ref, qvb_ref, fb_ref, tri_ref, ones_ref, of_ref, ob_ref, st_scr):
    @pl.when(pl.program_id(0) == 0)
    def _():
        st_scr[...] = jnp.zeros_like(st_scr)

    ones_blk = ones_ref[...]
    W = HG_WIDTH
    def body(b, carry):
        chains = []
        for direction, (qv_ref, f_ref) in enumerate(((qvf_ref, ff_ref), (qvb_ref, fb_ref))):
            for hp in range(N_PAIR):
                c0 = hp * LANES
                chains.append(_hgrn_group(
                    qv_ref[b, :, c0:c0 + LANES], qv_ref[b, :, W + c0:W + c0 + LANES],
                    f_ref[b, :, c0:c0 + LANES], f_ref[b, :, W + c0:W + c0 + LANES],
                    tri_ref[direction], ones_blk, st_scr[b, direction * N_PAIR + hp], direction))
        results = _round_robin(chains)
        for direction, o_ref in enumerate((of_ref, ob_ref)):
            for hp in range(N_PAIR):
                o, st = results[direction * N_PAIR + hp]
                o_ref[b, :, hp * LANES:(hp + 1) * LANES] = o
                st_scr[b, direction * N_PAIR + hp] = st
        return carry

    lax.fori_loop(0, BATCH, body, 0)


def _hgrn_call(qv, ff, fb, tri, ones_blk):
    def chunked(a):
        return a.reshape(BATCH, N_CHUNK, HG_CHUNK, a.shape[-1])

    def fwd_idx(s):
        return jnp.where(s < N_CHUNK_CTX, N_CHUNK_LAT + s, s - N_CHUNK_CTX)

    def bwd_idx(s):
        return jnp.where(s < N_CHUNK_CTX, N_CHUNK - 1 - s, N_CHUNK - 1 - s)

    def spec(c, idx):
        return pl.BlockSpec((BATCH, None, HG_CHUNK, c), lambda s: (0, idx(s), 0, 0))

    out_sds = jax.ShapeDtypeStruct((BATCH, N_CHUNK, HG_CHUNK, HG_WIDTH), F32)
    o_f, o_b = pl.pallas_call(
        _hgrn_kernel,
        grid=(N_CHUNK,),
        in_specs=[
            spec(2 * HG_WIDTH, fwd_idx), spec(2 * HG_WIDTH, fwd_idx),
            spec(2 * HG_WIDTH, bwd_idx), spec(2 * HG_WIDTH, bwd_idx),
            pl.BlockSpec((2, HG_CHUNK, HG_CHUNK), lambda s: (0, 0, 0)),
            pl.BlockSpec((LANES, LANES), lambda s: (0, 0)),
        ],
        out_specs=[spec(HG_WIDTH, fwd_idx), spec(HG_WIDTH, bwd_idx)],
        out_shape=[out_sds, out_sds],
        scratch_shapes=[pltpu.VMEM((BATCH, 2 * N_PAIR, LANES, LANES), F32)],
        compiler_params=_cparams(("arbitrary",)),
        name="hgrn2",
    )(chunked(qv), chunked(ff), chunked(qv), chunked(fb), tri, ones_blk)
    return o_f.reshape(BATCH, SEQ_ALL, HG_WIDTH), o_b.reshape(BATCH, SEQ_ALL, HG_WIDTH)


def _pair_queries(q):
    lane = lax.broadcasted_iota(jnp.int32, q.shape, 1)
    zero = jnp.zeros_like(q)
    return jnp.concatenate([jnp.where(lane < HEAD_DIM, q, zero), jnp.where(lane >= HEAD_DIM, q, zero)], axis=0)


def _pair_merge(o):
    m = o.shape[0] // 2
    lane = lax.broadcasted_iota(jnp.int32, (m, LANES), 1)
    return jnp.where(lane < HEAD_DIM, o[0:m], o[m:])


def _attend(q, keys, biases, values, extra=None):
    def lane_chunks(s):
        return [s[:, c:c + LANES] for c in range(0, s.shape[1], LANES)]

    q2 = _pair_queries(q)
    scores = []
    for k, b in zip(keys, biases):
        s = _dot_nt(q2, k)
        scores.append(s if b is None else s + b)
        yield
    m = functools.reduce(jnp.maximum, [c for s in scores for c in lane_chunks(s)])
    m = jnp.max(m, axis=-1, keepdims=True)
    if extra is not None:
        m = jnp.maximum(m, extra)
    yield
    acc = None
    for s, v in zip(scores, values):
        e = jnp.exp((s - m).astype(BF16))
        pv = _dot(e, jnp.concatenate([v, jnp.ones_like(v)], axis=1))
        acc = pv if acc is None else acc + pv
        yield
    denom = acc[:, LANES:]
    if extra is not None:
        denom = denom + jnp.exp(extra - m)
    return _pair_merge(acc[:, :LANES] / denom)


def _na_kernel(q_ref, k_ref, v_ref, kc_ref, vc_ref, bias_ref, o_ref):
    j = pl.program_id(1)

    @pl.when(j < ATT_STEPS)
    def _():
        units = []
        for rr in range(ROWS_PER_STEP):
            r = j * ROWS_PER_STEP + rr
            start = jnp.clip(r - NA_ROWS // 2, 0, GRID_W - NA_ROWS)
            k0 = pl.multiple_of(start * GRID_W, GRID_W)
            t0 = start - r + (NA_ROWS - 1)
            for p in range(NA_HEADS // 2):
                c = slice(p * LANES, (p + 1) * LANES)
                bias = jnp.concatenate(
                    [jnp.concatenate([bias_ref[h, t0 + i] for i in range(0, NA_ROWS, 2)], axis=1)
                     for h in (2 * p, 2 * p + 1)], axis=0)
                units.append(_attend(q_ref[rr * GRID_W:(rr + 1) * GRID_W, c],
                                     [k_ref[pl.ds(k0, NA_KEYS), c], kc_ref[:, c]], [bias, None],
                                     [v_ref[pl.ds(k0, NA_KEYS), c], vc_ref[:, c]]))
        outs = [o for k in range(0, len(units), ATT_IN_FLIGHT) for o in _round_robin(units[k:k + ATT_IN_FLIGHT])]
        for i, o in enumerate(outs):
            rr, p = divmod(i, NA_HEADS // 2)
            o_ref[rr * GRID_W:(rr + 1) * GRID_W, p * LANES:(p + 1) * LANES] = o.astype(BF16)

    @pl.when(j == ATT_STEPS)
    def _():
        units = [_attend(q_ref[0:CTX_LEN, p * LANES:(p + 1) * LANES], [kc_ref[:, p * LANES:(p + 1) * LANES]], [None],
                         [vc_ref[:, p * LANES:(p + 1) * LANES]]) for p in range(NA_HEADS // 2)]
        for p, o in enumerate(_round_robin(units)):
            o_ref[0:CTX_LEN, p * LANES:(p + 1) * LANES] = o.astype(BF16)


def _att_steps(need_ctx):
    return ATT_STEPS + 1 if need_ctx else ATT_STEPS


def _na_call(at, bias, *, layer, need_ctx):
    w = NA_WIDTH
    return pl.pallas_call(
        _na_kernel,
        grid=(BATCH, _att_steps(need_ctx)),
        in_specs=[
            pl.BlockSpec((None, TM_ATT, w), lambda b, j: (b, j, 0)),
            pl.BlockSpec((None, SEQ, w), lambda b, j: (b, 0, 1)),
            pl.BlockSpec((None, SEQ, w), lambda b, j: (b, 0, 2)),
            pl.BlockSpec((None, CTX_LEN, w), lambda b, j: (b, SEQ // CTX_LEN, 1)),
            pl.BlockSpec((None, CTX_LEN, w), lambda b, j: (b, SEQ // CTX_LEN, 2)),
            _resident((NA_HEADS, 2 * NA_ROWS - 2, GRID_W, 2 * GRID_W), (layer,)),
        ],
        out_specs=pl.BlockSpec((None, TM_ATT, w), lambda b, j: (b, j, 0)),
        out_shape=jax.ShapeDtypeStruct((BATCH, SEQ_ALL, w), BF16),
        compiler_params=_cparams(("parallel", "arbitrary")),
        name="nbr_attn",
    )(at, at, at, at, at, bias)


def _na_bias_table(rpb):
    col = np.arange(GRID_W)
    c0 = np.clip(col - NA_COLS // 2, 0, GRID_W - NA_COLS)
    col_ok = (col[None, :] >= c0[:, None]) & (col[None, :] < c0[:, None] + NA_COLS)
    d_col = np.clip(col[None, :] - col[:, None], 1 - NA_COLS, NA_COLS - 1)
    col_sel = (d_col[None, :, :] + NA_COLS - 1 == np.arange(2 * NA_COLS - 1)[:, None, None]).astype(np.float32)
    b = jnp.einsum('lhrd,dqk->lhrqk', rpb.astype(F32), col_sel, precision=lax.Precision.HIGHEST)
    b = jnp.where(col_ok, b, MASK_VALUE)
    return jnp.concatenate([b[:, :, :-1], b[:, :, 1:]], axis=-1)


def _sw_kernel(sink_ref, q_ref, k_ref, v_ref, kc_ref, vc_ref, o_ref):
    j = pl.program_id(1)
    n_pair = SW_HEADS // 2

    def sink_col(p, m):
        row = lax.broadcasted_iota(jnp.int32, (2 * m, 1), 0)
        return jnp.where(row < m, sink_ref[p], sink_ref[p + n_pair])

    def store_pair(rows, p, o):
        ob = o.astype(BF16)
        o_ref[rows, p * HEAD_DIM:(p + 1) * HEAD_DIM] = ob[:, :HEAD_DIM]
        o_ref[rows, (p + n_pair) * HEAD_DIM:(p + n_pair + 1) * HEAD_DIM] = ob[:, HEAD_DIM:]

    @pl.when(j < ATT_STEPS)
    def _():
        units = []
        for u in range(TM_ATT // SW_BLOCK):
            n = j * (TM_ATT // SW_BLOCK) + u
            start = jnp.clip(n * SW_BLOCK - SW_BLOCK, 0, SEQ - SW_KEYS)
            k0 = pl.multiple_of(start, SW_BLOCK)
            rel = (n * SW_BLOCK - start
                   + lax.broadcasted_iota(jnp.int32, (SW_BLOCK, SW_KEYS), 0)
                   - lax.broadcasted_iota(jnp.int32, (SW_BLOCK, SW_KEYS), 1))
            band = jnp.where(jnp.abs(rel) <= SW_WINDOW, 0.0, MASK_VALUE).astype(F32)
            band2 = jnp.concatenate([band, band], axis=0)
            kw = k_ref[pl.ds(k0, SW_KEYS), :]
            vw = v_ref[pl.ds(k0, SW_KEYS), :]
            for p in range(n_pair):
                units.append(_attend(q_ref[u * SW_BLOCK:(u + 1) * SW_BLOCK, p * LANES:(p + 1) * LANES],
                                     [kw, kc_ref[...]], [band2, None], [vw, vc_ref[...]], extra=sink_col(p, SW_BLOCK)))
        half = ATT_IN_FLIGHT // 2
        outs = [o for k in range(0, len(units), half) for o in _round_robin(units[k:k + half])]
        for i, o in enumerate(outs):
            u, p = divmod(i, n_pair)
            store_pair(slice(u * SW_BLOCK, (u + 1) * SW_BLOCK), p, o)

    @pl.when(j == ATT_STEPS)
    def _():
        units = [_attend(q_ref[0:CTX_LEN, p * LANES:(p + 1) * LANES], [kc_ref[...]], [None], [vc_ref[...]],
                         extra=sink_col(p, CTX_LEN)) for p in range(n_pair)]
        for p, o in enumerate(_round_robin(units)):
            store_pair(slice(0, CTX_LEN), p, o)


def _sw_call(at, sink_perm, need_ctx):
    q_blk = 3 * NA_WIDTH // SW_WIDTH
    k_blk = (3 * NA_WIDTH + SW_WIDTH) // SW_KV_WIDTH
    grid_spec = pltpu.PrefetchScalarGridSpec(
        num_scalar_prefetch=1,
        grid=(BATCH, _att_steps(need_ctx)),
        in_specs=[
            pl.BlockSpec((None, TM_ATT, SW_WIDTH), lambda b, j, s: (b, j, q_blk)),
            pl.BlockSpec((None, SEQ, SW_KV_WIDTH), lambda b, j, s: (b, 0, k_blk)),
            pl.BlockSpec((None, SEQ, SW_KV_WIDTH), lambda b, j, s: (b, 0, k_blk + 1)),
            pl.BlockSpec((None, CTX_LEN, SW_KV_WIDTH), lambda b, j, s: (b, SEQ // CTX_LEN, k_blk)),
            pl.BlockSpec((None, CTX_LEN, SW_KV_WIDTH), lambda b, j, s: (b, SEQ // CTX_LEN, k_blk + 1)),
        ],
        out_specs=pl.BlockSpec((None, TM_ATT, SW_WIDTH), lambda b, j, s: (b, j, 0)),
    )
    return pl.pallas_call(
        _sw_kernel,
        grid_spec=grid_spec,
        out_shape=jax.ShapeDtypeStruct((BATCH, SEQ_ALL, SW_WIDTH), BF16),
        compiler_params=_cparams(("parallel", "arbitrary")),
        name="win_attn",
    )(sink_perm, at, at, at, at, at)


N_MIX_IN = 15


def _rows(ref):
    v = ref[...]
    return v.reshape(-1, v.shape[-1])


def _mix_ffn_kernel(x_ref, mod_ref, of_ref, ob_ref, sg_ref, na_ref, sw_ref, w_ref, ng_ref, ones_ref,
                    g_ref, w1_ref, w3_ref, w2_ref, fg_ref, o_ref, *, final):
    o = _rows(of_ref) + _rows(ob_ref)
    ms = jnp.concatenate(
        [_dot((o[:, c:c + LANES] * o[:, c:c + LANES]).astype(BF16), ones_ref[...]) for c in range(0, HG_WIDTH, LANES)],
        axis=1) * (1.0 / HEAD_DIM)
    hg = (o * lax.rsqrt(ms + EPS)) * ng_ref[...] * _rows(sg_ref)
    y = _dot(hg.astype(BF16), w_ref[0:HG_WIDTH, :])
    y = y + _dot(_rows(na_ref), w_ref[HG_WIDTH:HG_WIDTH + NA_WIDTH, :])
    y = y + _dot(_rows(sw_ref), w_ref[HG_WIDTH + NA_WIDTH:, :])
    x = x_ref[...] + mod_ref[5:6, :] * y
    y = _swiglu_half_step(x, mod_ref, g_ref, w1_ref, w3_ref, w2_ref, mod0=6)
    if final:
        ms = jnp.mean(y * y, axis=-1, keepdims=True)
        y = (y * lax.rsqrt(ms + EPS)) * fg_ref[...]
    o_ref[...] = y


def _mix_ffn_ctx_kernel(*refs):
    _mix_ffn_kernel(*refs[:N_MIX_IN], refs[-1], final=False)


def _mix_ffn_call(x, mods_all, o_f, o_b, sg, o_na, o_sw, w_out, ng, ones_blk, norm_g, w1, w3, w2, final_g,
                  *, layer, need_ctx):
    n_rows = N_TOK if need_ctx else N_LAT
    mixer_widths = (HG_WIDTH, HG_WIDTH, HG_WIDTH, NA_WIDTH, SW_WIDTH)
    params = [_resident((D_MODEL, D_MODEL), (layer,)), _resident((1, HG_WIDTH), (layer,)), _resident((LANES, LANES)),
              _resident((1, D_MODEL), (layer, 2)), _resident((D_MODEL, D_FF), (layer, 1)),
              _resident((D_MODEL, D_FF), (layer, 1)), _resident((D_FF, D_MODEL), (layer, 1)), _resident((1, D_MODEL))]
    operands = (x, mods_all, o_f, o_b, sg, o_na, o_sw, w_out, ng, ones_blk, norm_g, w1, w3, w2, final_g)
    assert len(operands) == N_MIX_IN
    out_shape = jax.ShapeDtypeStruct((n_rows, D_MODEL), F32)
    lat_tiles = SEQ // TM_FFN

    latent = pl.pallas_call(
        functools.partial(_mix_ffn_kernel, final=not need_ctx),
        grid=(N_LAT // TM_FFN,),
        in_specs=[
            pl.BlockSpec((TM_FFN, D_MODEL), lambda i: (i, 0)),
            pl.BlockSpec((None, None, N_MOD, D_MODEL), lambda i: (layer, i // lat_tiles, 0, 0)),
            *[pl.BlockSpec((None, TM_FFN, c), lambda i: (i // lat_tiles, i % lat_tiles, 0)) for c in mixer_widths],
            *params,
        ],
        out_specs=pl.BlockSpec((TM_FFN, D_MODEL), lambda i: (i, 0)),
        out_shape=out_shape,
        compiler_params=_cparams(("parallel",)),
        name="mix_ffn",
    )(*operands)
    if not need_ctx:
        return latent

    per_tile = TM_FFN // CTX_LEN
    ctx_pos = SEQ // CTX_LEN
    return pl.pallas_call(
        _mix_ffn_ctx_kernel,
        grid=(N_CTX // TM_FFN,),
        in_specs=[
            pl.BlockSpec((TM_FFN, D_MODEL), lambda k: (N_LAT // TM_FFN + k, 0)),
            pl.BlockSpec((None, None, N_MOD, D_MODEL), lambda k: (layer, BATCH, 0, 0)),
            *[pl.BlockSpec((per_tile, CTX_LEN, c), lambda k: (k, ctx_pos, 0)) for c in mixer_widths],
            *params,
            pl.BlockSpec(memory_space=pl.ANY),
        ],
        out_specs=pl.BlockSpec((TM_FFN, D_MODEL), lambda k: (N_LAT // TM_FFN + k, 0)),
        out_shape=out_shape,
        input_output_aliases={N_MIX_IN: 0},
        compiler_params=_cparams(("parallel",)),
        name="mix_ffn_ctx",
    )(*operands, latent)


def _rope_tables():
    pos = jnp.arange(SEQ)
    pos = jnp.stack([pos // GRID_W, pos % GRID_W], axis=-1).astype(F32)
    nf = HEAD_DIM // 4
    inv = ROPE_THETA ** (-jnp.arange(nf, dtype=F32) / nf)
    ang = pos[:, :, None] * inv
    cos, sin = jnp.cos(ang), jnp.sin(ang)
    cos_h = jnp.stack([cos, cos], axis=2).reshape(SEQ, HEAD_DIM)
    sin_h = jnp.stack([-sin, sin], axis=2).reshape(SEQ, HEAD_DIM)
    reps = LANES // HEAD_DIM
    cos_t = jnp.concatenate([jnp.tile(cos_h, (1, reps)), jnp.ones((CTX_LEN, LANES), F32)], axis=0)
    sin_t = jnp.concatenate([jnp.tile(sin_h, (1, reps)), jnp.zeros((CTX_LEN, LANES), F32)], axis=0)
    return cos_t, sin_t


def kernel(x, c, ctx, c_ctx, ada_w, ada_b, norm_g, ffn_w1, ffn_w3, ffn_w2, w_in, w_out,
           hg_lb_logits, hg_norm_g, na_rpb, sw_sink, final_g):
    lb_soft = jax.nn.softmax(hg_lb_logits.astype(F32), axis=0)
    lower_bounds = jnp.cumsum(lb_soft, axis=0) - lb_soft[0]
    w_in_p = w_in.astype(BF16)
    w_out_p = w_out.astype(BF16)
    sink_p = sw_sink.astype(F32)
    w1 = ffn_w1.astype(BF16)
    w3 = ffn_w3.astype(BF16)
    w2 = ffn_w2.astype(BF16)
    cos_t, sin_t = _rope_tables()
    na_bias = _na_bias_table(na_rpb)
    tri, ones_blk = _hgrn_consts()
    final_g2 = final_g.reshape(1, D_MODEL)
    norm_g4 = norm_g.reshape(DEPTH, 3, 1, D_MODEL)
    hg_norm_g3 = hg_norm_g.reshape(DEPTH, 1, HG_WIDTH)

    c8 = jnp.concatenate([c, c_ctx[None, :], jnp.zeros((8 - BATCH - 1, D_MODEL), F32)], axis=0)
    mods_all = _ada_call(c8, ada_w, ada_b).reshape(DEPTH, 8, N_MOD, D_MODEL)

    xs = None
    for l in range(DEPTH):
        need_ctx = l < DEPTH - 1
        if l == 0:
            xs = _ffn_call(x.reshape(N_LAT, D_MODEL), ctx.reshape(N_CTX, D_MODEL), 0,
                           mods_all, norm_g4, w1, w3, w2, layer=l)
        else:
            xs = _ffn_call(xs, xs, N_LAT // TM_FFN, mods_all, norm_g4, w1, w3, w2, layer=l)
        qv, ff, fb, sg, at = _inproj_call(xs, mods_all, norm_g4, w_in_p, lower_bounds, cos_t, sin_t, layer=l)
        o_f, o_b = _hgrn_call(qv, ff, fb, tri, ones_blk)
        o_na = _na_call(at, na_bias, layer=l, need_ctx=need_ctx)
        o_sw = _sw_call(at, sink_p[l], need_ctx)
        xs = _mix_ffn_call(xs, mods_all, o_f, o_b, sg, o_na, o_sw, w_out_p, hg_norm_g3, ones_blk,
                           norm_g4, w1, w3, w2, final_g2, layer=l, need_ctx=need_ctx)
    return xs.reshape(BATCH, SEQ, D_MODEL)
```

```python
import functools

import jax
import jax.numpy as jnp
import numpy as np
from jax import lax
from jax.experimental import pallas as pl
from jax.experimental.pallas import tpu as pltpu

F32 = jnp.float32
BF16 = jnp.bfloat16

D_MODEL = 1024
BATCH = 4
SEQ = 4096
DEPTH = 4
GRID_W = 64
CTX_LEN = 256
HEAD_DIM = 64
EPS = 1e-6
MASK_VALUE = -1e30
LOG2_E = 1.4426950408889634
ROPE_THETA = 10000.0
N_MOD = 9
D_FF = 2816
HG_WIDTH = 256
HG_HEADS = 4
NA_WIDTH = 384
NA_HEADS = 6
NA_ROWS = 8
NA_COLS = 16
SW_WIDTH = 384
SW_HEADS = 6
SW_KV_WIDTH = 128
SW_WINDOW = 128
SW_BLOCK = 128
IN_WIDTH = 3072

LANES = 128
VMEM_LIMIT_BYTES = 56 * 1024 * 1024

N_LAT = BATCH * SEQ
N_CTX = BATCH * CTX_LEN
N_TOK = N_LAT + N_CTX
SEQ_ALL = SEQ + CTX_LEN
TM_PROJ = 256
TM_FFN = 512
TM_LAT = 1024
TM_ATT = 1024
ROWS_PER_STEP = TM_ATT // GRID_W
ATT_STEPS = SEQ // TM_ATT
ATT_IN_FLIGHT = 12
HG_CHUNK = 128
HG_SUB = 16
N_CHUNK_LAT = SEQ // HG_CHUNK
N_CHUNK_CTX = CTX_LEN // HG_CHUNK
N_CHUNK = N_CHUNK_LAT + N_CHUNK_CTX
NA_KEYS = NA_ROWS * GRID_W
SW_KEYS = 3 * SW_BLOCK


def _cparams(sem):
    return pltpu.CompilerParams(dimension_semantics=sem, vmem_limit_bytes=VMEM_LIMIT_BYTES)


def _silu(a):
    return a * jax.nn.sigmoid(a)


def _dot(a, b):
    return jnp.dot(a, b, preferred_element_type=F32)


def _dot_nt(a, b):
    return lax.dot_general(a, b, (((1,), (1,)), ((), ())), preferred_element_type=F32)


def _modulated_norm(x, g, shift, scale):
    ms = jnp.mean(x * x, axis=-1, keepdims=True)
    return (x * lax.rsqrt(ms + EPS)) * g * (1.0 + scale) + shift


ADA_TN = 1536


def _ada_kernel(c_ref, w_ref, b_ref, o_ref):
    s = _silu(c_ref[...]).astype(BF16)
    o_ref[...] = _dot(s, w_ref[...].astype(BF16)) + b_ref[...]


def _ada_call(c8, ada_w, ada_b):
    n_out = N_MOD * D_MODEL
    return pl.pallas_call(
        _ada_kernel,
        grid=(DEPTH, n_out // ADA_TN),
        in_specs=[
            pl.BlockSpec((8, D_MODEL), lambda l, j: (0, 0)),
            pl.BlockSpec((None, D_MODEL, ADA_TN), lambda l, j: (l, 0, j)),
            pl.BlockSpec((None, 1, ADA_TN), lambda l, j: (l, 0, j)),
        ],
        out_specs=pl.BlockSpec((None, 8, ADA_TN), lambda l, j: (l, 0, j)),
        out_shape=jax.ShapeDtypeStruct((DEPTH, 8, n_out), F32),
        compiler_params=_cparams(("parallel", "parallel")),
        name="ada_mod",
    )(c8, ada_w, ada_b.reshape(DEPTH, 1, n_out))


def _swiglu_half_step(x, mod_ref, g_ref, w1_ref, w3_ref, w2_ref, *, mod0):
    h = _modulated_norm(x, g_ref[...], mod_ref[mod0:mod0 + 1, :], mod_ref[mod0 + 1:mod0 + 2, :]).astype(BF16)
    a = _silu(_dot(h, w1_ref[...])) * _dot(h, w3_ref[...])
    return x + (0.5 * mod_ref[mod0 + 2:mod0 + 3, :]) * _dot(a.astype(BF16), w2_ref[...])


def _ffn_kernel(xl_ref, xc_ref, mod_ref, g_ref, w1_ref, w3_ref, w2_ref, o_ref):
    x = jnp.where(pl.program_id(0) < N_LAT // TM_FFN, xl_ref[...], xc_ref[...])
    o_ref[...] = _swiglu_half_step(x, mod_ref, g_ref, w1_ref, w3_ref, w2_ref, mod0=0)


def _resident(shape, lead=()):
    return pl.BlockSpec((None,) * len(lead) + tuple(shape), lambda *_: tuple(lead) + (0,) * len(shape),
                        pipeline_mode=pl.Buffered(1))


def _ffn_call(x_lat, x_ctx, ctx_tile0, mods_all, norm_g, w1, w3, w2, *, layer):
    tiles_per_batch = SEQ // TM_FFN
    lat_tiles = N_LAT // TM_FFN

    def mod_idx(i):
        return (layer, jnp.where(i < BATCH * tiles_per_batch, i // tiles_per_batch, BATCH), 0, 0)

    return pl.pallas_call(
        _ffn_kernel,
        grid=(N_TOK // TM_FFN,),
        in_specs=[
            pl.BlockSpec((TM_FFN, D_MODEL), lambda i: (jnp.minimum(i, lat_tiles - 1), 0)),
            pl.BlockSpec((TM_FFN, D_MODEL), lambda i: (ctx_tile0 + jnp.maximum(i - lat_tiles, 0), 0)),
            pl.BlockSpec((None, None, N_MOD, D_MODEL), mod_idx),
            _resident((1, D_MODEL), (layer, 0)),
            _resident((D_MODEL, D_FF), (layer, 0)),
            _resident((D_MODEL, D_FF), (layer, 0)),
            _resident((D_FF, D_MODEL), (layer, 0)),
        ],
        out_specs=pl.BlockSpec((TM_FFN, D_MODEL), lambda i: (i, 0)),
        out_shape=jax.ShapeDtypeStruct((N_TOK, D_MODEL), F32),
        compiler_params=_cparams(("parallel",)),
        name="ffn",
    )(x_lat, x_ctx, mods_all, norm_g, w1, w3, w2)


def _rope(z, cos, sin_signed, first_of_pair):
    partner = jnp.where(first_of_pair, pltpu.roll(z, LANES - 16, 1), pltpu.roll(z, 16, 1))
    return z * cos + partner * sin_signed


def _inproj_kernel(x_ref, *refs):
    _round_robin([_inproj_rows(pl.ds(r0, TM_PROJ), x_ref, *refs) for r0 in range(0, x_ref.shape[0], TM_PROJ)])


def _inproj_rows(rows, x_ref, mod_ref, g_ref, w_ref, lb_ref, cos_ref, sin_ref,
                 qv_ref, ff_ref, fb_ref, sg_ref, at_ref):
    h = _modulated_norm(x_ref[rows, :], g_ref[...], mod_ref[3:4, :], mod_ref[4:5, :]).astype(BF16)
    W = HG_WIDTH
    p_all = _dot(h, w_ref[...])
    yield

    def proj(lo, hi):
        return p_all[:, lo:hi]

    qv_ref[rows, 0:W] = _silu(proj(0, W))
    for d, dst in ((0, ff_ref), (1, fb_ref)):
        z = proj((1 + d) * W, (2 + d) * W)
        lb = lb_ref[d:d + 1, :]
        dst[rows, 0:W] = jnp.log(lb + (1.0 - lb) * jax.nn.sigmoid(z)) * LOG2_E
        dst[rows, W:2 * W] = jnp.log((1.0 - lb) * jax.nn.sigmoid(-z)) * LOG2_E
    qv_ref[rows, W:2 * W] = proj(3 * W, 4 * W)
    sg_ref[rows, :] = _silu(proj(4 * W, 5 * W))
    yield

    scale = HEAD_DIM ** -0.5
    na0 = 5 * W
    at_ref[rows, 0:NA_WIDTH] = (proj(na0, na0 + NA_WIDTH) * scale).astype(BF16)
    at_ref[rows, NA_WIDTH:3 * NA_WIDTH] = proj(na0 + NA_WIDTH, na0 + 3 * NA_WIDTH).astype(BF16)
    yield
    sw0 = na0 + 3 * NA_WIDTH
    cos = cos_ref[rows, :]
    sin = sin_ref[rows, :]
    lane = lax.broadcasted_iota(jnp.int32, (TM_PROJ, LANES), 1)
    first = (lane % 32) < 16
    z = [_rope(proj(sw0 + j * LANES, sw0 + (j + 1) * LANES), cos, sin, first)
         for j in range((SW_WIDTH + SW_KV_WIDTH) // LANES)]
    low = lane < HEAD_DIM
    z[0], z[1], z[2] = (jnp.where(low, z[0], z[1]), pltpu.roll(jnp.where(low, z[2], z[0]), HEAD_DIM, 1),
                        jnp.where(low, z[1], z[2]))
    for j in range(len(z)):
        zj = z[j] * scale if j < SW_WIDTH // LANES else z[j]
        at_ref[rows, 3 * NA_WIDTH + j * LANES:3 * NA_WIDTH + (j + 1) * LANES] = zj.astype(BF16)
    v0 = sw0 + SW_WIDTH + SW_KV_WIDTH
    at_ref[rows, 3 * NA_WIDTH + SW_WIDTH + SW_KV_WIDTH:] = proj(v0, v0 + SW_KV_WIDTH).astype(BF16)


AT_WIDTH = 3 * NA_WIDTH + SW_WIDTH + 2 * SW_KV_WIDTH


N_INPROJ_IN = 7
INPROJ_WIDTHS = (2 * HG_WIDTH, 2 * HG_WIDTH, 2 * HG_WIDTH, HG_WIDTH, AT_WIDTH)
INPROJ_DTYPES = (F32, F32, F32, F32, BF16)


def _inproj_ctx_kernel(*refs):
    _inproj_kernel(*refs[:N_INPROJ_IN], *refs[N_INPROJ_IN + len(INPROJ_WIDTHS):])


def _inproj_call(x, mods_all, norm_g, w_in, lb, cos_t, sin_t, *, layer):
    widths = INPROJ_WIDTHS
    out_shape = [jax.ShapeDtypeStruct((BATCH, SEQ_ALL, c), dt) for c, dt in zip(widths, INPROJ_DTYPES)]
    params = [_resident((1, D_MODEL), (layer, 1)), _resident((D_MODEL, IN_WIDTH), (layer,)),
              _resident((2, HG_WIDTH), (layer,))]
    lat_tiles = SEQ // TM_LAT

    latent = pl.pallas_call(
        _inproj_kernel,
        grid=(N_LAT // TM_LAT,),
        in_specs=[
            pl.BlockSpec((TM_LAT, D_MODEL), lambda i: (i, 0)),
            pl.BlockSpec((None, None, N_MOD, D_MODEL), lambda i: (layer, i // lat_tiles, 0, 0)),
            *params,
            pl.BlockSpec((TM_LAT, LANES), lambda i: (i % lat_tiles, 0)),
            pl.BlockSpec((TM_LAT, LANES), lambda i: (i % lat_tiles, 0)),
        ],
        out_specs=[pl.BlockSpec((None, TM_LAT, c), lambda i: (i // lat_tiles, i % lat_tiles, 0)) for c in widths],
        out_shape=out_shape,
        compiler_params=_cparams(("parallel",)),
        name="in_proj",
    )(x, mods_all, norm_g, w_in, lb, cos_t, sin_t)

    ctx_pos = SEQ // TM_PROJ
    return pl.pallas_call(
        _inproj_ctx_kernel,
        grid=(BATCH,),
        in_specs=[
            pl.BlockSpec((TM_PROJ, D_MODEL), lambda b: (N_LAT // TM_PROJ + b, 0)),
            pl.BlockSpec((None, None, N_MOD, D_MODEL), lambda b: (layer, BATCH, 0, 0)),
            *params,
            pl.BlockSpec((TM_PROJ, LANES), lambda b: (ctx_pos, 0)),
            pl.BlockSpec((TM_PROJ, LANES), lambda b: (ctx_pos, 0)),
            *[pl.BlockSpec(memory_space=pl.ANY) for _ in widths],
        ],
        out_specs=[pl.BlockSpec((None, TM_PROJ, c), lambda b: (b, ctx_pos, 0)) for c in widths],
        out_shape=out_shape,
        input_output_aliases={N_INPROJ_IN + k: k for k in range(len(widths))},
        compiler_params=_cparams(("parallel",)),
        name="in_proj_ctx",
    )(x, mods_all, norm_g, w_in, lb, cos_t, sin_t, *latent)


N_SUB = HG_CHUNK // HG_SUB
N_PAIR = HG_HEADS // 2
HG_MINI = HG_SUB // 2


def _hgrn_consts():
    t = np.arange(HG_CHUNK)
    same = (t[:, None] // HG_SUB) == (t[None, :] // HG_SUB)
    lower = same & (t[None, :] <= t[:, None])
    upper = same & (t[None, :] >= t[:, None])
    tri = np.stack([lower, upper]).astype(np.float32)
    d = np.arange(LANES)
    head_blocks = ((d[:, None] // HEAD_DIM) == (d[None, :] // HEAD_DIM)).astype(np.float32)
    return jnp.asarray(tri, BF16), jnp.asarray(head_blocks, BF16)


def _split2(a):
    hi = a.astype(BF16)
    lo = (a - hi.astype(F32)).astype(BF16)
    return jnp.concatenate([hi, lo], axis=1)


def _bcast_rows(a, s, block):
    n = HG_CHUNK // block
    a3 = a.reshape(n, block, LANES)
    return jnp.broadcast_to(a3[:, s:s + 1, :], (n, block, LANES)).reshape(HG_CHUNK, LANES)


def _hgrn_group(qs, v, gl, lk, tri, ones_blk, st, direction):
    fwd = direction == 0
    c = _dot(tri, _split2(gl))
    yield
    cum = c[:, 0:LANES] + c[:, LANES:]
    tot = _bcast_rows(cum, HG_SUB - 1 if fwd else 0, HG_SUB)
    row = lax.broadcasted_iota(jnp.int32, (HG_CHUNK, LANES), 0)
    t_mini = row % HG_MINI
    blk = row // HG_SUB
    v16 = v.astype(BF16)
    ck = cum - lk

    o = jnp.zeros((HG_CHUNK, LANES), F32)
    ones2 = jnp.concatenate([jnp.concatenate([ones_blk, jnp.zeros_like(ones_blk)], axis=1),
                             jnp.concatenate([jnp.zeros_like(ones_blk), ones_blk], axis=1)], axis=0)
    for s0 in range(0, HG_MINI, 2):
        w = []
        for s in (s0, s0 + 1):
            keep = (t_mini >= s) if fwd else (t_mini <= s)
            w.append(jnp.where(keep, qs * jnp.exp2(cum - _bcast_rows(ck, s, HG_MINI)), 0.0).astype(BF16))
        r = _dot(jnp.concatenate(w, axis=1), ones2)
        o = o + r[:, :LANES] * _bcast_rows(v, s0, HG_MINI) + r[:, LANES:] * _bcast_rows(v, s0 + 1, HG_MINI)
        yield

    later = ((row % HG_SUB) >= HG_MINI) if fwd else ((row % HG_SUB) < HG_MINI)
    edge = _bcast_rows(cum, HG_MINI - 1 if fwd else HG_MINI, HG_SUB)
    q_edge = jnp.where(later, qs * jnp.exp2(jnp.minimum(cum - edge, 0.0)), 0.0)
    k_edge = jnp.where(later, 0.0, jnp.exp2(jnp.minimum(edge - ck, 0.0)))
    a = _dot_nt(_pair_queries(q_edge.astype(BF16)), k_edge.astype(BF16))
    q_blk = lax.broadcasted_iota(jnp.int32, (2 * HG_CHUNK, HG_CHUNK), 0) % HG_CHUNK // HG_SUB
    k_blk = lax.broadcasted_iota(jnp.int32, (2 * HG_CHUNK, HG_CHUNK), 1) // HG_SUB
    yield
    a = jnp.where(q_blk == k_blk, a, 0.0)
    o = o + _pair_merge(_dot(a.astype(BF16), v16))
    yield

    qd = qs * jnp.exp2(cum)
    kd = jnp.exp2(tot - ck)
    dec = jnp.exp2(tot)
    k_exp = jnp.concatenate([jnp.where(blk == j, kd, 0.0).astype(BF16) for j in range(N_SUB)], axis=1)
    upd = _dot(v.T.astype(BF16), k_exp)
    yield
    head_mask = ones_blk.astype(F32)
    before = [None] * N_SUB
    for j in (range(N_SUB) if fwd else range(N_SUB - 1, -1, -1)):
        before[j] = st.astype(BF16)
        st = st * dec[j * HG_SUB:j * HG_SUB + 1, :] + upd[:, j * LANES:(j + 1) * LANES] * head_mask
    yield
    q_exp = jnp.concatenate([jnp.where(blk == j, qd, 0.0).astype(BF16) for j in range(N_SUB)], axis=1)
    o = o + _dot_nt(q_exp, jnp.concatenate(before, axis=1))
    return o, st


def _round_robin(generators):
    results = [None] * len(generators)
    live = list(range(len(generators)))
    while live:
        for k in list(live):
            try:
                next(generators[k])
            except StopIteration as done:
                results[k] = done.value
                live.remove(k)
    return results


def _hgrn_kernel(qvf_ref, ff_ref, qvb_ref, fb_ref, tri_ref, ones_ref, of_ref, ob_ref, st_scr):
    @pl.when(pl.program_id(0) == 0)
    def _():
        st_scr[...] = jnp.zeros_like(st_scr)

    ones_blk = ones_ref[...]
    W = HG_WIDTH
    keys = [(b, direction, hp) for b in range(BATCH) for direction in range(2) for hp in range(N_PAIR)]
    chains = []
    for b, direction, hp in keys:
        qv_ref, f_ref = ((qvf_ref, ff_ref), (qvb_ref, fb_ref))[direction]
        c0 = hp * LANES
        chains.append(_hgrn_group(
            qv_ref[b, :, c0:c0 + LANES], qv_ref[b, :, W + c0:W + c0 + LANES],
            f_ref[b, :, c0:c0 + LANES], f_ref[b, :, W + c0:W + c0 + LANES],
            tri_ref[direction], ones_blk, st_scr[b, direction * N_PAIR + hp], direction))
    for (b, direction, hp), (o, st) in zip(keys, _round_robin(chains)):
        (of_ref, ob_ref)[direction][b, :, hp * LANES:(hp + 1) * LANES] = o
        st_scr[b, direction * N_PAIR + hp] = st


def _hgrn_call(qv, ff, fb, tri, ones_blk):
    def chunked(a):
        return a.reshape(BATCH, N_CHUNK, HG_CHUNK, a.shape[-1])

    def fwd_idx(s):
        return jnp.where(s < N_CHUNK_CTX, N_CHUNK_LAT + s, s - N_CHUNK_CTX)

    def bwd_idx(s):
        return jnp.where(s < N_CHUNK_CTX, N_CHUNK - 1 - s, N_CHUNK - 1 - s)

    def spec(c, idx):
        return pl.BlockSpec((BATCH, None, HG_CHUNK, c), lambda s: (0, idx(s), 0, 0))

    out_sds = jax.ShapeDtypeStruct((BATCH, N_CHUNK, HG_CHUNK, HG_WIDTH), F32)
    o_f, o_b = pl.pallas_call(
        _hgrn_kernel,
        grid=(N_CHUNK,),
        in_specs=[
            spec(2 * HG_WIDTH, fwd_idx), spec(2 * HG_WIDTH, fwd_idx),
            spec(2 * HG_WIDTH, bwd_idx), spec(2 * HG_WIDTH, bwd_idx),
            pl.BlockSpec((2, HG_CHUNK, HG_CHUNK), lambda s: (0, 0, 0)),
            pl.BlockSpec((LANES, LANES), lambda s: (0, 0)),
        ],
        out_specs=[spec(HG_WIDTH, fwd_idx), spec(HG_WIDTH, bwd_idx)],
        out_shape=[out_sds, out_sds],
        scratch_shapes=[pltpu.VMEM((BATCH, 2 * N_PAIR, LANES, LANES), F32)],
        compiler_params=_cparams(("arbitrary",)),
        name="hgrn2",
    )(chunked(qv), chunked(ff), chunked(qv), chunked(fb), tri, ones_blk)
    return o_f.reshape(BATCH, SEQ_ALL, HG_WIDTH), o_b.reshape(BATCH, SEQ_ALL, HG_WIDTH)


def _pair_queries(q):
    lane = lax.broadcasted_iota(jnp.int32, q.shape, 1)
    zero = jnp.zeros_like(q)
    return jnp.concatenate([jnp.where(lane < HEAD_DIM, q, zero), jnp.where(lane >= HEAD_DIM, q, zero)], axis=0)


def _pair_merge(o):
    m = o.shape[0] // 2
    lane = lax.broadcasted_iota(jnp.int32, (m, LANES), 1)
    return jnp.where(lane < HEAD_DIM, o[0:m], o[m:])


def _attend(q, keys, biases, values, extra=None):
    def lane_chunks(s):
        return [s[:, c:c + LANES] for c in range(0, s.shape[1], LANES)]

    q2 = _pair_queries(q)
    scores = []
    for k, b in zip(keys, biases):
        s = _dot_nt(q2, k)
        scores.append(s if b is None else s + b)
        yield
    m = functools.reduce(jnp.maximum, [c for s in scores for c in lane_chunks(s)])
    m = jnp.max(m, axis=-1, keepdims=True)
    if extra is not None:
        m = jnp.maximum(m, extra)
    yield
    acc = None
    for s, v in zip(scores, values):
        e = jnp.exp((s - m).astype(BF16))
        pv = _dot(e, jnp.concatenate([v, jnp.ones_like(v)], axis=1))
        acc = pv if acc is None else acc + pv
        yield
    denom = acc[:, LANES:]
    if extra is not None:
        denom = denom + jnp.exp(extra - m)
    return _pair_merge(acc[:, :LANES] / denom)


def _na_kernel(q_ref, k_ref, v_ref, kc_ref, vc_ref, bias_ref, o_ref):
    j = pl.program_id(1)

    @pl.when(j < ATT_STEPS)
    def _():
        units = []
        for rr in range(ROWS_PER_STEP):
            r = j * ROWS_PER_STEP + rr
            start = jnp.clip(r - NA_ROWS // 2, 0, GRID_W - NA_ROWS)
            k0 = pl.multiple_of(start * GRID_W, GRID_W)
            t0 = start - r + (NA_ROWS - 1)
            for p in range(NA_HEADS // 2):
                c = slice(p * LANES, (p + 1) * LANES)
                bias = jnp.concatenate(
                    [jnp.concatenate([bias_ref[h, t0 + i] for i in range(0, NA_ROWS, 2)], axis=1)
                     for h in (2 * p, 2 * p + 1)], axis=0)
                units.append(_attend(q_ref[rr * GRID_W:(rr + 1) * GRID_W, c],
                                     [k_ref[pl.ds(k0, NA_KEYS), c], kc_ref[:, c]], [bias, None],
                                     [v_ref[pl.ds(k0, NA_KEYS), c], vc_ref[:, c]]))
        outs = [o for k in range(0, len(units), ATT_IN_FLIGHT) for o in _round_robin(units[k:k + ATT_IN_FLIGHT])]
        for i, o in enumerate(outs):
            rr, p = divmod(i, NA_HEADS // 2)
            o_ref[rr * GRID_W:(rr + 1) * GRID_W, p * LANES:(p + 1) * LANES] = o.astype(BF16)

    @pl.when(j == ATT_STEPS)
    def _():
        units = [_attend(q_ref[0:CTX_LEN, p * LANES:(p + 1) * LANES], [kc_ref[:, p * LANES:(p + 1) * LANES]], [None],
                         [vc_ref[:, p * LANES:(p + 1) * LANES]]) for p in range(NA_HEADS // 2)]
        for p, o in enumerate(_round_robin(units)):
            o_ref[0:CTX_LEN, p * LANES:(p + 1) * LANES] = o.astype(BF16)


def _att_steps(need_ctx):
    return ATT_STEPS + 1 if need_ctx else ATT_STEPS


def _na_call(at, bias, *, layer, need_ctx):
    w = NA_WIDTH
    return pl.pallas_call(
        _na_kernel,
        grid=(BATCH, _att_steps(need_ctx)),
        in_specs=[
            pl.BlockSpec((None, TM_ATT, w), lambda b, j: (b, j, 0)),
            pl.BlockSpec((None, SEQ, w), lambda b, j: (b, 0, 1)),
            pl.BlockSpec((None, SEQ, w), lambda b, j: (b, 0, 2)),
            pl.BlockSpec((None, CTX_LEN, w), lambda b, j: (b, SEQ // CTX_LEN, 1)),
            pl.BlockSpec((None, CTX_LEN, w), lambda b, j: (b, SEQ // CTX_LEN, 2)),
            _resident((NA_HEADS, 2 * NA_ROWS - 2, GRID_W, 2 * GRID_W), (layer,)),
        ],
        out_specs=pl.BlockSpec((None, TM_ATT, w), lambda b, j: (b, j, 0)),
        out_shape=jax.ShapeDtypeStruct((BATCH, SEQ_ALL, w), BF16),
        compiler_params=_cparams(("parallel", "arbitrary")),
        name="nbr_attn",
    )(at, at, at, at, at, bias)


def _na_bias_table(rpb):
    col = np.arange(GRID_W)
    c0 = np.clip(col - NA_COLS // 2, 0, GRID_W - NA_COLS)
    col_ok = (col[None, :] >= c0[:, None]) & (col[None, :] < c0[:, None] + NA_COLS)
    d_col = np.clip(col[None, :] - col[:, None], 1 - NA_COLS, NA_COLS - 1)
    col_sel = (d_col[None, :, :] + NA_COLS - 1 == np.arange(2 * NA_COLS - 1)[:, None, None]).astype(np.float32)
    b = jnp.einsum('lhrd,dqk->lhrqk', rpb.astype(F32), col_sel, precision=lax.Precision.HIGHEST)
    b = jnp.where(col_ok, b, MASK_VALUE)
    return jnp.concatenate([b[:, :, :-1], b[:, :, 1:]], axis=-1)


def _sw_kernel(sink_ref, q_ref, k_ref, v_ref, kc_ref, vc_ref, o_ref):
    j = pl.program_id(1)
    n_pair = SW_HEADS // 2

    def sink_col(p, m):
        row = lax.broadcasted_iota(jnp.int32, (2 * m, 1), 0)
        return jnp.where(row < m, sink_ref[p], sink_ref[p + n_pair])

    def store_pair(rows, p, o):
        ob = o.astype(BF16)
        o_ref[rows, p * HEAD_DIM:(p + 1) * HEAD_DIM] = ob[:, :HEAD_DIM]
        o_ref[rows, (p + n_pair) * HEAD_DIM:(p + n_pair + 1) * HEAD_DIM] = ob[:, HEAD_DIM:]

    @pl.when(j < ATT_STEPS)
    def _():
        units = []
        for u in range(TM_ATT // SW_BLOCK):
            n = j * (TM_ATT // SW_BLOCK) + u
            start = jnp.clip(n * SW_BLOCK - SW_BLOCK, 0, SEQ - SW_KEYS)
            k0 = pl.multiple_of(start, SW_BLOCK)
            rel = (n * SW_BLOCK - start
                   + lax.broadcasted_iota(jnp.int32, (SW_BLOCK, SW_KEYS), 0)
                   - lax.broadcasted_iota(jnp.int32, (SW_BLOCK, SW_KEYS), 1))
            band = jnp.where(jnp.abs(rel) <= SW_WINDOW, 0.0, MASK_VALUE).astype(F32)
            band2 = jnp.concatenate([band, band], axis=0)
            kw = k_ref[pl.ds(k0, SW_KEYS), :]
            vw = v_ref[pl.ds(k0, SW_KEYS), :]
            for p in range(n_pair):
                units.append(_attend(q_ref[u * SW_BLOCK:(u + 1) * SW_BLOCK, p * LANES:(p + 1) * LANES],
                                     [kw, kc_ref[...]], [band2, None], [vw, vc_ref[...]], extra=sink_col(p, SW_BLOCK)))
        half = ATT_IN_FLIGHT // 2
        outs = [o for k in range(0, len(units), half) for o in _round_robin(units[k:k + half])]
        for i, o in enumerate(outs):
            u, p = divmod(i, n_pair)
            store_pair(slice(u * SW_BLOCK, (u + 1) * SW_BLOCK), p, o)

    @pl.when(j == ATT_STEPS)
    def _():
        units = [_attend(q_ref[0:CTX_LEN, p * LANES:(p + 1) * LANES], [kc_ref[...]], [None], [vc_ref[...]],
                         extra=sink_col(p, CTX_LEN)) for p in range(n_pair)]
        for p, o in enumerate(_round_robin(units)):
            store_pair(slice(0, CTX_LEN), p, o)


def _sw_call(at, sink_perm, need_ctx):
    q_blk = 3 * NA_WIDTH // SW_WIDTH
    k_blk = (3 * NA_WIDTH + SW_WIDTH) // SW_KV_WIDTH
    grid_spec = pltpu.PrefetchScalarGridSpec(
        num_scalar_prefetch=1,
        grid=(BATCH, _att_steps(need_ctx)),
        in_specs=[
            pl.BlockSpec((None, TM_ATT, SW_WIDTH), lambda b, j, s: (b, j, q_blk)),
            pl.BlockSpec((None, SEQ, SW_KV_WIDTH), lambda b, j, s: (b, 0, k_blk)),
            pl.BlockSpec((None, SEQ, SW_KV_WIDTH), lambda b, j, s: (b, 0, k_blk + 1)),
            pl.BlockSpec((None, CTX_LEN, SW_KV_WIDTH), lambda b, j, s: (b, SEQ // CTX_LEN, k_blk)),
            pl.BlockSpec((None, CTX_LEN, SW_KV_WIDTH), lambda b, j, s: (b, SEQ // CTX_LEN, k_blk + 1)),
        ],
        out_specs=pl.BlockSpec((None, TM_ATT, SW_WIDTH), lambda b, j, s: (b, j, 0)),
    )
    return pl.pallas_call(
        _sw_kernel,
        grid_spec=grid_spec,
        out_shape=jax.ShapeDtypeStruct((BATCH, SEQ_ALL, SW_WIDTH), BF16),
        compiler_params=_cparams(("parallel", "arbitrary")),
        name="win_attn",
    )(sink_perm, at, at, at, at, at)


N_MIX_IN = 15


def _rows(ref):
    v = ref[...]
    return v.reshape(-1, v.shape[-1])


def _mix_ffn_kernel(x_ref, mod_ref, of_ref, ob_ref, sg_ref, na_ref, sw_ref, w_ref, ng_ref, ones_ref,
                    g_ref, w1_ref, w3_ref, w2_ref, fg_ref, o_ref, *, final):
    o = _rows(of_ref) + _rows(ob_ref)
    ms = jnp.concatenate(
        [_dot((o[:, c:c + LANES] * o[:, c:c + LANES]).astype(BF16), ones_ref[...]) for c in range(0, HG_WIDTH, LANES)],
        axis=1) * (1.0 / HEAD_DIM)
    hg = (o * lax.rsqrt(ms + EPS)) * ng_ref[...] * _rows(sg_ref)
    y = _dot(hg.astype(BF16), w_ref[0:HG_WIDTH, :])
    y = y + _dot(_rows(na_ref), w_ref[HG_WIDTH:HG_WIDTH + NA_WIDTH, :])
    y = y + _dot(_rows(sw_ref), w_ref[HG_WIDTH + NA_WIDTH:, :])
    x = x_ref[...] + mod_ref[5:6, :] * y
    y = _swiglu_half_step(x, mod_ref, g_ref, w1_ref, w3_ref, w2_ref, mod0=6)
    if final:
        ms = jnp.mean(y * y, axis=-1, keepdims=True)
        y = (y * lax.rsqrt(ms + EPS)) * fg_ref[...]
    o_ref[...] = y


def _mix_ffn_ctx_kernel(*refs):
    _mix_ffn_kernel(*refs[:N_MIX_IN], refs[-1], final=False)


def _mix_ffn_call(x, mods_all, o_f, o_b, sg, o_na, o_sw, w_out, ng, ones_blk, norm_g, w1, w3, w2, final_g,
                  *, layer, need_ctx):
    n_rows = N_TOK if need_ctx else N_LAT
    mixer_widths = (HG_WIDTH, HG_WIDTH, HG_WIDTH, NA_WIDTH, SW_WIDTH)
    params = [_resident((D_MODEL, D_MODEL), (layer,)), _resident((1, HG_WIDTH), (layer,)), _resident((LANES, LANES)),
              _resident((1, D_MODEL), (layer, 2)), _resident((D_MODEL, D_FF), (layer, 1)),
              _resident((D_MODEL, D_FF), (layer, 1)), _resident((D_FF, D_MODEL), (layer, 1)), _resident((1, D_MODEL))]
    operands = (x, mods_all, o_f, o_b, sg, o_na, o_sw, w_out, ng, ones_blk, norm_g, w1, w3, w2, final_g)
    assert len(operands) == N_MIX_IN
    out_shape = jax.ShapeDtypeStruct((n_rows, D_MODEL), F32)
    lat_tiles = SEQ // TM_FFN

    latent = pl.pallas_call(
        functools.partial(_mix_ffn_kernel, final=not need_ctx),
        grid=(N_LAT // TM_FFN,),
        in_specs=[
            pl.BlockSpec((TM_FFN, D_MODEL), lambda i: (i, 0)),
            pl.BlockSpec((None, None, N_MOD, D_MODEL), lambda i: (layer, i // lat_tiles, 0, 0)),
            *[pl.BlockSpec((None, TM_FFN, c), lambda i: (i // lat_tiles, i % lat_tiles, 0)) for c in mixer_widths],
            *params,
        ],
        out_specs=pl.BlockSpec((TM_FFN, D_MODEL), lambda i: (i, 0)),
        out_shape=out_shape,
        compiler_params=_cparams(("parallel",)),
        name="mix_ffn",
    )(*operands)
    if not need_ctx:
        return latent

    per_tile = TM_FFN // CTX_LEN
    ctx_pos = SEQ // CTX_LEN
    return pl.pallas_call(
        _mix_ffn_ctx_kernel,
        grid=(N_CTX // TM_FFN,),
        in_specs=[
            pl.BlockSpec((TM_FFN, D_MODEL), lambda k: (N_LAT // TM_FFN + k, 0)),
            pl.BlockSpec((None, None, N_MOD, D_MODEL), lambda k: (layer, BATCH, 0, 0)),
            *[pl.BlockSpec((per_tile, CTX_LEN, c), lambda k: (k, ctx_pos, 0)) for c in mixer_widths],
            *params,
            pl.BlockSpec(memory_space=pl.ANY),
        ],
        out_specs=pl.BlockSpec((TM_FFN, D_MODEL), lambda k: (N_LAT // TM_FFN + k, 0)),
        out_shape=out_shape,
        input_output_aliases={N_MIX_IN: 0},
        compiler_params=_cparams(("parallel",)),
        name="mix_ffn_ctx",
    )(*operands, latent)


def _rope_tables():
    pos = jnp.arange(SEQ)
    pos = jnp.stack([pos // GRID_W, pos % GRID_W], axis=-1).astype(F32)
    nf = HEAD_DIM // 4
    inv = ROPE_THETA ** (-jnp.arange(nf, dtype=F32) / nf)
    ang = pos[:, :, None] * inv
    cos, sin = jnp.cos(ang), jnp.sin(ang)
    cos_h = jnp.stack([cos, cos], axis=2).reshape(SEQ, HEAD_DIM)
    sin_h = jnp.stack([-sin, sin], axis=2).reshape(SEQ, HEAD_DIM)
    reps = LANES // HEAD_DIM
    cos_t = jnp.concatenate([jnp.tile(cos_h, (1, reps)), jnp.ones((CTX_LEN, LANES), F32)], axis=0)
    sin_t = jnp.concatenate([jnp.tile(sin_h, (1, reps)), jnp.zeros((CTX_LEN, LANES), F32)], axis=0)
    return cos_t, sin_t


def kernel(x, c, ctx, c_ctx, ada_w, ada_b, norm_g, ffn_w1, ffn_w3, ffn_w2, w_in, w_out,
           hg_lb_logits, hg_norm_g, na_rpb, sw_sink, final_g):
    lb_soft = jax.nn.softmax(hg_lb_logits.astype(F32), axis=0)
    lower_bounds = jnp.cumsum(lb_soft, axis=0) - lb_soft[0]
    w_in_p = w_in.astype(BF16)
    w_out_p = w_out.astype(BF16)
    sink_p = sw_sink.astype(F32)
    w1 = ffn_w1.astype(BF16)
    w3 = ffn_w3.astype(BF16)
    w2 = ffn_w2.astype(BF16)
    cos_t, sin_t = _rope_tables()
    na_bias = _na_bias_table(na_rpb)
    tri, ones_blk = _hgrn_consts()
    final_g2 = final_g.reshape(1, D_MODEL)
    norm_g4 = norm_g.reshape(DEPTH, 3, 1, D_MODEL)
    hg_norm_g3 = hg_norm_g.reshape(DEPTH, 1, HG_WIDTH)

    c8 = jnp.concatenate([c, c_ctx[None, :], jnp.zeros((8 - BATCH - 1, D_MODEL), F32)], axis=0)
    mods_all = _ada_call(c8, ada_w, ada_b).reshape(DEPTH, 8, N_MOD, D_MODEL)

    xs = None
    for l in range(DEPTH):
        need_ctx = l < DEPTH - 1
        if l == 0:
            xs = _ffn_call(x.reshape(N_LAT, D_MODEL), ctx.reshape(N_CTX, D_MODEL), 0,
                           mods_all, norm_g4, w1, w3, w2, layer=l)
        else:
            xs = _ffn_call(xs, xs, N_LAT // TM_FFN, mods_all, norm_g4, w1, w3, w2, layer=l)
        qv, ff, fb, sg, at = _inproj_call(xs, mods_all, norm_g4, w_in_p, lower_bounds, cos_t, sin_t, layer=l)
        o_f, o_b = _hgrn_call(qv, ff, fb, tri, ones_blk)
        o_na = _na_call(at, na_bias, layer=l, need_ctx=need_ctx)
        o_sw = _sw_call(at, sink_p[l], need_ctx)
        xs = _mix_ffn_call(xs, mods_all, o_f, o_b, sg, o_na, o_sw, w_out_p, hg_norm_g3, ones_blk,
                           norm_g4, w1, w3, w2, final_g2, layer=l, need_ctx=need_ctx)
    return xs.reshape(BATCH, SEQ, D_MODEL)
```

```python
import functools

import jax
import jax.numpy as jnp
import numpy as np
from jax import lax
from jax.experimental import pallas as pl
from jax.experimental.pallas import tpu as pltpu

F32 = jnp.float32
BF16 = jnp.bfloat16

D_MODEL = 1024
BATCH = 4
SEQ = 4096
DEPTH = 4
GRID_W = 64
CTX_LEN = 256
HEAD_DIM = 64
EPS = 1e-6
MASK_VALUE = -1e30
LOG2_E = 1.4426950408889634
ROPE_THETA = 10000.0
N_MOD = 9
D_FF = 2816
HG_WIDTH = 256
HG_HEADS = 4
NA_WIDTH = 384
NA_HEADS = 6
NA_ROWS = 8
NA_COLS = 16
SW_WIDTH = 384
SW_HEADS = 6
SW_KV_WIDTH = 128
SW_WINDOW = 128
SW_BLOCK = 128
IN_WIDTH = 3072

LANES = 128
SUBLANES = 8
VMEM_LIMIT_BYTES = 56 * 1024 * 1024

N_LAT = BATCH * SEQ
N_CTX = BATCH * CTX_LEN
N_TOK = N_LAT + N_CTX
SEQ_ALL = SEQ + CTX_LEN
TM_PROJ = 256
TM_FFN = 512
TM_LAT = 1024
TM_ATT = 1024
ROWS_PER_STEP = TM_ATT // GRID_W
ATT_STEPS = SEQ // TM_ATT
ATT_IN_FLIGHT = 12
HG_CHUNK = 128
HG_STEP_CHUNKS = 2
HG_SUB = 16
N_CHUNK_LAT = SEQ // HG_CHUNK
N_CHUNK_CTX = CTX_LEN // HG_CHUNK
N_CHUNK = N_CHUNK_LAT + N_CHUNK_CTX
NA_KEYS = NA_ROWS * GRID_W
SW_KEYS = 3 * SW_BLOCK


def _cparams(sem):
    return pltpu.CompilerParams(dimension_semantics=sem, vmem_limit_bytes=VMEM_LIMIT_BYTES)


def _silu(a):
    return a * jax.nn.sigmoid(a)


def _dot(a, b):
    return jnp.dot(a, b, preferred_element_type=F32)


def _dot_nt(a, b):
    return lax.dot_general(a, b, (((1,), (1,)), ((), ())), preferred_element_type=F32)


def _modulated_norm(x, g, shift, scale):
    ms = jnp.mean(x * x, axis=-1, keepdims=True)
    return (x * lax.rsqrt(ms + EPS)) * g * (1.0 + scale) + shift


ADA_TN = 1536
MOD_ROWS = SUBLANES


def _ada_kernel(c_ref, w_ref, b_ref, o_ref):
    s = _silu(c_ref[...]).astype(BF16)
    o_ref[...] = _dot(s, w_ref[...].astype(BF16)) + b_ref[...]


def _ada_call(cond, ada_w, ada_b):
    n_out = N_MOD * D_MODEL
    return pl.pallas_call(
        _ada_kernel,
        grid=(DEPTH, n_out // ADA_TN),
        in_specs=[
            pl.BlockSpec((MOD_ROWS, D_MODEL), lambda l, j: (0, 0)),
            pl.BlockSpec((None, D_MODEL, ADA_TN), lambda l, j: (l, 0, j)),
            pl.BlockSpec((None, 1, ADA_TN), lambda l, j: (l, 0, j)),
        ],
        out_specs=pl.BlockSpec((None, MOD_ROWS, ADA_TN), lambda l, j: (l, 0, j)),
        out_shape=jax.ShapeDtypeStruct((DEPTH, MOD_ROWS, n_out), F32),
        compiler_params=_cparams(("parallel", "parallel")),
        name="ada_mod",
    )(cond, ada_w, ada_b.reshape(DEPTH, 1, n_out))


def _swiglu_half_step(x, mod_ref, g_ref, w1_ref, w3_ref, w2_ref, *, mod0):
    h = _modulated_norm(x, g_ref[...], mod_ref[mod0:mod0 + 1, :], mod_ref[mod0 + 1:mod0 + 2, :]).astype(BF16)
    a = _silu(_dot(h, w1_ref[...])) * _dot(h, w3_ref[...])
    return x + (0.5 * mod_ref[mod0 + 2:mod0 + 3, :]) * _dot(a.astype(BF16), w2_ref[...])


def _ffn_kernel(xl_ref, xc_ref, mod_ref, g_ref, w1_ref, w3_ref, w2_ref, o_ref):
    x = jnp.where(pl.program_id(0) < N_LAT // TM_FFN, xl_ref[...], xc_ref[...])
    o_ref[...] = _swiglu_half_step(x, mod_ref, g_ref, w1_ref, w3_ref, w2_ref, mod0=0)


def _resident(shape, lead=()):
    return pl.BlockSpec((None,) * len(lead) + tuple(shape), lambda *_: tuple(lead) + (0,) * len(shape),
                        pipeline_mode=pl.Buffered(1))


def _ffn_call(x_lat, x_ctx, ctx_tile0, mods_all, norm_g, w1, w3, w2, *, layer):
    tiles_per_batch = SEQ // TM_FFN
    lat_tiles = N_LAT // TM_FFN

    def mod_idx(i):
        return (layer, jnp.where(i < BATCH * tiles_per_batch, i // tiles_per_batch, BATCH), 0, 0)

    return pl.pallas_call(
        _ffn_kernel,
        grid=(N_TOK // TM_FFN,),
        in_specs=[
            pl.BlockSpec((TM_FFN, D_MODEL), lambda i: (jnp.minimum(i, lat_tiles - 1), 0)),
            pl.BlockSpec((TM_FFN, D_MODEL), lambda i: (ctx_tile0 + jnp.maximum(i - lat_tiles, 0), 0)),
            pl.BlockSpec((None, None, N_MOD, D_MODEL), mod_idx),
            _resident((1, D_MODEL), (layer, 0)),
            _resident((D_MODEL, D_FF), (layer, 0)),
            _resident((D_MODEL, D_FF), (layer, 0)),
            _resident((D_FF, D_MODEL), (layer, 0)),
        ],
        out_specs=pl.BlockSpec((TM_FFN, D_MODEL), lambda i: (i, 0)),
        out_shape=jax.ShapeDtypeStruct((N_TOK, D_MODEL), F32),
        compiler_params=_cparams(("parallel",)),
        name="ffn",
    )(x_lat, x_ctx, mods_all, norm_g, w1, w3, w2)


def _rope(z, cos, sin_signed, first_of_pair):
    partner = jnp.where(first_of_pair, pltpu.roll(z, LANES - 16, 1), pltpu.roll(z, 16, 1))
    return z * cos + partner * sin_signed


def _inproj_kernel(x_ref, *refs):
    _round_robin([_inproj_rows(pl.ds(r0, TM_PROJ), x_ref, *refs) for r0 in range(0, x_ref.shape[0], TM_PROJ)])


def _inproj_rows(rows, x_ref, mod_ref, g_ref, w_ref, lb_ref, cos_ref, sin_ref,
                 qv_ref, ff_ref, fb_ref, sg_ref, at_ref):
    h = _modulated_norm(x_ref[rows, :], g_ref[...], mod_ref[3:4, :], mod_ref[4:5, :]).astype(BF16)
    W = HG_WIDTH
    p_all = _dot(h, w_ref[...])
    yield

    def proj(lo, hi):
        return p_all[:, lo:hi]

    qv_ref[rows, 0:W] = _silu(proj(0, W))
    for d, dst in ((0, ff_ref), (1, fb_ref)):
        z = proj((1 + d) * W, (2 + d) * W)
        lb = lb_ref[d:d + 1, :]
        dst[rows, 0:W] = jnp.log(lb + (1.0 - lb) * jax.nn.sigmoid(z)) * LOG2_E
        dst[rows, W:2 * W] = jnp.log((1.0 - lb) * jax.nn.sigmoid(-z)) * LOG2_E
    qv_ref[rows, W:2 * W] = proj(3 * W, 4 * W)
    sg_ref[rows, :] = _silu(proj(4 * W, 5 * W))
    yield

    scale = HEAD_DIM ** -0.5
    na0 = 5 * W
    at_ref[rows, 0:NA_WIDTH] = (proj(na0, na0 + NA_WIDTH) * scale).astype(BF16)
    at_ref[rows, NA_WIDTH:3 * NA_WIDTH] = proj(na0 + NA_WIDTH, na0 + 3 * NA_WIDTH).astype(BF16)
    yield
    sw0 = na0 + 3 * NA_WIDTH
    cos = cos_ref[rows, :]
    sin = sin_ref[rows, :]
    lane = lax.broadcasted_iota(jnp.int32, (TM_PROJ, LANES), 1)
    first = (lane % 32) < 16
    z = [_rope(proj(sw0 + j * LANES, sw0 + (j + 1) * LANES), cos, sin, first)
         for j in range((SW_WIDTH + SW_KV_WIDTH) // LANES)]
    low = lane < HEAD_DIM
    z[0], z[1], z[2] = (jnp.where(low, z[0], z[1]), pltpu.roll(jnp.where(low, z[2], z[0]), HEAD_DIM, 1),
                        jnp.where(low, z[1], z[2]))
    for j in range(len(z)):
        zj = z[j] * scale if j < SW_WIDTH // LANES else z[j]
        at_ref[rows, 3 * NA_WIDTH + j * LANES:3 * NA_WIDTH + (j + 1) * LANES] = zj.astype(BF16)
    v0 = sw0 + SW_WIDTH + SW_KV_WIDTH
    at_ref[rows, 3 * NA_WIDTH + SW_WIDTH + SW_KV_WIDTH:] = proj(v0, v0 + SW_KV_WIDTH).astype(BF16)


AT_WIDTH = 3 * NA_WIDTH + SW_WIDTH + 2 * SW_KV_WIDTH


N_INPROJ_IN = 7
INPROJ_WIDTHS = (2 * HG_WIDTH, 2 * HG_WIDTH, 2 * HG_WIDTH, HG_WIDTH, AT_WIDTH)
INPROJ_DTYPES = (F32, F32, F32, F32, BF16)


def _inproj_ctx_kernel(*refs):
    _inproj_kernel(*refs[:N_INPROJ_IN], *refs[N_INPROJ_IN + len(INPROJ_WIDTHS):])


def _inproj_call(x, mods_all, norm_g, w_in, lb, cos_t, sin_t, *, layer):
    widths = INPROJ_WIDTHS
    out_shape = [jax.ShapeDtypeStruct((BATCH, SEQ_ALL, c), dt) for c, dt in zip(widths, INPROJ_DTYPES)]
    params = [_resident((1, D_MODEL), (layer, 1)), _resident((D_MODEL, IN_WIDTH), (layer,)),
              _resident((2, HG_WIDTH), (layer,))]
    lat_tiles = SEQ // TM_LAT

    latent = pl.pallas_call(
        _inproj_kernel,
        grid=(N_LAT // TM_LAT,),
        in_specs=[
            pl.BlockSpec((TM_LAT, D_MODEL), lambda i: (i, 0)),
            pl.BlockSpec((None, None, N_MOD, D_MODEL), lambda i: (layer, i // lat_tiles, 0, 0)),
            *params,
            pl.BlockSpec((TM_LAT, LANES), lambda i: (i % lat_tiles, 0)),
            pl.BlockSpec((TM_LAT, LANES), lambda i: (i % lat_tiles, 0)),
        ],
        out_specs=[pl.BlockSpec((None, TM_LAT, c), lambda i: (i // lat_tiles, i % lat_tiles, 0)) for c in widths],
        out_shape=out_shape,
        compiler_params=_cparams(("parallel",)),
        name="in_proj",
    )(x, mods_all, norm_g, w_in, lb, cos_t, sin_t)

    ctx_pos = SEQ // TM_PROJ
    return pl.pallas_call(
        _inproj_ctx_kernel,
        grid=(BATCH,),
        in_specs=[
            pl.BlockSpec((TM_PROJ, D_MODEL), lambda b: (N_LAT // TM_PROJ + b, 0)),
            pl.BlockSpec((None, None, N_MOD, D_MODEL), lambda b: (layer, BATCH, 0, 0)),
            *params,
            pl.BlockSpec((TM_PROJ, LANES), lambda b: (ctx_pos, 0)),
            pl.BlockSpec((TM_PROJ, LANES), lambda b: (ctx_pos, 0)),
            *[pl.BlockSpec(memory_space=pl.ANY) for _ in widths],
        ],
        out_specs=[pl.BlockSpec((None, TM_PROJ, c), lambda b: (b, ctx_pos, 0)) for c in widths],
        out_shape=out_shape,
        input_output_aliases={N_INPROJ_IN + k: k for k in range(len(widths))},
        compiler_params=_cparams(("parallel",)),
        name="in_proj_ctx",
    )(x, mods_all, norm_g, w_in, lb, cos_t, sin_t, *latent)


N_SUB = HG_CHUNK // HG_SUB
N_PAIR = HG_HEADS // 2
HG_MINI = HG_SUB // 2


def _hgrn_consts():
    t = np.arange(HG_CHUNK)
    same = (t[:, None] // HG_SUB) == (t[None, :] // HG_SUB)
    lower = same & (t[None, :] <= t[:, None])
    upper = same & (t[None, :] >= t[:, None])
    tri = np.stack([lower, upper]).astype(np.float32)
    d = np.arange(LANES)
    head_blocks = ((d[:, None] // HEAD_DIM) == (d[None, :] // HEAD_DIM)).astype(np.float32)
    return jnp.asarray(tri, BF16), jnp.asarray(head_blocks, BF16)


def _split2(a):
    hi = a.astype(BF16)
    lo = (a - hi.astype(F32)).astype(BF16)
    return jnp.concatenate([hi, lo], axis=1)


def _bcast_rows(a, s, block):
    n = HG_CHUNK // block
    a3 = a.reshape(n, block, LANES)
    return jnp.broadcast_to(a3[:, s:s + 1, :], (n, block, LANES)).reshape(HG_CHUNK, LANES)


def _hgrn_group(qs, v, gl, lk, tri, ones_blk, st, direction):
    fwd = direction == 0
    c = _dot(tri, _split2(gl))
    yield
    cum = c[:, 0:LANES] + c[:, LANES:]
    tot = _bcast_rows(cum, HG_SUB - 1 if fwd else 0, HG_SUB)
    row = lax.broadcasted_iota(jnp.int32, (HG_CHUNK, LANES), 0)
    t_mini = row % HG_MINI
    blk = row // HG_SUB
    v16 = v.astype(BF16)
    ck = cum - lk

    o = jnp.zeros((HG_CHUNK, LANES), F32)
    ones2 = jnp.concatenate([jnp.concatenate([ones_blk, jnp.zeros_like(ones_blk)], axis=1),
                             jnp.concatenate([jnp.zeros_like(ones_blk), ones_blk], axis=1)], axis=0)
    for s0 in range(0, HG_MINI, 2):
        w = []
        for s in (s0, s0 + 1):
            keep = (t_mini >= s) if fwd else (t_mini <= s)
            w.append(jnp.where(keep, qs * jnp.exp2(cum - _bcast_rows(ck, s, HG_MINI)), 0.0).astype(BF16))
        r = _dot(jnp.concatenate(w, axis=1), ones2)
        o = o + r[:, :LANES] * _bcast_rows(v, s0, HG_MINI) + r[:, LANES:] * _bcast_rows(v, s0 + 1, HG_MINI)
        yield

    later = ((row % HG_SUB) >= HG_MINI) if fwd else ((row % HG_SUB) < HG_MINI)
    edge = _bcast_rows(cum, HG_MINI - 1 if fwd else HG_MINI, HG_SUB)
    q_edge = jnp.where(later, qs * jnp.exp2(jnp.minimum(cum - edge, 0.0)), 0.0)
    k_edge = jnp.where(later, 0.0, jnp.exp2(jnp.minimum(edge - ck, 0.0)))
    a = _dot_nt(_pair_queries(q_edge.astype(BF16)), k_edge.astype(BF16))
    q_blk = lax.broadcasted_iota(jnp.int32, (2 * HG_CHUNK, HG_CHUNK), 0) % HG_CHUNK // HG_SUB
    k_blk = lax.broadcasted_iota(jnp.int32, (2 * HG_CHUNK, HG_CHUNK), 1) // HG_SUB
    yield
    a = jnp.where(q_blk == k_blk, a, 0.0)
    o = o + _pair_merge(_dot(a.astype(BF16), v16))
    yield

    qd = qs * jnp.exp2(cum)
    kd = jnp.exp2(tot - ck)
    dec = jnp.exp2(tot)
    k_exp = jnp.concatenate([jnp.where(blk == j, kd, 0.0).astype(BF16) for j in range(N_SUB)], axis=1)
    upd = _dot(v.T.astype(BF16), k_exp)
    yield
    head_mask = ones_blk.astype(F32)
    before = [None] * N_SUB
    for j in (range(N_SUB) if fwd else range(N_SUB - 1, -1, -1)):
        before[j] = st.astype(BF16)
        st = st * dec[j * HG_SUB:j * HG_SUB + 1, :] + upd[:, j * LANES:(j + 1) * LANES] * head_mask
    yield
    q_exp = jnp.concatenate([jnp.where(blk == j, qd, 0.0).astype(BF16) for j in range(N_SUB)], axis=1)
    o = o + _dot_nt(q_exp, jnp.concatenate(before, axis=1))
    return o, st


def _round_robin(generators):
    results = [None] * len(generators)
    live = list(range(len(generators)))
    while live:
        for k in list(live):
            try:
                next(generators[k])
            except StopIteration as done:
                results[k] = done.value
                live.remove(k)
    return results


def _hgrn_kernel(qvf_ref, ff_ref, qvb_ref, fb_ref, tri_ref, ones_ref, of_ref, ob_ref, st_scr):
    @pl.when(pl.program_id(0) == 0)
    def _():
        st_scr[...] = jnp.zeros_like(st_scr)

    ones_blk = ones_ref[...]
    W = HG_WIDTH
    keys = [(b, direction, hp) for b in range(BATCH) for direction in range(2) for hp in range(N_PAIR)]
    for step in range(HG_STEP_CHUNKS):
        chunk = (step, HG_STEP_CHUNKS - 1 - step)
        chains = []
        for b, direction, hp in keys:
            qv_ref, f_ref = ((qvf_ref, ff_ref), (qvb_ref, fb_ref))[direction]
            ch = chunk[direction]
            c0 = hp * LANES
            chains.append(_hgrn_group(
                qv_ref[b, ch, :, c0:c0 + LANES], qv_ref[b, ch, :, W + c0:W + c0 + LANES],
                f_ref[b, ch, :, c0:c0 + LANES], f_ref[b, ch, :, W + c0:W + c0 + LANES],
                tri_ref[direction], ones_blk, st_scr[b, direction * N_PAIR + hp], direction))
        for (b, direction, hp), (o, st) in zip(keys, _round_robin(chains)):
            (of_ref, ob_ref)[direction][b, chunk[direction], :, hp * LANES:(hp + 1) * LANES] = o
            st_scr[b, direction * N_PAIR + hp] = st


def _hgrn_call(qv, ff, fb, tri, ones_blk):
    def chunked(a):
        return a.reshape(BATCH, N_CHUNK, HG_CHUNK, a.shape[-1])

    n_steps = N_CHUNK // HG_STEP_CHUNKS
    ctx_steps = N_CHUNK_CTX // HG_STEP_CHUNKS

    def fwd_idx(s):
        return jnp.where(s < ctx_steps, n_steps - ctx_steps + s, s - ctx_steps)

    def bwd_idx(s):
        return n_steps - 1 - s

    def spec(c, idx):
        return pl.BlockSpec((BATCH, HG_STEP_CHUNKS, HG_CHUNK, c), lambda s: (0, idx(s), 0, 0))

    out_sds = jax.ShapeDtypeStruct((BATCH, N_CHUNK, HG_CHUNK, HG_WIDTH), F32)
    o_f, o_b = pl.pallas_call(
        _hgrn_kernel,
        grid=(n_steps,),
        in_specs=[
            spec(2 * HG_WIDTH, fwd_idx), spec(2 * HG_WIDTH, fwd_idx),
            spec(2 * HG_WIDTH, bwd_idx), spec(2 * HG_WIDTH, bwd_idx),
            pl.BlockSpec((2, HG_CHUNK, HG_CHUNK), lambda s: (0, 0, 0)),
            pl.BlockSpec((LANES, LANES), lambda s: (0, 0)),
        ],
        out_specs=[spec(HG_WIDTH, fwd_idx), spec(HG_WIDTH, bwd_idx)],
        out_shape=[out_sds, out_sds],
        scratch_shapes=[pltpu.VMEM((BATCH, 2 * N_PAIR, LANES, LANES), F32)],
        compiler_params=_cparams(("arbitrary",)),
        name="hgrn2",
    )(chunked(qv), chunked(ff), chunked(qv), chunked(fb), tri, ones_blk)
    return o_f.reshape(BATCH, SEQ_ALL, HG_WIDTH), o_b.reshape(BATCH, SEQ_ALL, HG_WIDTH)


def _pair_queries(q):
    lane = lax.broadcasted_iota(jnp.int32, q.shape, 1)
    zero = jnp.zeros_like(q)
    return jnp.concatenate([jnp.where(lane < HEAD_DIM, q, zero), jnp.where(lane >= HEAD_DIM, q, zero)], axis=0)


def _pair_merge(o):
    m = o.shape[0] // 2
    lane = lax.broadcasted_iota(jnp.int32, (m, LANES), 1)
    return jnp.where(lane < HEAD_DIM, o[0:m], o[m:])


def _attend(q, keys, biases, values, extra=None):
    def lane_chunks(s):
        return [s[:, c:c + LANES] for c in range(0, s.shape[1], LANES)]

    q2 = _pair_queries(q)
    scores = []
    for k, b in zip(keys, biases):
        s = _dot_nt(q2, k)
        scores.append(s if b is None else s + b)
        yield
    m = functools.reduce(jnp.maximum, [c for s in scores for c in lane_chunks(s)])
    m = jnp.max(m, axis=-1, keepdims=True)
    if extra is not None:
        m = jnp.maximum(m, extra)
    yield
    acc = None
    for s, v in zip(scores, values):
        e = jnp.exp((s - m).astype(BF16))
        pv = _dot(e, jnp.concatenate([v, jnp.ones_like(v)], axis=1))
        acc = pv if acc is None else acc + pv
        yield
    denom = acc[:, LANES:]
    if extra is not None:
        denom = denom + jnp.exp(extra - m)
    return _pair_merge(acc[:, :LANES] / denom)


def _na_kernel(q_ref, k_ref, v_ref, kc_ref, vc_ref, bias_ref, o_ref):
    j = pl.program_id(1)

    @pl.when(j < ATT_STEPS)
    def _():
        units = []
        for rr in range(ROWS_PER_STEP):
            r = j * ROWS_PER_STEP + rr
            start = jnp.clip(r - NA_ROWS // 2, 0, GRID_W - NA_ROWS)
            k0 = pl.multiple_of(start * GRID_W, GRID_W)
            t0 = start - r + (NA_ROWS - 1)
            for p in range(NA_HEADS // 2):
                c = slice(p * LANES, (p + 1) * LANES)
                bias = jnp.concatenate(
                    [jnp.concatenate([bias_ref[h, t0 + i] for i in range(0, NA_ROWS, 2)], axis=1)
                     for h in (2 * p, 2 * p + 1)], axis=0)
                units.append(_attend(q_ref[rr * GRID_W:(rr + 1) * GRID_W, c],
                                     [k_ref[pl.ds(k0, NA_KEYS), c], kc_ref[:, c]], [bias, None],
                                     [v_ref[pl.ds(k0, NA_KEYS), c], vc_ref[:, c]]))
        outs = [o for k in range(0, len(units), ATT_IN_FLIGHT) for o in _round_robin(units[k:k + ATT_IN_FLIGHT])]
        for i, o in enumerate(outs):
            rr, p = divmod(i, NA_HEADS // 2)
            o_ref[rr * GRID_W:(rr + 1) * GRID_W, p * LANES:(p + 1) * LANES] = o.astype(BF16)

    @pl.when(j == ATT_STEPS)
    def _():
        units = [_attend(q_ref[0:CTX_LEN, p * LANES:(p + 1) * LANES], [kc_ref[:, p * LANES:(p + 1) * LANES]], [None],
                         [vc_ref[:, p * LANES:(p + 1) * LANES]]) for p in range(NA_HEADS // 2)]
        for p, o in enumerate(_round_robin(units)):
            o_ref[0:CTX_LEN, p * LANES:(p + 1) * LANES] = o.astype(BF16)


def _att_steps(need_ctx):
    return ATT_STEPS + 1 if need_ctx else ATT_STEPS


def _na_call(at, bias, *, layer, need_ctx):
    w = NA_WIDTH
    return pl.pallas_call(
        _na_kernel,
        grid=(BATCH, _att_steps(need_ctx)),
        in_specs=[
            pl.BlockSpec((None, TM_ATT, w), lambda b, j: (b, j, 0)),
            pl.BlockSpec((None, SEQ, w), lambda b, j: (b, 0, 1)),
            pl.BlockSpec((None, SEQ, w), lambda b, j: (b, 0, 2)),
            pl.BlockSpec((None, CTX_LEN, w), lambda b, j: (b, SEQ // CTX_LEN, 1)),
            pl.BlockSpec((None, CTX_LEN, w), lambda b, j: (b, SEQ // CTX_LEN, 2)),
            _resident((NA_HEADS, 2 * NA_ROWS - 2, GRID_W, 2 * GRID_W), (layer,)),
        ],
        out_specs=pl.BlockSpec((None, TM_ATT, w), lambda b, j: (b, j, 0)),
        out_shape=jax.ShapeDtypeStruct((BATCH, SEQ_ALL, w), BF16),
        compiler_params=_cparams(("parallel", "arbitrary")),
        name="nbr_attn",
    )(at, at, at, at, at, bias)


def _na_bias_table(rpb):
    col = np.arange(GRID_W)
    c0 = np.clip(col - NA_COLS // 2, 0, GRID_W - NA_COLS)
    col_ok = (col[None, :] >= c0[:, None]) & (col[None, :] < c0[:, None] + NA_COLS)
    d_col = np.clip(col[None, :] - col[:, None], 1 - NA_COLS, NA_COLS - 1)
    col_sel = (d_col[None, :, :] + NA_COLS - 1 == np.arange(2 * NA_COLS - 1)[:, None, None]).astype(np.float32)
    b = jnp.einsum('lhrd,dqk->lhrqk', rpb.astype(F32), col_sel, precision=lax.Precision.HIGHEST)
    b = jnp.where(col_ok, b, MASK_VALUE)
    return jnp.concatenate([b[:, :, :-1], b[:, :, 1:]], axis=-1)


def _sw_kernel(sink_ref, q_ref, k_ref, v_ref, kc_ref, vc_ref, o_ref):
    j = pl.program_id(1)
    n_pair = SW_HEADS // 2

    def sink_col(p, m):
        row = lax.broadcasted_iota(jnp.int32, (2 * m, 1), 0)
        return jnp.where(row < m, sink_ref[p], sink_ref[p + n_pair])

    def store_pair(rows, p, o):
        ob = o.astype(BF16)
        o_ref[rows, p * HEAD_DIM:(p + 1) * HEAD_DIM] = ob[:, :HEAD_DIM]
        o_ref[rows, (p + n_pair) * HEAD_DIM:(p + n_pair + 1) * HEAD_DIM] = ob[:, HEAD_DIM:]

    @pl.when(j < ATT_STEPS)
    def _():
        units = []
        for u in range(TM_ATT // SW_BLOCK):
            n = j * (TM_ATT // SW_BLOCK) + u
            start = jnp.clip(n * SW_BLOCK - SW_BLOCK, 0, SEQ - SW_KEYS)
            k0 = pl.multiple_of(start, SW_BLOCK)
            rel = (n * SW_BLOCK - start
                   + lax.broadcasted_iota(jnp.int32, (SW_BLOCK, SW_KEYS), 0)
                   - lax.broadcasted_iota(jnp.int32, (SW_BLOCK, SW_KEYS), 1))
            band = jnp.where(jnp.abs(rel) <= SW_WINDOW, 0.0, MASK_VALUE).astype(F32)
            band2 = jnp.concatenate([band, band], axis=0)
            kw = k_ref[pl.ds(k0, SW_KEYS), :]
            vw = v_ref[pl.ds(k0, SW_KEYS), :]
            for p in range(n_pair):
                units.append(_attend(q_ref[u * SW_BLOCK:(u + 1) * SW_BLOCK, p * LANES:(p + 1) * LANES],
                                     [kw, kc_ref[...]], [band2, None], [vw, vc_ref[...]], extra=sink_col(p, SW_BLOCK)))
        half = ATT_IN_FLIGHT // 2
        outs = [o for k in range(0, len(units), half) for o in _round_robin(units[k:k + half])]
        for i, o in enumerate(outs):
            u, p = divmod(i, n_pair)
            store_pair(slice(u * SW_BLOCK, (u + 1) * SW_BLOCK), p, o)

    @pl.when(j == ATT_STEPS)
    def _():
        units = [_attend(q_ref[0:CTX_LEN, p * LANES:(p + 1) * LANES], [kc_ref[...]], [None], [vc_ref[...]],
                         extra=sink_col(p, CTX_LEN)) for p in range(n_pair)]
        for p, o in enumerate(_round_robin(units)):
            store_pair(slice(0, CTX_LEN), p, o)


def _sw_call(at, sink_perm, need_ctx):
    q_blk = 3 * NA_WIDTH // SW_WIDTH
    k_blk = (3 * NA_WIDTH + SW_WIDTH) // SW_KV_WIDTH
    grid_spec = pltpu.PrefetchScalarGridSpec(
        num_scalar_prefetch=1,
        grid=(BATCH, _att_steps(need_ctx)),
        in_specs=[
            pl.BlockSpec((None, TM_ATT, SW_WIDTH), lambda b, j, s: (b, j, q_blk)),
            pl.BlockSpec((None, SEQ, SW_KV_WIDTH), lambda b, j, s: (b, 0, k_blk)),
            pl.BlockSpec((None, SEQ, SW_KV_WIDTH), lambda b, j, s: (b, 0, k_blk + 1)),
            pl.BlockSpec((None, CTX_LEN, SW_KV_WIDTH), lambda b, j, s: (b, SEQ // CTX_LEN, k_blk)),
            pl.BlockSpec((None, CTX_LEN, SW_KV_WIDTH), lambda b, j, s: (b, SEQ // CTX_LEN, k_blk + 1)),
        ],
        out_specs=pl.BlockSpec((None, TM_ATT, SW_WIDTH), lambda b, j, s: (b, j, 0)),
    )
    return pl.pallas_call(
        _sw_kernel,
        grid_spec=grid_spec,
        out_shape=jax.ShapeDtypeStruct((BATCH, SEQ_ALL, SW_WIDTH), BF16),
        compiler_params=_cparams(("parallel", "arbitrary")),
        name="win_attn",
    )(sink_perm, at, at, at, at, at)


N_MIX_IN = 15


def _rows(ref):
    v = ref[...]
    return v.reshape(-1, v.shape[-1])


def _mix_ffn_kernel(x_ref, mod_ref, of_ref, ob_ref, sg_ref, na_ref, sw_ref, w_ref, ng_ref, ones_ref,
                    g_ref, w1_ref, w3_ref, w2_ref, fg_ref, o_ref, *, final):
    o = _rows(of_ref) + _rows(ob_ref)
    ms = jnp.concatenate(
        [_dot((o[:, c:c + LANES] * o[:, c:c + LANES]).astype(BF16), ones_ref[...]) for c in range(0, HG_WIDTH, LANES)],
        axis=1) * (1.0 / HEAD_DIM)
    hg = (o * lax.rsqrt(ms + EPS)) * ng_ref[...] * _rows(sg_ref)
    y = _dot(hg.astype(BF16), w_ref[0:HG_WIDTH, :])
    y = y + _dot(_rows(na_ref), w_ref[HG_WIDTH:HG_WIDTH + NA_WIDTH, :])
    y = y + _dot(_rows(sw_ref), w_ref[HG_WIDTH + NA_WIDTH:, :])
    x = x_ref[...] + mod_ref[5:6, :] * y
    y = _swiglu_half_step(x, mod_ref, g_ref, w1_ref, w3_ref, w2_ref, mod0=6)
    if final:
        ms = jnp.mean(y * y, axis=-1, keepdims=True)
        y = (y * lax.rsqrt(ms + EPS)) * fg_ref[...]
    o_ref[...] = y


def _mix_ffn_ctx_kernel(*refs):
    _mix_ffn_kernel(*refs[:N_MIX_IN], refs[-1], final=False)


def _mix_ffn_call(x, mods_all, o_f, o_b, sg, o_na, o_sw, w_out, ng, ones_blk, norm_g, w1, w3, w2, final_g,
                  *, layer, need_ctx):
    n_rows = N_TOK if need_ctx else N_LAT
    mixer_widths = (HG_WIDTH, HG_WIDTH, HG_WIDTH, NA_WIDTH, SW_WIDTH)
    params = [_resident((D_MODEL, D_MODEL), (layer,)), _resident((1, HG_WIDTH), (layer,)), _resident((LANES, LANES)),
              _resident((1, D_MODEL), (layer, 2)), _resident((D_MODEL, D_FF), (layer, 1)),
              _resident((D_MODEL, D_FF), (layer, 1)), _resident((D_FF, D_MODEL), (layer, 1)), _resident((1, D_MODEL))]
    operands = (x, mods_all, o_f, o_b, sg, o_na, o_sw, w_out, ng, ones_blk, norm_g, w1, w3, w2, final_g)
    assert len(operands) == N_MIX_IN
    out_shape = jax.ShapeDtypeStruct((n_rows, D_MODEL), F32)
    lat_tiles = SEQ // TM_FFN

    latent = pl.pallas_call(
        functools.partial(_mix_ffn_kernel, final=not need_ctx),
        grid=(N_LAT // TM_FFN,),
        in_specs=[
            pl.BlockSpec((TM_FFN, D_MODEL), lambda i: (i, 0)),
            pl.BlockSpec((None, None, N_MOD, D_MODEL), lambda i: (layer, i // lat_tiles, 0, 0)),
            *[pl.BlockSpec((None, TM_FFN, c), lambda i: (i // lat_tiles, i % lat_tiles, 0)) for c in mixer_widths],
            *params,
        ],
        out_specs=pl.BlockSpec((TM_FFN, D_MODEL), lambda i: (i, 0)),
        out_shape=out_shape,
        compiler_params=_cparams(("parallel",)),
        name="mix_ffn",
    )(*operands)
    if not need_ctx:
        return latent

    per_tile = TM_FFN // CTX_LEN
    ctx_pos = SEQ // CTX_LEN
    return pl.pallas_call(
        _mix_ffn_ctx_kernel,
        grid=(N_CTX // TM_FFN,),
        in_specs=[
            pl.BlockSpec((TM_FFN, D_MODEL), lambda k: (N_LAT // TM_FFN + k, 0)),
            pl.BlockSpec((None, None, N_MOD, D_MODEL), lambda k: (layer, BATCH, 0, 0)),
            *[pl.BlockSpec((per_tile, CTX_LEN, c), lambda k: (k, ctx_pos, 0)) for c in mixer_widths],
            *params,
            pl.BlockSpec(memory_space=pl.ANY),
        ],
        out_specs=pl.BlockSpec((TM_FFN, D_MODEL), lambda k: (N_LAT // TM_FFN + k, 0)),
        out_shape=out_shape,
        input_output_aliases={N_MIX_IN: 0},
        compiler_params=_cparams(("parallel",)),
        name="mix_ffn_ctx",
    )(*operands, latent)


def _rope_tables():
    pos = jnp.arange(SEQ)
    pos = jnp.stack([pos // GRID_W, pos % GRID_W], axis=-1).astype(F32)
    nf = HEAD_DIM // 4
    inv = ROPE_THETA ** (-jnp.arange(nf, dtype=F32) / nf)
    ang = pos[:, :, None] * inv
    cos, sin = jnp.cos(ang), jnp.sin(ang)
    cos_h = jnp.stack([cos, cos], axis=2).reshape(SEQ, HEAD_DIM)
    sin_h = jnp.stack([-sin, sin], axis=2).reshape(SEQ, HEAD_DIM)
    reps = LANES // HEAD_DIM
    cos_t = jnp.concatenate([jnp.tile(cos_h, (1, reps)), jnp.ones((CTX_LEN, LANES), F32)], axis=0)
    sin_t = jnp.concatenate([jnp.tile(sin_h, (1, reps)), jnp.zeros((CTX_LEN, LANES), F32)], axis=0)
    return cos_t, sin_t


def kernel(x, c, ctx, c_ctx, ada_w, ada_b, norm_g, ffn_w1, ffn_w3, ffn_w2, w_in, w_out,
           hg_lb_logits, hg_norm_g, na_rpb, sw_sink, final_g):
    lb_soft = jax.nn.softmax(hg_lb_logits.astype(F32), axis=0)
    lower_bounds = jnp.cumsum(lb_soft, axis=0) - lb_soft[0]
    w_in_p = w_in.astype(BF16)
    w_out_p = w_out.astype(BF16)
    sink_p = sw_sink.astype(F32)
    w1 = ffn_w1.astype(BF16)
    w3 = ffn_w3.astype(BF16)
    w2 = ffn_w2.astype(BF16)
    cos_t, sin_t = _rope_tables()
    na_bias = _na_bias_table(na_rpb)
    tri, ones_blk = _hgrn_consts()
    final_g2 = final_g.reshape(1, D_MODEL)
    norm_g4 = norm_g.reshape(DEPTH, 3, 1, D_MODEL)
    hg_norm_g3 = hg_norm_g.reshape(DEPTH, 1, HG_WIDTH)

    cond = jnp.concatenate([c, c_ctx[None, :], jnp.zeros((MOD_ROWS - BATCH - 1, D_MODEL), F32)], axis=0)
    mods_all = _ada_call(cond, ada_w, ada_b).reshape(DEPTH, MOD_ROWS, N_MOD, D_MODEL)

    xs = None
    for l in range(DEPTH):
        need_ctx = l < DEPTH - 1
        if l == 0:
            xs = _ffn_call(x.reshape(N_LAT, D_MODEL), ctx.reshape(N_CTX, D_MODEL), 0,
                           mods_all, norm_g4, w1, w3, w2, layer=l)
        else:
            xs = _ffn_call(xs, xs, N_LAT // TM_FFN, mods_all, norm_g4, w1, w3, w2, layer=l)
        qv, ff, fb, sg, at = _inproj_call(xs, mods_all, norm_g4, w_in_p, lower_bounds, cos_t, sin_t, layer=l)
        o_f, o_b = _hgrn_call(qv, ff, fb, tri, ones_blk)
        o_na = _na_call(at, na_bias, layer=l, need_ctx=need_ctx)
        o_sw = _sw_call(at, sink_p[l], need_ctx)
        xs = _mix_ffn_call(xs, mods_all, o_f, o_b, sg, o_na, o_sw, w_out_p, hg_norm_g3, ones_blk,
                           norm_g4, w1, w3, w2, final_g2, layer=l, need_ctx=need_ctx)
    return xs.reshape(BATCH, SEQ, D_MODEL)
```

```python
import functools

import jax
import jax.numpy as jnp
import numpy as np
from jax import lax
from jax.experimental import pallas as pl
from jax.experimental.pallas import tpu as pltpu

F32 = jnp.float32
BF16 = jnp.bfloat16

D_MODEL = 1024
BATCH = 4
SEQ = 4096
DEPTH = 4
GRID_W = 64
CTX_LEN = 256
HEAD_DIM = 64
EPS = 1e-6
MASK_VALUE = -1e30
LOG2_E = 1.4426950408889634
ROPE_THETA = 10000.0
N_MOD = 9
D_FF = 2816
HG_WIDTH = 256
HG_HEADS = 4
NA_WIDTH = 384
NA_HEADS = 6
NA_ROWS = 8
NA_COLS = 16
SW_WIDTH = 384
SW_HEADS = 6
SW_KV_WIDTH = 128
SW_WINDOW = 128
SW_BLOCK = 128
IN_WIDTH = 3072

LANES = 128
SUBLANES = 8
VMEM_LIMIT_BYTES = 56 * 1024 * 1024

N_LAT = BATCH * SEQ
N_CTX = BATCH * CTX_LEN
N_TOK = N_LAT + N_CTX
SEQ_ALL = SEQ + CTX_LEN
TM_PROJ = 256
TM_FFN = 512
FFN_SLAB = 128
TM_LAT = 1024
INPROJ_SLAB = 128
TM_ATT = 1024
ROWS_PER_STEP = TM_ATT // GRID_W
ATT_STEPS = SEQ // TM_ATT
ATT_IN_FLIGHT = 12
HG_CHUNK = 128
HG_STEP_CHUNKS = 2
HG_SUB = 16
N_CHUNK_LAT = SEQ // HG_CHUNK
N_CHUNK_CTX = CTX_LEN // HG_CHUNK
N_CHUNK = N_CHUNK_LAT + N_CHUNK_CTX
NA_KEYS = NA_ROWS * GRID_W
SW_KEYS = 3 * SW_BLOCK


def _cparams(sem):
    return pltpu.CompilerParams(dimension_semantics=sem, vmem_limit_bytes=VMEM_LIMIT_BYTES)


def _silu(a):
    return a * jax.nn.sigmoid(a)


def _dot(a, b):
    return jnp.dot(a, b, preferred_element_type=F32)


def _dot_nt(a, b):
    return lax.dot_general(a, b, (((1,), (1,)), ((), ())), preferred_element_type=F32)


def _modulated_norm(x, g, shift, scale):
    ms = jnp.mean(x * x, axis=-1, keepdims=True)
    return (x * lax.rsqrt(ms + EPS)) * g * (1.0 + scale) + shift


ADA_TN = 1536
MOD_ROWS = SUBLANES


def _ada_kernel(c_ref, w_ref, b_ref, o_ref):
    s = _silu(c_ref[...]).astype(BF16)
    o_ref[...] = _dot(s, w_ref[...].astype(BF16)) + b_ref[...]


def _ada_call(cond, ada_w, ada_b):
    n_out = N_MOD * D_MODEL
    return pl.pallas_call(
        _ada_kernel,
        grid=(DEPTH, n_out // ADA_TN),
        in_specs=[
            pl.BlockSpec((MOD_ROWS, D_MODEL), lambda l, j: (0, 0)),
            pl.BlockSpec((None, D_MODEL, ADA_TN), lambda l, j: (l, 0, j)),
            pl.BlockSpec((None, 1, ADA_TN), lambda l, j: (l, 0, j)),
        ],
        out_specs=pl.BlockSpec((None, MOD_ROWS, ADA_TN), lambda l, j: (l, 0, j)),
        out_shape=jax.ShapeDtypeStruct((DEPTH, MOD_ROWS, n_out), F32),
        compiler_params=_cparams(("parallel", "parallel")),
        name="ada_mod",
    )(cond, ada_w, ada_b.reshape(DEPTH, 1, n_out))


def _swiglu_half_step(x, mod_ref, g_ref, w1_ref, w3_ref, w2_ref, *, mod0):
    def slab(xs):
        h = _modulated_norm(xs, g_ref[...], mod_ref[mod0:mod0 + 1, :], mod_ref[mod0 + 1:mod0 + 2, :]).astype(BF16)
        yield
        a1 = _dot(h, w1_ref[...])
        a3 = _dot(h, w3_ref[...])
        yield
        a = (_silu(a1) * a3).astype(BF16)
        yield
        return xs + (0.5 * mod_ref[mod0 + 2:mod0 + 3, :]) * _dot(a, w2_ref[...])

    return jnp.concatenate(_round_robin([slab(x[r:r + FFN_SLAB]) for r in range(0, x.shape[0], FFN_SLAB)]), axis=0)


def _ffn_kernel(xl_ref, xc_ref, mod_ref, g_ref, w1_ref, w3_ref, w2_ref, o_ref):
    x = jnp.where(pl.program_id(0) < N_LAT // TM_FFN, xl_ref[...], xc_ref[...])
    o_ref[...] = _swiglu_half_step(x, mod_ref, g_ref, w1_ref, w3_ref, w2_ref, mod0=0)


def _resident(shape, lead=()):
    return pl.BlockSpec((None,) * len(lead) + tuple(shape), lambda *_: tuple(lead) + (0,) * len(shape),
                        pipeline_mode=pl.Buffered(1))


def _ffn_call(x_lat, x_ctx, ctx_tile0, mods_all, norm_g, w1, w3, w2, *, layer):
    tiles_per_batch = SEQ // TM_FFN
    lat_tiles = N_LAT // TM_FFN

    def mod_idx(i):
        return (layer, jnp.where(i < BATCH * tiles_per_batch, i // tiles_per_batch, BATCH), 0, 0)

    return pl.pallas_call(
        _ffn_kernel,
        grid=(N_TOK // TM_FFN,),
        in_specs=[
            pl.BlockSpec((TM_FFN, D_MODEL), lambda i: (jnp.minimum(i, lat_tiles - 1), 0)),
            pl.BlockSpec((TM_FFN, D_MODEL), lambda i: (ctx_tile0 + jnp.maximum(i - lat_tiles, 0), 0)),
            pl.BlockSpec((None, None, N_MOD, D_MODEL), mod_idx),
            _resident((1, D_MODEL), (layer, 0)),
            _resident((D_MODEL, D_FF), (layer, 0)),
            _resident((D_MODEL, D_FF), (layer, 0)),
            _resident((D_FF, D_MODEL), (layer, 0)),
        ],
        out_specs=pl.BlockSpec((TM_FFN, D_MODEL), lambda i: (i, 0)),
        out_shape=jax.ShapeDtypeStruct((N_TOK, D_MODEL), F32),
        compiler_params=_cparams(("parallel",)),
        name="ffn",
    )(x_lat, x_ctx, mods_all, norm_g, w1, w3, w2)


def _rope(z, cos, sin_signed, first_of_pair):
    partner = jnp.where(first_of_pair, pltpu.roll(z, LANES - 16, 1), pltpu.roll(z, 16, 1))
    return z * cos + partner * sin_signed


def _inproj_kernel(x_ref, *refs):
    n = min(INPROJ_SLAB, x_ref.shape[0])
    _round_robin([_inproj_rows(pl.ds(r0, n), x_ref, *refs) for r0 in range(0, x_ref.shape[0], n)])


def _inproj_rows(rows, x_ref, mod_ref, g_ref, w_ref, lb_ref, cos_ref, sin_ref,
                 qv_ref, ff_ref, fb_ref, sg_ref, at_ref):
    h = _modulated_norm(x_ref[rows, :], g_ref[...], mod_ref[3:4, :], mod_ref[4:5, :]).astype(BF16)
    W = HG_WIDTH
    p_all = _dot(h, w_ref[...])
    yield

    def proj(lo, hi):
        return p_all[:, lo:hi]

    qv_ref[rows, 0:W] = _silu(proj(0, W))
    for d, dst in ((0, ff_ref), (1, fb_ref)):
        z = proj((1 + d) * W, (2 + d) * W)
        lb = lb_ref[d:d + 1, :]
        dst[rows, 0:W] = jnp.log(lb + (1.0 - lb) * jax.nn.sigmoid(z)) * LOG2_E
        dst[rows, W:2 * W] = jnp.log((1.0 - lb) * jax.nn.sigmoid(-z)) * LOG2_E
    qv_ref[rows, W:2 * W] = proj(3 * W, 4 * W)
    sg_ref[rows, :] = _silu(proj(4 * W, 5 * W))
    yield

    scale = HEAD_DIM ** -0.5
    na0 = 5 * W
    at_ref[rows, 0:NA_WIDTH] = (proj(na0, na0 + NA_WIDTH) * scale).astype(BF16)
    at_ref[rows, NA_WIDTH:3 * NA_WIDTH] = proj(na0 + NA_WIDTH, na0 + 3 * NA_WIDTH).astype(BF16)
    yield
    sw0 = na0 + 3 * NA_WIDTH
    cos = cos_ref[rows, :]
    sin = sin_ref[rows, :]
    lane = lax.broadcasted_iota(jnp.int32, (rows.size, LANES), 1)
    first = (lane % 32) < 16
    z = [_rope(proj(sw0 + j * LANES, sw0 + (j + 1) * LANES), cos, sin, first)
         for j in range((SW_WIDTH + SW_KV_WIDTH) // LANES)]
    low = lane < HEAD_DIM
    z[0], z[1], z[2] = (jnp.where(low, z[0], z[1]), pltpu.roll(jnp.where(low, z[2], z[0]), HEAD_DIM, 1),
                        jnp.where(low, z[1], z[2]))
    for j in range(len(z)):
        zj = z[j] * scale if j < SW_WIDTH // LANES else z[j]
        at_ref[rows, 3 * NA_WIDTH + j * LANES:3 * NA_WIDTH + (j + 1) * LANES] = zj.astype(BF16)
    v0 = sw0 + SW_WIDTH + SW_KV_WIDTH
    at_ref[rows, 3 * NA_WIDTH + SW_WIDTH + SW_KV_WIDTH:] = proj(v0, v0 + SW_KV_WIDTH).astype(BF16)


AT_WIDTH = 3 * NA_WIDTH + SW_WIDTH + 2 * SW_KV_WIDTH


N_INPROJ_IN = 7
INPROJ_WIDTHS = (2 * HG_WIDTH, 2 * HG_WIDTH, 2 * HG_WIDTH, HG_WIDTH, AT_WIDTH)
INPROJ_DTYPES = (F32, F32, F32, F32, BF16)


def _inproj_ctx_kernel(*refs):
    _inproj_kernel(*refs[:N_INPROJ_IN], *refs[N_INPROJ_IN + len(INPROJ_WIDTHS):])


def _inproj_call(x, mods_all, norm_g, w_in, lb, cos_t, sin_t, *, layer):
    widths = INPROJ_WIDTHS
    out_shape = [jax.ShapeDtypeStruct((BATCH, SEQ_ALL, c), dt) for c, dt in zip(widths, INPROJ_DTYPES)]
    params = [_resident((1, D_MODEL), (layer, 1)), _resident((D_MODEL, IN_WIDTH), (layer,)),
              _resident((2, HG_WIDTH), (layer,))]
    lat_tiles = SEQ // TM_LAT

    latent = pl.pallas_call(
        _inproj_kernel,
        grid=(N_LAT // TM_LAT,),
        in_specs=[
            pl.BlockSpec((TM_LAT, D_MODEL), lambda i: (i, 0)),
            pl.BlockSpec((None, None, N_MOD, D_MODEL), lambda i: (layer, i // lat_tiles, 0, 0)),
            *params,
            pl.BlockSpec((TM_LAT, LANES), lambda i: (i % lat_tiles, 0)),
            pl.BlockSpec((TM_LAT, LANES), lambda i: (i % lat_tiles, 0)),
        ],
        out_specs=[pl.BlockSpec((None, TM_LAT, c), lambda i: (i // lat_tiles, i % lat_tiles, 0)) for c in widths],
        out_shape=out_shape,
        compiler_params=_cparams(("parallel",)),
        name="in_proj",
    )(x, mods_all, norm_g, w_in, lb, cos_t, sin_t)

    ctx_pos = SEQ // TM_PROJ
    return pl.pallas_call(
        _inproj_ctx_kernel,
        grid=(BATCH,),
        in_specs=[
            pl.BlockSpec((TM_PROJ, D_MODEL), lambda b: (N_LAT // TM_PROJ + b, 0)),
            pl.BlockSpec((None, None, N_MOD, D_MODEL), lambda b: (layer, BATCH, 0, 0)),
            *params,
            pl.BlockSpec((TM_PROJ, LANES), lambda b: (ctx_pos, 0)),
            pl.BlockSpec((TM_PROJ, LANES), lambda b: (ctx_pos, 0)),
            *[pl.BlockSpec(memory_space=pl.ANY) for _ in widths],
        ],
        out_specs=[pl.BlockSpec((None, TM_PROJ, c), lambda b: (b, ctx_pos, 0)) for c in widths],
        out_shape=out_shape,
        input_output_aliases={N_INPROJ_IN + k: k for k in range(len(widths))},
        compiler_params=_cparams(("parallel",)),
        name="in_proj_ctx",
    )(x, mods_all, norm_g, w_in, lb, cos_t, sin_t, *latent)


N_SUB = HG_CHUNK // HG_SUB
N_PAIR = HG_HEADS // 2
HG_MINI = HG_SUB // 2


def _hgrn_consts():
    t = np.arange(HG_CHUNK)
    same = (t[:, None] // HG_SUB) == (t[None, :] // HG_SUB)
    lower = same & (t[None, :] <= t[:, None])
    upper = same & (t[None, :] >= t[:, None])
    tri = np.stack([lower, upper]).astype(np.float32)
    d = np.arange(LANES)
    head_blocks = ((d[:, None] // HEAD_DIM) == (d[None, :] // HEAD_DIM)).astype(np.float32)
    return jnp.asarray(tri, BF16), jnp.asarray(head_blocks, BF16)


def _split2(a):
    hi = a.astype(BF16)
    lo = (a - hi.astype(F32)).astype(BF16)
    return jnp.concatenate([hi, lo], axis=1)


def _bcast_rows(a, s, block):
    n = HG_CHUNK // block
    a3 = a.reshape(n, block, LANES)
    return jnp.broadcast_to(a3[:, s:s + 1, :], (n, block, LANES)).reshape(HG_CHUNK, LANES)


def _hgrn_group(qs, v, gl, lk, tri, ones_blk, st, direction):
    fwd = direction == 0
    c = _dot(tri, _split2(gl))
    yield
    cum = c[:, 0:LANES] + c[:, LANES:]
    tot = _bcast_rows(cum, HG_SUB - 1 if fwd else 0, HG_SUB)
    row = lax.broadcasted_iota(jnp.int32, (HG_CHUNK, LANES), 0)
    t_mini = row % HG_MINI
    blk = row // HG_SUB
    v16 = v.astype(BF16)
    ck = cum - lk

    o = jnp.zeros((HG_CHUNK, LANES), F32)
    ones2 = jnp.concatenate([jnp.concatenate([ones_blk, jnp.zeros_like(ones_blk)], axis=1),
                             jnp.concatenate([jnp.zeros_like(ones_blk), ones_blk], axis=1)], axis=0)
    for s0 in range(0, HG_MINI, 2):
        w = []
        for s in (s0, s0 + 1):
            keep = (t_mini >= s) if fwd else (t_mini <= s)
            w.append(jnp.where(keep, qs * jnp.exp2(cum - _bcast_rows(ck, s, HG_MINI)), 0.0).astype(BF16))
        r = _dot(jnp.concatenate(w, axis=1), ones2)
        o = o + r[:, :LANES] * _bcast_rows(v, s0, HG_MINI) + r[:, LANES:] * _bcast_rows(v, s0 + 1, HG_MINI)
        yield

    later = ((row % HG_SUB) >= HG_MINI) if fwd else ((row % HG_SUB) < HG_MINI)
    edge = _bcast_rows(cum, HG_MINI - 1 if fwd else HG_MINI, HG_SUB)
    q_edge = jnp.where(later, qs * jnp.exp2(jnp.minimum(cum - edge, 0.0)), 0.0)
    k_edge = jnp.where(later, 0.0, jnp.exp2(jnp.minimum(edge - ck, 0.0)))
    a = _dot_nt(_pair_queries(q_edge.astype(BF16)), k_edge.astype(BF16))
    q_blk = lax.broadcasted_iota(jnp.int32, (2 * HG_CHUNK, HG_CHUNK), 0) % HG_CHUNK // HG_SUB
    k_blk = lax.broadcasted_iota(jnp.int32, (2 * HG_CHUNK, HG_CHUNK), 1) // HG_SUB
    yield
    a = jnp.where(q_blk == k_blk, a, 0.0)
    o = o + _pair_merge(_dot(a.astype(BF16), v16))
    yield

    qd = qs * jnp.exp2(cum)
    kd = jnp.exp2(tot - ck)
    dec = jnp.exp2(tot)
    k_exp = jnp.concatenate([jnp.where(blk == j, kd, 0.0).astype(BF16) for j in range(N_SUB)], axis=1)
    upd = _dot(v.T.astype(BF16), k_exp)
    yield
    head_mask = ones_blk.astype(F32)
    before = [None] * N_SUB
    for j in (range(N_SUB) if fwd else range(N_SUB - 1, -1, -1)):
        before[j] = st.astype(BF16)
        st = st * dec[j * HG_SUB:j * HG_SUB + 1, :] + upd[:, j * LANES:(j + 1) * LANES] * head_mask
    yield
    q_exp = jnp.concatenate([jnp.where(blk == j, qd, 0.0).astype(BF16) for j in range(N_SUB)], axis=1)
    o = o + _dot_nt(q_exp, jnp.concatenate(before, axis=1))
    return o, st


def _round_robin(generators):
    results = [None] * len(generators)
    live = list(range(len(generators)))
    while live:
        for k in list(live):
            try:
                next(generators[k])
            except StopIteration as done:
                results[k] = done.value
                live.remove(k)
    return results


def _hgrn_kernel(qvf_ref, ff_ref, qvb_ref, fb_ref, tri_ref, ones_ref, of_ref, ob_ref, st_scr):
    @pl.when(pl.program_id(0) == 0)
    def _():
        st_scr[...] = jnp.zeros_like(st_scr)

    ones_blk = ones_ref[...]
    W = HG_WIDTH
    keys = [(b, direction, hp) for b in range(BATCH) for direction in range(2) for hp in range(N_PAIR)]
    for step in range(HG_STEP_CHUNKS):
        chunk = (step, HG_STEP_CHUNKS - 1 - step)
        chains = []
        for b, direction, hp in keys:
            qv_ref, f_ref = ((qvf_ref, ff_ref), (qvb_ref, fb_ref))[direction]
            ch = chunk[direction]
            c0 = hp * LANES
            chains.append(_hgrn_group(
                qv_ref[b, ch, :, c0:c0 + LANES], qv_ref[b, ch, :, W + c0:W + c0 + LANES],
                f_ref[b, ch, :, c0:c0 + LANES], f_ref[b, ch, :, W + c0:W + c0 + LANES],
                tri_ref[direction], ones_blk, st_scr[b, direction * N_PAIR + hp], direction))
        for (b, direction, hp), (o, st) in zip(keys, _round_robin(chains)):
            (of_ref, ob_ref)[direction][b, chunk[direction], :, hp * LANES:(hp + 1) * LANES] = o
            st_scr[b, direction * N_PAIR + hp] = st


def _hgrn_call(qv, ff, fb, tri, ones_blk):
    def chunked(a):
        return a.reshape(BATCH, N_CHUNK, HG_CHUNK, a.shape[-1])

    n_steps = N_CHUNK // HG_STEP_CHUNKS
    ctx_steps = N_CHUNK_CTX // HG_STEP_CHUNKS

    def fwd_idx(s):
        return jnp.where(s < ctx_steps, n_steps - ctx_steps + s, s - ctx_steps)

    def bwd_idx(s):
        return n_steps - 1 - s

    def spec(c, idx):
        return pl.BlockSpec((BATCH, HG_STEP_CHUNKS, HG_CHUNK, c), lambda s: (0, idx(s), 0, 0))

    out_sds = jax.ShapeDtypeStruct((BATCH, N_CHUNK, HG_CHUNK, HG_WIDTH), F32)
    o_f, o_b = pl.pallas_call(
        _hgrn_kernel,
        grid=(n_steps,),
        in_specs=[
            spec(2 * HG_WIDTH, fwd_idx), spec(2 * HG_WIDTH, fwd_idx),
            spec(2 * HG_WIDTH, bwd_idx), spec(2 * HG_WIDTH, bwd_idx),
            pl.BlockSpec((2, HG_CHUNK, HG_CHUNK), lambda s: (0, 0, 0)),
            pl.BlockSpec((LANES, LANES), lambda s: (0, 0)),
        ],
        out_specs=[spec(HG_WIDTH, fwd_idx), spec(HG_WIDTH, bwd_idx)],
        out_shape=[out_sds, out_sds],
        scratch_shapes=[pltpu.VMEM((BATCH, 2 * N_PAIR, LANES, LANES), F32)],
        compiler_params=_cparams(("arbitrary",)),
        name="hgrn2",
    )(chunked(qv), chunked(ff), chunked(qv), chunked(fb), tri, ones_blk)
    return o_f.reshape(BATCH, SEQ_ALL, HG_WIDTH), o_b.reshape(BATCH, SEQ_ALL, HG_WIDTH)


def _pair_queries(q):
    lane = lax.broadcasted_iota(jnp.int32, q.shape, 1)
    zero = jnp.zeros_like(q)
    return jnp.concatenate([jnp.where(lane < HEAD_DIM, q, zero), jnp.where(lane >= HEAD_DIM, q, zero)], axis=0)


def _pair_merge(o):
    m = o.shape[0] // 2
    lane = lax.broadcasted_iota(jnp.int32, (m, LANES), 1)
    return jnp.where(lane < HEAD_DIM, o[0:m], o[m:])


def _attend(q, keys, biases, values, extra=None):
    def lane_chunks(s):
        return [s[:, c:c + LANES] for c in range(0, s.shape[1], LANES)]

    q2 = _pair_queries(q)
    scores = []
    for k, b in zip(keys, biases):
        s = _dot_nt(q2, k)
        scores.append(s if b is None else s + b)
        yield
    m = functools.reduce(jnp.maximum, [c for s in scores for c in lane_chunks(s)])
    m = jnp.max(m, axis=-1, keepdims=True)
    if extra is not None:
        m = jnp.maximum(m, extra)
    yield
    acc = None
    for s, v in zip(scores, values):
        e = jnp.exp((s - m).astype(BF16))
        pv = _dot(e, jnp.concatenate([v, jnp.ones_like(v)], axis=1))
        acc = pv if acc is None else acc + pv
        yield
    denom = acc[:, LANES:]
    if extra is not None:
        denom = denom + jnp.exp(extra - m)
    return _pair_merge(acc[:, :LANES] / denom)


def _na_kernel(q_ref, k_ref, v_ref, kc_ref, vc_ref, bias_ref, o_ref):
    j = pl.program_id(1)

    @pl.when(j < ATT_STEPS)
    def _():
        units = []
        for rr in range(ROWS_PER_STEP):
            r = j * ROWS_PER_STEP + rr
            start = jnp.clip(r - NA_ROWS // 2, 0, GRID_W - NA_ROWS)
            k0 = pl.multiple_of(start * GRID_W, GRID_W)
            t0 = start - r + (NA_ROWS - 1)
            for p in range(NA_HEADS // 2):
                c = slice(p * LANES, (p + 1) * LANES)
                bias = jnp.concatenate(
                    [jnp.concatenate([bias_ref[h, t0 + i] for i in range(0, NA_ROWS, 2)], axis=1)
                     for h in (2 * p, 2 * p + 1)], axis=0)
                units.append(_attend(q_ref[rr * GRID_W:(rr + 1) * GRID_W, c],
                                     [k_ref[pl.ds(k0, NA_KEYS), c], kc_ref[:, c]], [bias, None],
                                     [v_ref[pl.ds(k0, NA_KEYS), c], vc_ref[:, c]]))
        outs = [o for k in range(0, len(units), ATT_IN_FLIGHT) for o in _round_robin(units[k:k + ATT_IN_FLIGHT])]
        for i, o in enumerate(outs):
            rr, p = divmod(i, NA_HEADS // 2)
            o_ref[rr * GRID_W:(rr + 1) * GRID_W, p * LANES:(p + 1) * LANES] = o.astype(BF16)

    @pl.when(j == ATT_STEPS)
    def _():
        units = [_attend(q_ref[0:CTX_LEN, p * LANES:(p + 1) * LANES], [kc_ref[:, p * LANES:(p + 1) * LANES]], [None],
                         [vc_ref[:, p * LANES:(p + 1) * LANES]]) for p in range(NA_HEADS // 2)]
        for p, o in enumerate(_round_robin(units)):
            o_ref[0:CTX_LEN, p * LANES:(p + 1) * LANES] = o.astype(BF16)


def _att_steps(need_ctx):
    return ATT_STEPS + 1 if need_ctx else ATT_STEPS


def _na_call(at, bias, *, layer, need_ctx):
    w = NA_WIDTH
    return pl.pallas_call(
        _na_kernel,
        grid=(BATCH, _att_steps(need_ctx)),
        in_specs=[
            pl.BlockSpec((None, TM_ATT, w), lambda b, j: (b, j, 0)),
            pl.BlockSpec((None, SEQ, w), lambda b, j: (b, 0, 1)),
            pl.BlockSpec((None, SEQ, w), lambda b, j: (b, 0, 2)),
            pl.BlockSpec((None, CTX_LEN, w), lambda b, j: (b, SEQ // CTX_LEN, 1)),
            pl.BlockSpec((None, CTX_LEN, w), lambda b, j: (b, SEQ // CTX_LEN, 2)),
            _resident((NA_HEADS, 2 * NA_ROWS - 2, GRID_W, 2 * GRID_W), (layer,)),
        ],
        out_specs=pl.BlockSpec((None, TM_ATT, w), lambda b, j: (b, j, 0)),
        out_shape=jax.ShapeDtypeStruct((BATCH, SEQ_ALL, w), BF16),
        compiler_params=_cparams(("parallel", "arbitrary")),
        name="nbr_attn",
    )(at, at, at, at, at, bias)


def _na_bias_table(rpb):
    col = np.arange(GRID_W)
    c0 = np.clip(col - NA_COLS // 2, 0, GRID_W - NA_COLS)
    col_ok = (col[None, :] >= c0[:, None]) & (col[None, :] < c0[:, None] + NA_COLS)
    d_col = np.clip(col[None, :] - col[:, None], 1 - NA_COLS, NA_COLS - 1)
    col_sel = (d_col[None, :, :] + NA_COLS - 1 == np.arange(2 * NA_COLS - 1)[:, None, None]).astype(np.float32)
    b = jnp.einsum('lhrd,dqk->lhrqk', rpb.astype(F32), col_sel, precision=lax.Precision.HIGHEST)
    b = jnp.where(col_ok, b, MASK_VALUE)
    return jnp.concatenate([b[:, :, :-1], b[:, :, 1:]], axis=-1)


def _sw_kernel(sink_ref, q_ref, k_ref, v_ref, kc_ref, vc_ref, o_ref):
    j = pl.program_id(1)
    n_pair = SW_HEADS // 2

    def sink_col(p, m):
        row = lax.broadcasted_iota(jnp.int32, (2 * m, 1), 0)
        return jnp.where(row < m, sink_ref[p], sink_ref[p + n_pair])

    def store_pair(rows, p, o):
        ob = o.astype(BF16)
        o_ref[rows, p * HEAD_DIM:(p + 1) * HEAD_DIM] = ob[:, :HEAD_DIM]
        o_ref[rows, (p + n_pair) * HEAD_DIM:(p + n_pair + 1) * HEAD_DIM] = ob[:, HEAD_DIM:]

    @pl.when(j < ATT_STEPS)
    def _():
        units = []
        for u in range(TM_ATT // SW_BLOCK):
            n = j * (TM_ATT // SW_BLOCK) + u
            start = jnp.clip(n * SW_BLOCK - SW_BLOCK, 0, SEQ - SW_KEYS)
            k0 = pl.multiple_of(start, SW_BLOCK)
            rel = (n * SW_BLOCK - start
                   + lax.broadcasted_iota(jnp.int32, (SW_BLOCK, SW_KEYS), 0)
                   - lax.broadcasted_iota(jnp.int32, (SW_BLOCK, SW_KEYS), 1))
            band = jnp.where(jnp.abs(rel) <= SW_WINDOW, 0.0, MASK_VALUE).astype(F32)
            band2 = jnp.concatenate([band, band], axis=0)
            kw = k_ref[pl.ds(k0, SW_KEYS), :]
            vw = v_ref[pl.ds(k0, SW_KEYS), :]
            for p in range(n_pair):
                units.append(_attend(q_ref[u * SW_BLOCK:(u + 1) * SW_BLOCK, p * LANES:(p + 1) * LANES],
                                     [kw, kc_ref[...]], [band2, None], [vw, vc_ref[...]], extra=sink_col(p, SW_BLOCK)))
        half = ATT_IN_FLIGHT // 2
        outs = [o for k in range(0, len(units), half) for o in _round_robin(units[k:k + half])]
        for i, o in enumerate(outs):
            u, p = divmod(i, n_pair)
            store_pair(slice(u * SW_BLOCK, (u + 1) * SW_BLOCK), p, o)

    @pl.when(j == ATT_STEPS)
    def _():
        units = [_attend(q_ref[0:CTX_LEN, p * LANES:(p + 1) * LANES], [kc_ref[...]], [None], [vc_ref[...]],
                         extra=sink_col(p, CTX_LEN)) for p in range(n_pair)]
        for p, o in enumerate(_round_robin(units)):
            store_pair(slice(0, CTX_LEN), p, o)


def _sw_call(at, sink_perm, need_ctx):
    q_blk = 3 * NA_WIDTH // SW_WIDTH
    k_blk = (3 * NA_WIDTH + SW_WIDTH) // SW_KV_WIDTH
    grid_spec = pltpu.PrefetchScalarGridSpec(
        num_scalar_prefetch=1,
        grid=(BATCH, _att_steps(need_ctx)),
        in_specs=[
            pl.BlockSpec((None, TM_ATT, SW_WIDTH), lambda b, j, s: (b, j, q_blk)),
            pl.BlockSpec((None, SEQ, SW_KV_WIDTH), lambda b, j, s: (b, 0, k_blk)),
            pl.BlockSpec((None, SEQ, SW_KV_WIDTH), lambda b, j, s: (b, 0, k_blk + 1)),
            pl.BlockSpec((None, CTX_LEN, SW_KV_WIDTH), lambda b, j, s: (b, SEQ // CTX_LEN, k_blk)),
            pl.BlockSpec((None, CTX_LEN, SW_KV_WIDTH), lambda b, j, s: (b, SEQ // CTX_LEN, k_blk + 1)),
        ],
        out_specs=pl.BlockSpec((None, TM_ATT, SW_WIDTH), lambda b, j, s: (b, j, 0)),
    )
    return pl.pallas_call(
        _sw_kernel,
        grid_spec=grid_spec,
        out_shape=jax.ShapeDtypeStruct((BATCH, SEQ_ALL, SW_WIDTH), BF16),
        compiler_params=_cparams(("parallel", "arbitrary")),
        name="win_attn",
    )(sink_perm, at, at, at, at, at)


N_MIX_IN = 15


def _rows(ref):
    v = ref[...]
    return v.reshape(-1, v.shape[-1])


def _mix_ffn_kernel(x_ref, mod_ref, of_ref, ob_ref, sg_ref, na_ref, sw_ref, w_ref, ng_ref, ones_ref,
                    g_ref, w1_ref, w3_ref, w2_ref, fg_ref, o_ref, *, final):
    o = _rows(of_ref) + _rows(ob_ref)
    ms = jnp.concatenate(
        [_dot((o[:, c:c + LANES] * o[:, c:c + LANES]).astype(BF16), ones_ref[...]) for c in range(0, HG_WIDTH, LANES)],
        axis=1) * (1.0 / HEAD_DIM)
    hg = (o * lax.rsqrt(ms + EPS)) * ng_ref[...] * _rows(sg_ref)
    y = _dot(hg.astype(BF16), w_ref[0:HG_WIDTH, :])
    y = y + _dot(_rows(na_ref), w_ref[HG_WIDTH:HG_WIDTH + NA_WIDTH, :])
    y = y + _dot(_rows(sw_ref), w_ref[HG_WIDTH + NA_WIDTH:, :])
    x = x_ref[...] + mod_ref[5:6, :] * y
    y = _swiglu_half_step(x, mod_ref, g_ref, w1_ref, w3_ref, w2_ref, mod0=6)
    if final:
        ms = jnp.mean(y * y, axis=-1, keepdims=True)
        y = (y * lax.rsqrt(ms + EPS)) * fg_ref[...]
    o_ref[...] = y


def _mix_ffn_ctx_kernel(*refs):
    _mix_ffn_kernel(*refs[:N_MIX_IN], refs[-1], final=False)


def _mix_ffn_call(x, mods_all, o_f, o_b, sg, o_na, o_sw, w_out, ng, ones_blk, norm_g, w1, w3, w2, final_g,
                  *, layer, need_ctx):
    n_rows = N_TOK if need_ctx else N_LAT
    mixer_widths = (HG_WIDTH, HG_WIDTH, HG_WIDTH, NA_WIDTH, SW_WIDTH)
    params = [_resident((D_MODEL, D_MODEL), (layer,)), _resident((1, HG_WIDTH), (layer,)), _resident((LANES, LANES)),
              _resident((1, D_MODEL), (layer, 2)), _resident((D_MODEL, D_FF), (layer, 1)),
              _resident((D_MODEL, D_FF), (layer, 1)), _resident((D_FF, D_MODEL), (layer, 1)), _resident((1, D_MODEL))]
    operands = (x, mods_all, o_f, o_b, sg, o_na, o_sw, w_out, ng, ones_blk, norm_g, w1, w3, w2, final_g)
    assert len(operands) == N_MIX_IN
    out_shape = jax.ShapeDtypeStruct((n_rows, D_MODEL), F32)
    lat_tiles = SEQ // TM_FFN

    latent = pl.pallas_call(
        functools.partial(_mix_ffn_kernel, final=not need_ctx),
        grid=(N_LAT // TM_FFN,),
        in_specs=[
            pl.BlockSpec((TM_FFN, D_MODEL), lambda i: (i, 0)),
            pl.BlockSpec((None, None, N_MOD, D_MODEL), lambda i: (layer, i // lat_tiles, 0, 0)),
            *[pl.BlockSpec((None, TM_FFN, c), lambda i: (i // lat_tiles, i % lat_tiles, 0)) for c in mixer_widths],
            *params,
        ],
        out_specs=pl.BlockSpec((TM_FFN, D_MODEL), lambda i: (i, 0)),
        out_shape=out_shape,
        compiler_params=_cparams(("parallel",)),
        name="mix_ffn",
    )(*operands)
    if not need_ctx:
        return latent

    per_tile = TM_FFN // CTX_LEN
    ctx_pos = SEQ // CTX_LEN
    return pl.pallas_call(
        _mix_ffn_ctx_kernel,
        grid=(N_CTX // TM_FFN,),
        in_specs=[
            pl.BlockSpec((TM_FFN, D_MODEL), lambda k: (N_LAT // TM_FFN + k, 0)),
            pl.BlockSpec((None, None, N_MOD, D_MODEL), lambda k: (layer, BATCH, 0, 0)),
            *[pl.BlockSpec((per_tile, CTX_LEN, c), lambda k: (k, ctx_pos, 0)) for c in mixer_widths],
            *params,
            pl.BlockSpec(memory_space=pl.ANY),
        ],
        out_specs=pl.BlockSpec((TM_FFN, D_MODEL), lambda k: (N_LAT // TM_FFN + k, 0)),
        out_shape=out_shape,
        input_output_aliases={N_MIX_IN: 0},
        compiler_params=_cparams(("parallel",)),
        name="mix_ffn_ctx",
    )(*operands, latent)


def _rope_tables():
    pos = jnp.arange(SEQ)
    pos = jnp.stack([pos // GRID_W, pos % GRID_W], axis=-1).astype(F32)
    nf = HEAD_DIM // 4
    inv = ROPE_THETA ** (-jnp.arange(nf, dtype=F32) / nf)
    ang = pos[:, :, None] * inv
    cos, sin = jnp.cos(ang), jnp.sin(ang)
    cos_h = jnp.stack([cos, cos], axis=2).reshape(SEQ, HEAD_DIM)
    sin_h = jnp.stack([-sin, sin], axis=2).reshape(SEQ, HEAD_DIM)
    reps = LANES // HEAD_DIM
    cos_t = jnp.concatenate([jnp.tile(cos_h, (1, reps)), jnp.ones((CTX_LEN, LANES), F32)], axis=0)
    sin_t = jnp.concatenate([jnp.tile(sin_h, (1, reps)), jnp.zeros((CTX_LEN, LANES), F32)], axis=0)
    return cos_t, sin_t


def kernel(x, c, ctx, c_ctx, ada_w, ada_b, norm_g, ffn_w1, ffn_w3, ffn_w2, w_in, w_out,
           hg_lb_logits, hg_norm_g, na_rpb, sw_sink, final_g):
    lb_soft = jax.nn.softmax(hg_lb_logits.astype(F32), axis=0)
    lower_bounds = jnp.cumsum(lb_soft, axis=0) - lb_soft[0]
    w_in_p = w_in.astype(BF16)
    w_out_p = w_out.astype(BF16)
    sink_p = sw_sink.astype(F32)
    w1 = ffn_w1.astype(BF16)
    w3 = ffn_w3.astype(BF16)
    w2 = ffn_w2.astype(BF16)
    cos_t, sin_t = _rope_tables()
    na_bias = _na_bias_table(na_rpb)
    tri, ones_blk = _hgrn_consts()
    final_g2 = final_g.reshape(1, D_MODEL)
    norm_g4 = norm_g.reshape(DEPTH, 3, 1, D_MODEL)
    hg_norm_g3 = hg_norm_g.reshape(DEPTH, 1, HG_WIDTH)

    cond = jnp.concatenate([c, c_ctx[None, :], jnp.zeros((MOD_ROWS - BATCH - 1, D_MODEL), F32)], axis=0)
    mods_all = _ada_call(cond, ada_w, ada_b).reshape(DEPTH, MOD_ROWS, N_MOD, D_MODEL)

    xs = None
    for l in range(DEPTH):
        need_ctx = l < DEPTH - 1
        if l == 0:
            xs = _ffn_call(x.reshape(N_LAT, D_MODEL), ctx.reshape(N_CTX, D_MODEL), 0,
                           mods_all, norm_g4, w1, w3, w2, layer=l)
        else:
            xs = _ffn_call(xs, xs, N_LAT // TM_FFN, mods_all, norm_g4, w1, w3, w2, layer=l)
        qv, ff, fb, sg, at = _inproj_call(xs, mods_all, norm_g4, w_in_p, lower_bounds, cos_t, sin_t, layer=l)
        o_f, o_b = _hgrn_call(qv, ff, fb, tri, ones_blk)
        o_na = _na_call(at, na_bias, layer=l, need_ctx=need_ctx)
        o_sw = _sw_call(at, sink_p[l], need_ctx)
        xs = _mix_ffn_call(xs, mods_all, o_f, o_b, sg, o_na, o_sw, w_out_p, hg_norm_g3, ones_blk,
                           norm_g4, w1, w3, w2, final_g2, layer=l, need_ctx=need_ctx)
    return xs.reshape(BATCH, SEQ, D_MODEL)
```

```python
import functools

import jax
import jax.numpy as jnp
import numpy as np
from jax import lax
from jax.experimental import pallas as pl
from jax.experimental.pallas import tpu as pltpu

F32 = jnp.float32
BF16 = jnp.bfloat16

D_MODEL = 1024
BATCH = 4
SEQ = 4096
DEPTH = 4
GRID_W = 64
CTX_LEN = 256
HEAD_DIM = 64
EPS = 1e-6
MASK_VALUE = -1e30
LOG2_E = 1.4426950408889634
ROPE_THETA = 10000.0
N_MOD = 9
D_FF = 2816
HG_WIDTH = 256
HG_HEADS = 4
NA_WIDTH = 384
NA_HEADS = 6
NA_ROWS = 8
NA_COLS = 16
SW_WIDTH = 384
SW_HEADS = 6
SW_KV_WIDTH = 128
SW_WINDOW = 128
SW_BLOCK = 128
IN_WIDTH = 3072

LANES = 128
SUBLANES = 8
VMEM_LIMIT_BYTES = 56 * 1024 * 1024

N_LAT = BATCH * SEQ
N_CTX = BATCH * CTX_LEN
N_TOK = N_LAT + N_CTX
SEQ_ALL = SEQ + CTX_LEN
TM_PROJ = 256
TM_FFN = 512
FFN_SLAB = 128
TM_LAT = 1024
INPROJ_SLAB = 128
TM_ATT = 1024
ROWS_PER_STEP = TM_ATT // GRID_W
ATT_STEPS = SEQ // TM_ATT
ATT_IN_FLIGHT = 12
HG_CHUNK = 128
HG_STEP_CHUNKS = 2
HG_SUB = 16
N_CHUNK_LAT = SEQ // HG_CHUNK
N_CHUNK_CTX = CTX_LEN // HG_CHUNK
N_CHUNK = N_CHUNK_LAT + N_CHUNK_CTX
NA_KEYS = NA_ROWS * GRID_W
SW_KEYS = 3 * SW_BLOCK


def _cparams(sem):
    return pltpu.CompilerParams(dimension_semantics=sem, vmem_limit_bytes=VMEM_LIMIT_BYTES)


def _silu(a):
    return a * jax.nn.sigmoid(a)


def _dot(a, b):
    return jnp.dot(a, b, preferred_element_type=F32)


def _dot_nt(a, b):
    return lax.dot_general(a, b, (((1,), (1,)), ((), ())), preferred_element_type=F32)


def _modulated_norm(x, g, shift, scale):
    ms = jnp.mean(x * x, axis=-1, keepdims=True)
    return (x * lax.rsqrt(ms + EPS)) * g * (1.0 + scale) + shift


ADA_TN = 1536
MOD_ROWS = SUBLANES


def _ada_kernel(c_ref, w_ref, b_ref, o_ref):
    s = _silu(c_ref[...]).astype(BF16)
    o_ref[...] = _dot(s, w_ref[...].astype(BF16)) + b_ref[...]


def _ada_call(cond, ada_w, ada_b):
    n_out = N_MOD * D_MODEL
    return pl.pallas_call(
        _ada_kernel,
        grid=(DEPTH, n_out // ADA_TN),
        in_specs=[
            pl.BlockSpec((MOD_ROWS, D_MODEL), lambda l, j: (0, 0)),
            pl.BlockSpec((None, D_MODEL, ADA_TN), lambda l, j: (l, 0, j)),
            pl.BlockSpec((None, 1, ADA_TN), lambda l, j: (l, 0, j)),
        ],
        out_specs=pl.BlockSpec((None, MOD_ROWS, ADA_TN), lambda l, j: (l, 0, j)),
        out_shape=jax.ShapeDtypeStruct((DEPTH, MOD_ROWS, n_out), F32),
        compiler_params=_cparams(("parallel", "parallel")),
        name="ada_mod",
    )(cond, ada_w, ada_b.reshape(DEPTH, 1, n_out))


def _swiglu_half_step(slab_input, n_rows, mod_ref, g_ref, w1_ref, w3_ref, w2_ref, *, mod0, slab_output=None):
    def slab(r0):
        xs = slab_input(r0)
        yield
        h = _modulated_norm(xs, g_ref[...], mod_ref[mod0:mod0 + 1, :], mod_ref[mod0 + 1:mod0 + 2, :]).astype(BF16)
        yield
        a1 = _dot(h, w1_ref[...])
        a3 = _dot(h, w3_ref[...])
        yield
        a = (_silu(a1) * a3).astype(BF16)
        yield
        y = xs + (0.5 * mod_ref[mod0 + 2:mod0 + 3, :]) * _dot(a, w2_ref[...])
        return y if slab_output is None else slab_output(y)

    return jnp.concatenate(_round_robin([slab(r0) for r0 in range(0, n_rows, FFN_SLAB)]), axis=0)


def _ffn_kernel(xl_ref, xc_ref, mod_ref, g_ref, w1_ref, w3_ref, w2_ref, o_ref):
    x = jnp.where(pl.program_id(0) < N_LAT // TM_FFN, xl_ref[...], xc_ref[...])
    o_ref[...] = _swiglu_half_step(lambda r0: x[r0:r0 + FFN_SLAB], TM_FFN, mod_ref, g_ref, w1_ref, w3_ref, w2_ref,
                                   mod0=0)


def _resident(shape, lead=()):
    return pl.BlockSpec((None,) * len(lead) + tuple(shape), lambda *_: tuple(lead) + (0,) * len(shape),
                        pipeline_mode=pl.Buffered(1))


def _ffn_call(x_lat, x_ctx, ctx_tile0, mods_all, norm_g, w1, w3, w2, *, layer):
    tiles_per_batch = SEQ // TM_FFN
    lat_tiles = N_LAT // TM_FFN

    def mod_idx(i):
        return (layer, jnp.where(i < BATCH * tiles_per_batch, i // tiles_per_batch, BATCH), 0, 0)

    return pl.pallas_call(
        _ffn_kernel,
        grid=(N_TOK // TM_FFN,),
        in_specs=[
            pl.BlockSpec((TM_FFN, D_MODEL), lambda i: (jnp.minimum(i, lat_tiles - 1), 0)),
            pl.BlockSpec((TM_FFN, D_MODEL), lambda i: (ctx_tile0 + jnp.maximum(i - lat_tiles, 0), 0)),
            pl.BlockSpec((None, None, N_MOD, D_MODEL), mod_idx),
            _resident((1, D_MODEL), (layer, 0)),
            _resident((D_MODEL, D_FF), (layer, 0)),
            _resident((D_MODEL, D_FF), (layer, 0)),
            _resident((D_FF, D_MODEL), (layer, 0)),
        ],
        out_specs=pl.BlockSpec((TM_FFN, D_MODEL), lambda i: (i, 0)),
        out_shape=jax.ShapeDtypeStruct((N_TOK, D_MODEL), F32),
        compiler_params=_cparams(("parallel",)),
        name="ffn",
    )(x_lat, x_ctx, mods_all, norm_g, w1, w3, w2)


def _rope(z, cos, sin_signed, first_of_pair):
    partner = jnp.where(first_of_pair, pltpu.roll(z, LANES - 16, 1), pltpu.roll(z, 16, 1))
    return z * cos + partner * sin_signed


def _inproj_kernel(x_ref, *refs):
    n = min(INPROJ_SLAB, x_ref.shape[0])
    _round_robin([_inproj_rows(pl.ds(r0, n), x_ref, *refs) for r0 in range(0, x_ref.shape[0], n)])


def _inproj_rows(rows, x_ref, mod_ref, g_ref, w_ref, lb_ref, cos_ref, sin_ref,
                 qv_ref, ff_ref, fb_ref, sg_ref, at_ref):
    h = _modulated_norm(x_ref[rows, :], g_ref[...], mod_ref[3:4, :], mod_ref[4:5, :]).astype(BF16)
    W = HG_WIDTH
    p_all = _dot(h, w_ref[...])
    yield

    def proj(lo, hi):
        return p_all[:, lo:hi]

    qv_ref[rows, 0:W] = _silu(proj(0, W))
    for d, dst in ((0, ff_ref), (1, fb_ref)):
        z = proj((1 + d) * W, (2 + d) * W)
        lb = lb_ref[d:d + 1, :]
        dst[rows, 0:W] = jnp.log(lb + (1.0 - lb) * jax.nn.sigmoid(z)) * LOG2_E
        dst[rows, W:2 * W] = jnp.log((1.0 - lb) * jax.nn.sigmoid(-z)) * LOG2_E
    qv_ref[rows, W:2 * W] = proj(3 * W, 4 * W)
    sg_ref[rows, :] = _silu(proj(4 * W, 5 * W))
    yield

    scale = HEAD_DIM ** -0.5
    na0 = 5 * W
    at_ref[rows, 0:NA_WIDTH] = (proj(na0, na0 + NA_WIDTH) * scale).astype(BF16)
    at_ref[rows, NA_WIDTH:3 * NA_WIDTH] = proj(na0 + NA_WIDTH, na0 + 3 * NA_WIDTH).astype(BF16)
    yield
    sw0 = na0 + 3 * NA_WIDTH
    cos = cos_ref[rows, :]
    sin = sin_ref[rows, :]
    lane = lax.broadcasted_iota(jnp.int32, (rows.size, LANES), 1)
    first = (lane % 32) < 16
    z = [_rope(proj(sw0 + j * LANES, sw0 + (j + 1) * LANES), cos, sin, first)
         for j in range((SW_WIDTH + SW_KV_WIDTH) // LANES)]
    low = lane < HEAD_DIM
    z[0], z[1], z[2] = (jnp.where(low, z[0], z[1]), pltpu.roll(jnp.where(low, z[2], z[0]), HEAD_DIM, 1),
                        jnp.where(low, z[1], z[2]))
    for j in range(len(z)):
        zj = z[j] * scale if j < SW_WIDTH // LANES else z[j]
        at_ref[rows, 3 * NA_WIDTH + j * LANES:3 * NA_WIDTH + (j + 1) * LANES] = zj.astype(BF16)
    v0 = sw0 + SW_WIDTH + SW_KV_WIDTH
    at_ref[rows, 3 * NA_WIDTH + SW_WIDTH + SW_KV_WIDTH:] = proj(v0, v0 + SW_KV_WIDTH).astype(BF16)


AT_WIDTH = 3 * NA_WIDTH + SW_WIDTH + 2 * SW_KV_WIDTH


N_INPROJ_IN = 7
INPROJ_WIDTHS = (2 * HG_WIDTH, 2 * HG_WIDTH, 2 * HG_WIDTH, HG_WIDTH, AT_WIDTH)
INPROJ_DTYPES = (F32, F32, F32, F32, BF16)


def _inproj_ctx_kernel(*refs):
    _inproj_kernel(*refs[:N_INPROJ_IN], *refs[N_INPROJ_IN + len(INPROJ_WIDTHS):])


def _inproj_call(x, mods_all, norm_g, w_in, lb, cos_t, sin_t, *, layer):
    widths = INPROJ_WIDTHS
    out_shape = [jax.ShapeDtypeStruct((BATCH, SEQ_ALL, c), dt) for c, dt in zip(widths, INPROJ_DTYPES)]
    params = [_resident((1, D_MODEL), (layer, 1)), _resident((D_MODEL, IN_WIDTH), (layer,)),
              _resident((2, HG_WIDTH), (layer,))]
    lat_tiles = SEQ // TM_LAT

    latent = pl.pallas_call(
        _inproj_kernel,
        grid=(N_LAT // TM_LAT,),
        in_specs=[
            pl.BlockSpec((TM_LAT, D_MODEL), lambda i: (i, 0)),
            pl.BlockSpec((None, None, N_MOD, D_MODEL), lambda i: (layer, i // lat_tiles, 0, 0)),
            *params,
            pl.BlockSpec((TM_LAT, LANES), lambda i: (i % lat_tiles, 0)),
            pl.BlockSpec((TM_LAT, LANES), lambda i: (i % lat_tiles, 0)),
        ],
        out_specs=[pl.BlockSpec((None, TM_LAT, c), lambda i: (i // lat_tiles, i % lat_tiles, 0)) for c in widths],
        out_shape=out_shape,
        compiler_params=_cparams(("parallel",)),
        name="in_proj",
    )(x, mods_all, norm_g, w_in, lb, cos_t, sin_t)

    ctx_pos = SEQ // TM_PROJ
    return pl.pallas_call(
        _inproj_ctx_kernel,
        grid=(BATCH,),
        in_specs=[
            pl.BlockSpec((TM_PROJ, D_MODEL), lambda b: (N_LAT // TM_PROJ + b, 0)),
            pl.BlockSpec((None, None, N_MOD, D_MODEL), lambda b: (layer, BATCH, 0, 0)),
            *params,
            pl.BlockSpec((TM_PROJ, LANES), lambda b: (ctx_pos, 0)),
            pl.BlockSpec((TM_PROJ, LANES), lambda b: (ctx_pos, 0)),
            *[pl.BlockSpec(memory_space=pl.ANY) for _ in widths],
        ],
        out_specs=[pl.BlockSpec((None, TM_PROJ, c), lambda b: (b, ctx_pos, 0)) for c in widths],
        out_shape=out_shape,
        input_output_aliases={N_INPROJ_IN + k: k for k in range(len(widths))},
        compiler_params=_cparams(("parallel",)),
        name="in_proj_ctx",
    )(x, mods_all, norm_g, w_in, lb, cos_t, sin_t, *latent)


N_SUB = HG_CHUNK // HG_SUB
N_PAIR = HG_HEADS // 2
HG_MINI = HG_SUB // 2


def _hgrn_consts():
    t = np.arange(HG_CHUNK)
    same = (t[:, None] // HG_SUB) == (t[None, :] // HG_SUB)
    lower = same & (t[None, :] <= t[:, None])
    upper = same & (t[None, :] >= t[:, None])
    tri = np.stack([lower, upper]).astype(np.float32)
    d = np.arange(LANES)
    head_blocks = ((d[:, None] // HEAD_DIM) == (d[None, :] // HEAD_DIM)).astype(np.float32)
    return jnp.asarray(tri, BF16), jnp.asarray(head_blocks, BF16)


def _split2(a):
    hi = a.astype(BF16)
    lo = (a - hi.astype(F32)).astype(BF16)
    return jnp.concatenate([hi, lo], axis=1)


def _bcast_rows(a, s, block):
    n = HG_CHUNK // block
    a3 = a.reshape(n, block, LANES)
    return jnp.broadcast_to(a3[:, s:s + 1, :], (n, block, LANES)).reshape(HG_CHUNK, LANES)


def _hgrn_group(qs, v, gl, lk, tri, ones_blk, st, direction):
    fwd = direction == 0
    c = _dot(tri, _split2(gl))
    yield
    cum = c[:, 0:LANES] + c[:, LANES:]
    tot = _bcast_rows(cum, HG_SUB - 1 if fwd else 0, HG_SUB)
    row = lax.broadcasted_iota(jnp.int32, (HG_CHUNK, LANES), 0)
    t_mini = row % HG_MINI
    blk = row // HG_SUB
    v16 = v.astype(BF16)
    ck = cum - lk

    o = jnp.zeros((HG_CHUNK, LANES), F32)
    ones2 = jnp.concatenate([jnp.concatenate([ones_blk, jnp.zeros_like(ones_blk)], axis=1),
                             jnp.concatenate([jnp.zeros_like(ones_blk), ones_blk], axis=1)], axis=0)
    for s0 in range(0, HG_MINI, 2):
        w = []
        for s in (s0, s0 + 1):
            keep = (t_mini >= s) if fwd else (t_mini <= s)
            w.append(jnp.where(keep, qs * jnp.exp2(cum - _bcast_rows(ck, s, HG_MINI)), 0.0).astype(BF16))
        r = _dot(jnp.concatenate(w, axis=1), ones2)
        o = o + r[:, :LANES] * _bcast_rows(v, s0, HG_MINI) + r[:, LANES:] * _bcast_rows(v, s0 + 1, HG_MINI)
        yield

    later = ((row % HG_SUB) >= HG_MINI) if fwd else ((row % HG_SUB) < HG_MINI)
    edge = _bcast_rows(cum, HG_MINI - 1 if fwd else HG_MINI, HG_SUB)
    q_edge = jnp.where(later, qs * jnp.exp2(jnp.minimum(cum - edge, 0.0)), 0.0)
    k_edge = jnp.where(later, 0.0, jnp.exp2(jnp.minimum(edge - ck, 0.0)))
    a = _dot_nt(_pair_queries(q_edge.astype(BF16)), k_edge.astype(BF16))
    q_blk = lax.broadcasted_iota(jnp.int32, (2 * HG_CHUNK, HG_CHUNK), 0) % HG_CHUNK // HG_SUB
    k_blk = lax.broadcasted_iota(jnp.int32, (2 * HG_CHUNK, HG_CHUNK), 1) // HG_SUB
    yield
    a = jnp.where(q_blk == k_blk, a, 0.0)
    o = o + _pair_merge(_dot(a.astype(BF16), v16))
    yield

    qd = qs * jnp.exp2(cum)
    kd = jnp.exp2(tot - ck)
    dec = jnp.exp2(tot)
    k_exp = jnp.concatenate([jnp.where(blk == j, kd, 0.0).astype(BF16) for j in range(N_SUB)], axis=1)
    upd = _dot(v.T.astype(BF16), k_exp)
    yield
    head_mask = ones_blk.astype(F32)
    before = [None] * N_SUB
    for j in (range(N_SUB) if fwd else range(N_SUB - 1, -1, -1)):
        before[j] = st.astype(BF16)
        st = st * dec[j * HG_SUB:j * HG_SUB + 1, :] + upd[:, j * LANES:(j + 1) * LANES] * head_mask
    yield
    q_exp = jnp.concatenate([jnp.where(blk == j, qd, 0.0).astype(BF16) for j in range(N_SUB)], axis=1)
    o = o + _dot_nt(q_exp, jnp.concatenate(before, axis=1))
    return o, st


def _round_robin(generators):
    results = [None] * len(generators)
    live = list(range(len(generators)))
    while live:
        for k in list(live):
            try:
                next(generators[k])
            except StopIteration as done:
                results[k] = done.value
                live.remove(k)
    return results


def _hgrn_kernel(qvf_ref, ff_ref, qvb_ref, fb_ref, tri_ref, ones_ref, of_ref, ob_ref, st_scr):
    @pl.when(pl.program_id(0) == 0)
    def _():
        st_scr[...] = jnp.zeros_like(st_scr)

    ones_blk = ones_ref[...]
    W = HG_WIDTH
    keys = [(b, direction, hp) for b in range(BATCH) for direction in range(2) for hp in range(N_PAIR)]
    for step in range(HG_STEP_CHUNKS):
        chunk = (step, HG_STEP_CHUNKS - 1 - step)
        chains = []
        for b, direction, hp in keys:
            qv_ref, f_ref = ((qvf_ref, ff_ref), (qvb_ref, fb_ref))[direction]
            ch = chunk[direction]
            c0 = hp * LANES
            chains.append(_hgrn_group(
                qv_ref[b, ch, :, c0:c0 + LANES], qv_ref[b, ch, :, W + c0:W + c0 + LANES],
                f_ref[b, ch, :, c0:c0 + LANES], f_ref[b, ch, :, W + c0:W + c0 + LANES],
                tri_ref[direction], ones_blk, st_scr[b, direction * N_PAIR + hp], direction))
        for (b, direction, hp), (o, st) in zip(keys, _round_robin(chains)):
            (of_ref, ob_ref)[direction][b, chunk[direction], :, hp * LANES:(hp + 1) * LANES] = o
            st_scr[b, direction * N_PAIR + hp] = st


def _hgrn_call(qv, ff, fb, tri, ones_blk):
    def chunked(a):
        return a.reshape(BATCH, N_CHUNK, HG_CHUNK, a.shape[-1])

    n_steps = N_CHUNK // HG_STEP_CHUNKS
    ctx_steps = N_CHUNK_CTX // HG_STEP_CHUNKS

    def fwd_idx(s):
        return jnp.where(s < ctx_steps, n_steps - ctx_steps + s, s - ctx_steps)

    def bwd_idx(s):
        return n_steps - 1 - s

    def spec(c, idx):
        return pl.BlockSpec((BATCH, HG_STEP_CHUNKS, HG_CHUNK, c), lambda s: (0, idx(s), 0, 0))

    out_sds = jax.ShapeDtypeStruct((BATCH, N_CHUNK, HG_CHUNK, HG_WIDTH), F32)
    o_f, o_b = pl.pallas_call(
        _hgrn_kernel,
        grid=(n_steps,),
        in_specs=[
            spec(2 * HG_WIDTH, fwd_idx), spec(2 * HG_WIDTH, fwd_idx),
            spec(2 * HG_WIDTH, bwd_idx), spec(2 * HG_WIDTH, bwd_idx),
            pl.BlockSpec((2, HG_CHUNK, HG_CHUNK), lambda s: (0, 0, 0)),
            pl.BlockSpec((LANES, LANES), lambda s: (0, 0)),
        ],
        out_specs=[spec(HG_WIDTH, fwd_idx), spec(HG_WIDTH, bwd_idx)],
        out_shape=[out_sds, out_sds],
        scratch_shapes=[pltpu.VMEM((BATCH, 2 * N_PAIR, LANES, LANES), F32)],
        compiler_params=_cparams(("arbitrary",)),
        name="hgrn2",
    )(chunked(qv), chunked(ff), chunked(qv), chunked(fb), tri, ones_blk)
    return o_f.reshape(BATCH, SEQ_ALL, HG_WIDTH), o_b.reshape(BATCH, SEQ_ALL, HG_WIDTH)


def _pair_queries(q):
    lane = lax.broadcasted_iota(jnp.int32, q.shape, 1)
    zero = jnp.zeros_like(q)
    return jnp.concatenate([jnp.where(lane < HEAD_DIM, q, zero), jnp.where(lane >= HEAD_DIM, q, zero)], axis=0)


def _pair_merge(o):
    m = o.shape[0] // 2
    lane = lax.broadcasted_iota(jnp.int32, (m, LANES), 1)
    return jnp.where(lane < HEAD_DIM, o[0:m], o[m:])


def _attend(q, keys, biases, values, extra=None):
    def lane_chunks(s):
        return [s[:, c:c + LANES] for c in range(0, s.shape[1], LANES)]

    q2 = _pair_queries(q)
    scores = []
    for k, b in zip(keys, biases):
        s = _dot_nt(q2, k)
        scores.append(s if b is None else s + b)
        yield
    m = functools.reduce(jnp.maximum, [c for s in scores for c in lane_chunks(s)])
    m = jnp.max(m, axis=-1, keepdims=True)
    if extra is not None:
        m = jnp.maximum(m, extra)
    yield
    acc = None
    for s, v in zip(scores, values):
        e = jnp.exp((s - m).astype(BF16))
        pv = _dot(e, jnp.concatenate([v, jnp.ones_like(v)], axis=1))
        acc = pv if acc is None else acc + pv
        yield
    denom = acc[:, LANES:]
    if extra is not None:
        denom = denom + jnp.exp(extra - m)
    return _pair_merge(acc[:, :LANES] / denom)


def _na_kernel(q_ref, k_ref, v_ref, kc_ref, vc_ref, bias_ref, o_ref):
    j = pl.program_id(1)

    @pl.when(j < ATT_STEPS)
    def _():
        units = []
        for rr in range(ROWS_PER_STEP):
            r = j * ROWS_PER_STEP + rr
            start = jnp.clip(r - NA_ROWS // 2, 0, GRID_W - NA_ROWS)
            k0 = pl.multiple_of(start * GRID_W, GRID_W)
            t0 = start - r + (NA_ROWS - 1)
            for p in range(NA_HEADS // 2):
                c = slice(p * LANES, (p + 1) * LANES)
                bias = jnp.concatenate(
                    [jnp.concatenate([bias_ref[h, t0 + i] for i in range(0, NA_ROWS, 2)], axis=1)
                     for h in (2 * p, 2 * p + 1)], axis=0)
                units.append(_attend(q_ref[rr * GRID_W:(rr + 1) * GRID_W, c],
                                     [k_ref[pl.ds(k0, NA_KEYS), c], kc_ref[:, c]], [bias, None],
                                     [v_ref[pl.ds(k0, NA_KEYS), c], vc_ref[:, c]]))
        outs = [o for k in range(0, len(units), ATT_IN_FLIGHT) for o in _round_robin(units[k:k + ATT_IN_FLIGHT])]
        for i, o in enumerate(outs):
            rr, p = divmod(i, NA_HEADS // 2)
            o_ref[rr * GRID_W:(rr + 1) * GRID_W, p * LANES:(p + 1) * LANES] = o.astype(BF16)

    @pl.when(j == ATT_STEPS)
    def _():
        units = [_attend(q_ref[0:CTX_LEN, p * LANES:(p + 1) * LANES], [kc_ref[:, p * LANES:(p + 1) * LANES]], [None],
                         [vc_ref[:, p * LANES:(p + 1) * LANES]]) for p in range(NA_HEADS // 2)]
        for p, o in enumerate(_round_robin(units)):
            o_ref[0:CTX_LEN, p * LANES:(p + 1) * LANES] = o.astype(BF16)


def _att_steps(need_ctx):
    return ATT_STEPS + 1 if need_ctx else ATT_STEPS


def _na_call(at, bias, *, layer, need_ctx):
    w = NA_WIDTH
    return pl.pallas_call(
        _na_kernel,
        grid=(BATCH, _att_steps(need_ctx)),
        in_specs=[
            pl.BlockSpec((None, TM_ATT, w), lambda b, j: (b, j, 0)),
            pl.BlockSpec((None, SEQ, w), lambda b, j: (b, 0, 1)),
            pl.BlockSpec((None, SEQ, w), lambda b, j: (b, 0, 2)),
            pl.BlockSpec((None, CTX_LEN, w), lambda b, j: (b, SEQ // CTX_LEN, 1)),
            pl.BlockSpec((None, CTX_LEN, w), lambda b, j: (b, SEQ // CTX_LEN, 2)),
            _resident((NA_HEADS, 2 * NA_ROWS - 2, GRID_W, 2 * GRID_W), (layer,)),
        ],
        out_specs=pl.BlockSpec((None, TM_ATT, w), lambda b, j: (b, j, 0)),
        out_shape=jax.ShapeDtypeStruct((BATCH, SEQ_ALL, w), BF16),
        compiler_params=_cparams(("parallel", "arbitrary")),
        name="nbr_attn",
    )(at, at, at, at, at, bias)


def _na_bias_table(rpb):
    col = np.arange(GRID_W)
    c0 = np.clip(col - NA_COLS // 2, 0, GRID_W - NA_COLS)
    col_ok = (col[None, :] >= c0[:, None]) & (col[None, :] < c0[:, None] + NA_COLS)
    d_col = np.clip(col[None, :] - col[:, None], 1 - NA_COLS, NA_COLS - 1)
    col_sel = (d_col[None, :, :] + NA_COLS - 1 == np.arange(2 * NA_COLS - 1)[:, None, None]).astype(np.float32)
    b = jnp.einsum('lhrd,dqk->lhrqk', rpb.astype(F32), col_sel, precision=lax.Precision.HIGHEST)
    b = jnp.where(col_ok, b, MASK_VALUE)
    return jnp.concatenate([b[:, :, :-1], b[:, :, 1:]], axis=-1)


def _sw_kernel(sink_ref, q_ref, k_ref, v_ref, kc_ref, vc_ref, o_ref):
    j = pl.program_id(1)
    n_pair = SW_HEADS // 2

    def sink_col(p, m):
        row = lax.broadcasted_iota(jnp.int32, (2 * m, 1), 0)
        return jnp.where(row < m, sink_ref[p], sink_ref[p + n_pair])

    def store_pair(rows, p, o):
        ob = o.astype(BF16)
        o_ref[rows, p * HEAD_DIM:(p + 1) * HEAD_DIM] = ob[:, :HEAD_DIM]
        o_ref[rows, (p + n_pair) * HEAD_DIM:(p + n_pair + 1) * HEAD_DIM] = ob[:, HEAD_DIM:]

    @pl.when(j < ATT_STEPS)
    def _():
        units = []
        for u in range(TM_ATT // SW_BLOCK):
            n = j * (TM_ATT // SW_BLOCK) + u
            start = jnp.clip(n * SW_BLOCK - SW_BLOCK, 0, SEQ - SW_KEYS)
            k0 = pl.multiple_of(start, SW_BLOCK)
            rel = (n * SW_BLOCK - start
                   + lax.broadcasted_iota(jnp.int32, (SW_BLOCK, SW_KEYS), 0)
                   - lax.broadcasted_iota(jnp.int32, (SW_BLOCK, SW_KEYS), 1))
            band = jnp.where(jnp.abs(rel) <= SW_WINDOW, 0.0, MASK_VALUE).astype(F32)
            band2 = jnp.concatenate([band, band], axis=0)
            kw = k_ref[pl.ds(k0, SW_KEYS), :]
            vw = v_ref[pl.ds(k0, SW_KEYS), :]
            for p in range(n_pair):
                units.append(_attend(q_ref[u * SW_BLOCK:(u + 1) * SW_BLOCK, p * LANES:(p + 1) * LANES],
                                     [kw, kc_ref[...]], [band2, None], [vw, vc_ref[...]], extra=sink_col(p, SW_BLOCK)))
        half = ATT_IN_FLIGHT // 2
        outs = [o for k in range(0, len(units), half) for o in _round_robin(units[k:k + half])]
        for i, o in enumerate(outs):
            u, p = divmod(i, n_pair)
            store_pair(slice(u * SW_BLOCK, (u + 1) * SW_BLOCK), p, o)

    @pl.when(j == ATT_STEPS)
    def _():
        units = [_attend(q_ref[0:CTX_LEN, p * LANES:(p + 1) * LANES], [kc_ref[...]], [None], [vc_ref[...]],
                         extra=sink_col(p, CTX_LEN)) for p in range(n_pair)]
        for p, o in enumerate(_round_robin(units)):
            store_pair(slice(0, CTX_LEN), p, o)


def _sw_call(at, sink_perm, need_ctx):
    q_blk = 3 * NA_WIDTH // SW_WIDTH
    k_blk = (3 * NA_WIDTH + SW_WIDTH) // SW_KV_WIDTH
    grid_spec = pltpu.PrefetchScalarGridSpec(
        num_scalar_prefetch=1,
        grid=(BATCH, _att_steps(need_ctx)),
        in_specs=[
            pl.BlockSpec((None, TM_ATT, SW_WIDTH), lambda b, j, s: (b, j, q_blk)),
            pl.BlockSpec((None, SEQ, SW_KV_WIDTH), lambda b, j, s: (b, 0, k_blk)),
            pl.BlockSpec((None, SEQ, SW_KV_WIDTH), lambda b, j, s: (b, 0, k_blk + 1)),
            pl.BlockSpec((None, CTX_LEN, SW_KV_WIDTH), lambda b, j, s: (b, SEQ // CTX_LEN, k_blk)),
            pl.BlockSpec((None, CTX_LEN, SW_KV_WIDTH), lambda b, j, s: (b, SEQ // CTX_LEN, k_blk + 1)),
        ],
        out_specs=pl.BlockSpec((None, TM_ATT, SW_WIDTH), lambda b, j, s: (b, j, 0)),
    )
    return pl.pallas_call(
        _sw_kernel,
        grid_spec=grid_spec,
        out_shape=jax.ShapeDtypeStruct((BATCH, SEQ_ALL, SW_WIDTH), BF16),
        compiler_params=_cparams(("parallel", "arbitrary")),
        name="win_attn",
    )(sink_perm, at, at, at, at, at)


N_MIX_IN = 15


def _rows(ref):
    v = ref[...]
    return v.reshape(-1, v.shape[-1])


def _mix_ffn_kernel(x_ref, mod_ref, of_ref, ob_ref, sg_ref, na_ref, sw_ref, w_ref, ng_ref, ones_ref,
                    g_ref, w1_ref, w3_ref, w2_ref, fg_ref, o_ref, *, final):
    o_all = _rows(of_ref) + _rows(ob_ref)
    sg, na, sw = _rows(sg_ref), _rows(na_ref), _rows(sw_ref)

    def mixed(r0):
        rows = slice(r0, r0 + FFN_SLAB)
        o = o_all[rows]
        ms = jnp.concatenate(
            [_dot((o[:, c:c + LANES] * o[:, c:c + LANES]).astype(BF16), ones_ref[...])
             for c in range(0, HG_WIDTH, LANES)], axis=1) * (1.0 / HEAD_DIM)
        hg = (o * lax.rsqrt(ms + EPS)) * ng_ref[...] * sg[rows]
        y = _dot(hg.astype(BF16), w_ref[0:HG_WIDTH, :])
        y = y + _dot(na[rows], w_ref[HG_WIDTH:HG_WIDTH + NA_WIDTH, :])
        y = y + _dot(sw[rows], w_ref[HG_WIDTH + NA_WIDTH:, :])
        return x_ref[rows, :] + mod_ref[5:6, :] * y

    def final_norm(y):
        ms = jnp.mean(y * y, axis=-1, keepdims=True)
        return (y * lax.rsqrt(ms + EPS)) * fg_ref[...]

    o_ref[...] = _swiglu_half_step(mixed, TM_FFN, mod_ref, g_ref, w1_ref, w3_ref, w2_ref, mod0=6,
                                   slab_output=final_norm if final else None)


def _mix_ffn_ctx_kernel(*refs):
    _mix_ffn_kernel(*refs[:N_MIX_IN], refs[-1], final=False)


def _mix_ffn_call(x, mods_all, o_f, o_b, sg, o_na, o_sw, w_out, ng, ones_blk, norm_g, w1, w3, w2, final_g,
                  *, layer, need_ctx):
    n_rows = N_TOK if need_ctx else N_LAT
    mixer_widths = (HG_WIDTH, HG_WIDTH, HG_WIDTH, NA_WIDTH, SW_WIDTH)
    params = [_resident((D_MODEL, D_MODEL), (layer,)), _resident((1, HG_WIDTH), (layer,)), _resident((LANES, LANES)),
              _resident((1, D_MODEL), (layer, 2)), _resident((D_MODEL, D_FF), (layer, 1)),
              _resident((D_MODEL, D_FF), (layer, 1)), _resident((D_FF, D_MODEL), (layer, 1)), _resident((1, D_MODEL))]
    operands = (x, mods_all, o_f, o_b, sg, o_na, o_sw, w_out, ng, ones_blk, norm_g, w1, w3, w2, final_g)
    assert len(operands) == N_MIX_IN
    out_shape = jax.ShapeDtypeStruct((n_rows, D_MODEL), F32)
    lat_tiles = SEQ // TM_FFN

    latent = pl.pallas_call(
        functools.partial(_mix_ffn_kernel, final=not need_ctx),
        grid=(N_LAT // TM_FFN,),
        in_specs=[
            pl.BlockSpec((TM_FFN, D_MODEL), lambda i: (i, 0)),
            pl.BlockSpec((None, None, N_MOD, D_MODEL), lambda i: (layer, i // lat_tiles, 0, 0)),
            *[pl.BlockSpec((None, TM_FFN, c), lambda i: (i // lat_tiles, i % lat_tiles, 0)) for c in mixer_widths],
            *params,
        ],
        out_specs=pl.BlockSpec((TM_FFN, D_MODEL), lambda i: (i, 0)),
        out_shape=out_shape,
        compiler_params=_cparams(("parallel",)),
        name="mix_ffn",
    )(*operands)
    if not need_ctx:
        return latent

    per_tile = TM_FFN // CTX_LEN
    ctx_pos = SEQ // CTX_LEN
    return pl.pallas_call(
        _mix_ffn_ctx_kernel,
        grid=(N_CTX // TM_FFN,),
        in_specs=[
            pl.BlockSpec((TM_FFN, D_MODEL), lambda k: (N_LAT // TM_FFN + k, 0)),
            pl.BlockSpec((None, None, N_MOD, D_MODEL), lambda k: (layer, BATCH, 0, 0)),
            *[pl.BlockSpec((per_tile, CTX_LEN, c), lambda k: (k, ctx_pos, 0)) for c in mixer_widths],
            *params,
            pl.BlockSpec(memory_space=pl.ANY),
        ],
        out_specs=pl.BlockSpec((TM_FFN, D_MODEL), lambda k: (N_LAT // TM_FFN + k, 0)),
        out_shape=out_shape,
        input_output_aliases={N_MIX_IN: 0},
        compiler_params=_cparams(("parallel",)),
        name="mix_ffn_ctx",
    )(*operands, latent)


def _rope_tables():
    pos = jnp.arange(SEQ)
    pos = jnp.stack([pos // GRID_W, pos % GRID_W], axis=-1).astype(F32)
    nf = HEAD_DIM // 4
    inv = ROPE_THETA ** (-jnp.arange(nf, dtype=F32) / nf)
    ang = pos[:, :, None] * inv
    cos, sin = jnp.cos(ang), jnp.sin(ang)
    cos_h = jnp.stack([cos, cos], axis=2).reshape(SEQ, HEAD_DIM)
    sin_h = jnp.stack([-sin, sin], axis=2).reshape(SEQ, HEAD_DIM)
    reps = LANES // HEAD_DIM
    cos_t = jnp.concatenate([jnp.tile(cos_h, (1, reps)), jnp.ones((CTX_LEN, LANES), F32)], axis=0)
    sin_t = jnp.concatenate([jnp.tile(sin_h, (1, reps)), jnp.zeros((CTX_LEN, LANES), F32)], axis=0)
    return cos_t, sin_t


def kernel(x, c, ctx, c_ctx, ada_w, ada_b, norm_g, ffn_w1, ffn_w3, ffn_w2, w_in, w_out,
           hg_lb_logits, hg_norm_g, na_rpb, sw_sink, final_g):
    lb_soft = jax.nn.softmax(hg_lb_logits.astype(F32), axis=0)
    lower_bounds = jnp.cumsum(lb_soft, axis=0) - lb_soft[0]
    w_in_p = w_in.astype(BF16)
    w_out_p = w_out.astype(BF16)
    sink_p = sw_sink.astype(F32)
    w1 = ffn_w1.astype(BF16)
    w3 = ffn_w3.astype(BF16)
    w2 = ffn_w2.astype(BF16)
    cos_t, sin_t = _rope_tables()
    na_bias = _na_bias_table(na_rpb)
    tri, ones_blk = _hgrn_consts()
    final_g2 = final_g.reshape(1, D_MODEL)
    norm_g4 = norm_g.reshape(DEPTH, 3, 1, D_MODEL)
    hg_norm_g3 = hg_norm_g.reshape(DEPTH, 1, HG_WIDTH)

    cond = jnp.concatenate([c, c_ctx[None, :], jnp.zeros((MOD_ROWS - BATCH - 1, D_MODEL), F32)], axis=0)
    mods_all = _ada_call(cond, ada_w, ada_b).reshape(DEPTH, MOD_ROWS, N_MOD, D_MODEL)

    xs = None
    for l in range(DEPTH):
        need_ctx = l < DEPTH - 1
        if l == 0:
            xs = _ffn_call(x.reshape(N_LAT, D_MODEL), ctx.reshape(N_CTX, D_MODEL), 0,
                           mods_all, norm_g4, w1, w3, w2, layer=l)
        else:
            xs = _ffn_call(xs, xs, N_LAT // TM_FFN, mods_all, norm_g4, w1, w3, w2, layer=l)
        qv, ff, fb, sg, at = _inproj_call(xs, mods_all, norm_g4, w_in_p, lower_bounds, cos_t, sin_t, layer=l)
        o_f, o_b = _hgrn_call(qv, ff, fb, tri, ones_blk)
        o_na = _na_call(at, na_bias, layer=l, need_ctx=need_ctx)
        o_sw = _sw_call(at, sink_p[l], need_ctx)
        xs = _mix_ffn_call(xs, mods_all, o_f, o_b, sg, o_na, o_sw, w_out_p, hg_norm_g3, ones_blk,
                           norm_g4, w1, w3, w2, final_g2, layer=l, need_ctx=need_ctx)
    return xs.reshape(BATCH, SEQ, D_MODEL)
```

```python
import functools

import jax
import jax.numpy as jnp
import numpy as np
from jax import lax
from jax.experimental import pallas as pl
from jax.experimental.pallas import tpu as pltpu

F32 = jnp.float32
BF16 = jnp.bfloat16

D_MODEL = 1024
BATCH = 4
SEQ = 4096
DEPTH = 4
GRID_W = 64
CTX_LEN = 256
HEAD_DIM = 64
EPS = 1e-6
MASK_VALUE = -1e30
LOG2_E = 1.4426950408889634
ROPE_THETA = 10000.0
N_MOD = 9
D_FF = 2816
HG_WIDTH = 256
HG_HEADS = 4
NA_WIDTH = 384
NA_HEADS = 6
NA_ROWS = 8
NA_COLS = 16
SW_WIDTH = 384
SW_HEADS = 6
SW_KV_WIDTH = 128
SW_WINDOW = 128
SW_BLOCK = 128
IN_WIDTH = 3072

LANES = 128
SUBLANES = 8
VMEM_LIMIT_BYTES = 56 * 1024 * 1024

N_LAT = BATCH * SEQ
N_CTX = BATCH * CTX_LEN
N_TOK = N_LAT + N_CTX
SEQ_ALL = SEQ + CTX_LEN
TM_PROJ = 256
TM_FFN = 512
FFN_SLAB = 128
TM_LAT = 1024
INPROJ_SLAB = 128
TM_ATT = 1024
ROWS_PER_STEP = TM_ATT // GRID_W
ATT_STEPS = SEQ // TM_ATT
ATT_IN_FLIGHT = 12
HG_CHUNK = 128
HG_STEP_CHUNKS = 2
HG_SUB = 16
N_CHUNK_LAT = SEQ // HG_CHUNK
N_CHUNK_CTX = CTX_LEN // HG_CHUNK
N_CHUNK = N_CHUNK_LAT + N_CHUNK_CTX
NA_KEYS = NA_ROWS * GRID_W
SW_KEYS = 3 * SW_BLOCK


def _cparams(sem):
    return pltpu.CompilerParams(dimension_semantics=sem, vmem_limit_bytes=VMEM_LIMIT_BYTES)


def _silu(a):
    return a * jax.nn.sigmoid(a)


def _dot(a, b):
    return jnp.dot(a, b, preferred_element_type=F32)


def _dot_nt(a, b):
    return lax.dot_general(a, b, (((1,), (1,)), ((), ())), preferred_element_type=F32)


def _modulated_norm(x, g, shift, scale):
    ms = jnp.mean(x * x, axis=-1, keepdims=True)
    return (x * lax.rsqrt(ms + EPS)) * g * (1.0 + scale) + shift


ADA_TN = 2304
MOD_ROWS = SUBLANES


def _ada_kernel(c_ref, w_ref, b_ref, o_ref):
    s = _silu(c_ref[...]).astype(BF16)
    o_ref[...] = _dot(s, w_ref[...].astype(BF16)) + b_ref[...]


def _ada_call(cond, ada_w, ada_b):
    n_out = N_MOD * D_MODEL
    return pl.pallas_call(
        _ada_kernel,
        grid=(DEPTH, n_out // ADA_TN),
        in_specs=[
            pl.BlockSpec((MOD_ROWS, D_MODEL), lambda l, j: (0, 0)),
            pl.BlockSpec((None, D_MODEL, ADA_TN), lambda l, j: (l, 0, j)),
            pl.BlockSpec((None, 1, ADA_TN), lambda l, j: (l, 0, j)),
        ],
        out_specs=pl.BlockSpec((None, MOD_ROWS, ADA_TN), lambda l, j: (l, 0, j)),
        out_shape=jax.ShapeDtypeStruct((DEPTH, MOD_ROWS, n_out), F32),
        compiler_params=_cparams(("parallel", "parallel")),
        name="ada_mod",
    )(cond, ada_w, ada_b.reshape(DEPTH, 1, n_out))


def _swiglu_half_step(slab_input, n_rows, mod_ref, g_ref, w1_ref, w3_ref, w2_ref, *, mod0, slab_output=None):
    def slab(r0):
        xs = slab_input(r0)
        yield
        h = _modulated_norm(xs, g_ref[...], mod_ref[mod0:mod0 + 1, :], mod_ref[mod0 + 1:mod0 + 2, :]).astype(BF16)
        yield
        a1 = _dot(h, w1_ref[...])
        a3 = _dot(h, w3_ref[...])
        yield
        a = (_silu(a1) * a3).astype(BF16)
        yield
        y = xs + (0.5 * mod_ref[mod0 + 2:mod0 + 3, :]) * _dot(a, w2_ref[...])
        return y if slab_output is None else slab_output(y)

    return jnp.concatenate(_round_robin([slab(r0) for r0 in range(0, n_rows, FFN_SLAB)]), axis=0)


def _ffn_kernel(xl_ref, xc_ref, mod_ref, g_ref, w1_ref, w3_ref, w2_ref, o_ref):
    x = jnp.where(pl.program_id(0) < N_LAT // TM_FFN, xl_ref[...], xc_ref[...])
    o_ref[...] = _swiglu_half_step(lambda r0: x[r0:r0 + FFN_SLAB], TM_FFN, mod_ref, g_ref, w1_ref, w3_ref, w2_ref,
                                   mod0=0)


def _resident(shape, lead=()):
    return pl.BlockSpec((None,) * len(lead) + tuple(shape), lambda *_: tuple(lead) + (0,) * len(shape),
                        pipeline_mode=pl.Buffered(1))


def _ffn_call(x_lat, x_ctx, ctx_tile0, mods_all, norm_g, w1, w3, w2, *, layer):
    tiles_per_batch = SEQ // TM_FFN
    lat_tiles = N_LAT // TM_FFN

    def mod_idx(i):
        return (layer, jnp.where(i < BATCH * tiles_per_batch, i // tiles_per_batch, BATCH), 0, 0)

    return pl.pallas_call(
        _ffn_kernel,
        grid=(N_TOK // TM_FFN,),
        in_specs=[
            pl.BlockSpec((TM_FFN, D_MODEL), lambda i: (jnp.minimum(i, lat_tiles - 1), 0)),
            pl.BlockSpec((TM_FFN, D_MODEL), lambda i: (ctx_tile0 + jnp.maximum(i - lat_tiles, 0), 0)),
            pl.BlockSpec((None, None, N_MOD, D_MODEL), mod_idx),
            _resident((1, D_MODEL), (layer, 0)),
            _resident((D_MODEL, D_FF), (layer, 0)),
            _resident((D_MODEL, D_FF), (layer, 0)),
            _resident((D_FF, D_MODEL), (layer, 0)),
        ],
        out_specs=pl.BlockSpec((TM_FFN, D_MODEL), lambda i: (i, 0)),
        out_shape=jax.ShapeDtypeStruct((N_TOK, D_MODEL), F32),
        compiler_params=_cparams(("parallel",)),
        name="ffn",
    )(x_lat, x_ctx, mods_all, norm_g, w1, w3, w2)


def _rope(z, cos, sin_signed, first_of_pair):
    partner = jnp.where(first_of_pair, pltpu.roll(z, LANES - 16, 1), pltpu.roll(z, 16, 1))
    return z * cos + partner * sin_signed


def _inproj_kernel(x_ref, *refs):
    n = min(INPROJ_SLAB, x_ref.shape[0])
    _round_robin([_inproj_rows(pl.ds(r0, n), x_ref, *refs) for r0 in range(0, x_ref.shape[0], n)])


def _inproj_rows(rows, x_ref, mod_ref, g_ref, w_ref, lb_ref, cos_ref, sin_ref,
                 qv_ref, ff_ref, fb_ref, sg_ref, at_ref):
    h = _modulated_norm(x_ref[rows, :], g_ref[...], mod_ref[3:4, :], mod_ref[4:5, :]).astype(BF16)
    W = HG_WIDTH
    p_all = _dot(h, w_ref[...])
    yield

    def proj(lo, hi):
        return p_all[:, lo:hi]

    qv_ref[rows, 0:W] = _silu(proj(0, W))
    for d, dst in ((0, ff_ref), (1, fb_ref)):
        z = proj((1 + d) * W, (2 + d) * W)
        lb = lb_ref[d:d + 1, :]
        dst[rows, 0:W] = jnp.log(lb + (1.0 - lb) * jax.nn.sigmoid(z)) * LOG2_E
        dst[rows, W:2 * W] = jnp.log((1.0 - lb) * jax.nn.sigmoid(-z)) * LOG2_E
    qv_ref[rows, W:2 * W] = proj(3 * W, 4 * W)
    sg_ref[rows, :] = _silu(proj(4 * W, 5 * W))
    yield

    scale = HEAD_DIM ** -0.5
    na0 = 5 * W
    at_ref[rows, 0:NA_WIDTH] = (proj(na0, na0 + NA_WIDTH) * scale).astype(BF16)
    at_ref[rows, NA_WIDTH:3 * NA_WIDTH] = proj(na0 + NA_WIDTH, na0 + 3 * NA_WIDTH).astype(BF16)
    yield
    sw0 = na0 + 3 * NA_WIDTH
    cos = cos_ref[rows, :]
    sin = sin_ref[rows, :]
    lane = lax.broadcasted_iota(jnp.int32, (rows.size, LANES), 1)
    first = (lane % 32) < 16
    z = [_rope(proj(sw0 + j * LANES, sw0 + (j + 1) * LANES), cos, sin, first)
         for j in range((SW_WIDTH + SW_KV_WIDTH) // LANES)]
    low = lane < HEAD_DIM
    z[0], z[1], z[2] = (jnp.where(low, z[0], z[1]), pltpu.roll(jnp.where(low, z[2], z[0]), HEAD_DIM, 1),
                        jnp.where(low, z[1], z[2]))
    for j in range(len(z)):
        zj = z[j] * scale if j < SW_WIDTH // LANES else z[j]
        at_ref[rows, 3 * NA_WIDTH + j * LANES:3 * NA_WIDTH + (j + 1) * LANES] = zj.astype(BF16)
    v0 = sw0 + SW_WIDTH + SW_KV_WIDTH
    at_ref[rows, 3 * NA_WIDTH + SW_WIDTH + SW_KV_WIDTH:] = proj(v0, v0 + SW_KV_WIDTH).astype(BF16)


AT_WIDTH = 3 * NA_WIDTH + SW_WIDTH + 2 * SW_KV_WIDTH


N_INPROJ_IN = 7
INPROJ_WIDTHS = (2 * HG_WIDTH, 2 * HG_WIDTH, 2 * HG_WIDTH, HG_WIDTH, AT_WIDTH)
INPROJ_DTYPES = (F32, F32, F32, F32, BF16)


def _inproj_ctx_kernel(*refs):
    _inproj_kernel(*refs[:N_INPROJ_IN], *refs[N_INPROJ_IN + len(INPROJ_WIDTHS):])


def _inproj_call(x, mods_all, norm_g, w_in, lb, cos_t, sin_t, *, layer):
    widths = INPROJ_WIDTHS
    out_shape = [jax.ShapeDtypeStruct((BATCH, SEQ_ALL, c), dt) for c, dt in zip(widths, INPROJ_DTYPES)]
    params = [_resident((1, D_MODEL), (layer, 1)), _resident((D_MODEL, IN_WIDTH), (layer,)),
              _resident((2, HG_WIDTH), (layer,))]
    lat_tiles = SEQ // TM_LAT

    latent = pl.pallas_call(
        _inproj_kernel,
        grid=(N_LAT // TM_LAT,),
        in_specs=[
            pl.BlockSpec((TM_LAT, D_MODEL), lambda i: (i, 0)),
            pl.BlockSpec((None, None, N_MOD, D_MODEL), lambda i: (layer, i // lat_tiles, 0, 0)),
            *params,
            pl.BlockSpec((TM_LAT, LANES), lambda i: (i % lat_tiles, 0)),
            pl.BlockSpec((TM_LAT, LANES), lambda i: (i % lat_tiles, 0)),
        ],
        out_specs=[pl.BlockSpec((None, TM_LAT, c), lambda i: (i // lat_tiles, i % lat_tiles, 0)) for c in widths],
        out_shape=out_shape,
        compiler_params=_cparams(("parallel",)),
        name="in_proj",
    )(x, mods_all, norm_g, w_in, lb, cos_t, sin_t)

    ctx_pos = SEQ // TM_PROJ
    return pl.pallas_call(
        _inproj_ctx_kernel,
        grid=(BATCH,),
        in_specs=[
            pl.BlockSpec((TM_PROJ, D_MODEL), lambda b: (N_LAT // TM_PROJ + b, 0)),
            pl.BlockSpec((None, None, N_MOD, D_MODEL), lambda b: (layer, BATCH, 0, 0)),
            *params,
            pl.BlockSpec((TM_PROJ, LANES), lambda b: (ctx_pos, 0)),
            pl.BlockSpec((TM_PROJ, LANES), lambda b: (ctx_pos, 0)),
            *[pl.BlockSpec(memory_space=pl.ANY) for _ in widths],
        ],
        out_specs=[pl.BlockSpec((None, TM_PROJ, c), lambda b: (b, ctx_pos, 0)) for c in widths],
        out_shape=out_shape,
        input_output_aliases={N_INPROJ_IN + k: k for k in range(len(widths))},
        compiler_params=_cparams(("parallel",)),
        name="in_proj_ctx",
    )(x, mods_all, norm_g, w_in, lb, cos_t, sin_t, *latent)


N_SUB = HG_CHUNK // HG_SUB
N_PAIR = HG_HEADS // 2
HG_MINI = HG_SUB // 2


def _hgrn_consts():
    t = np.arange(HG_CHUNK)
    same = (t[:, None] // HG_SUB) == (t[None, :] // HG_SUB)
    lower = same & (t[None, :] <= t[:, None])
    upper = same & (t[None, :] >= t[:, None])
    tri = np.stack([lower, upper]).astype(np.float32)
    d = np.arange(LANES)
    head_blocks = ((d[:, None] // HEAD_DIM) == (d[None, :] // HEAD_DIM)).astype(np.float32)
    return jnp.asarray(tri, BF16), jnp.asarray(head_blocks, BF16)


def _split2(a):
    hi = a.astype(BF16)
    lo = (a - hi.astype(F32)).astype(BF16)
    return jnp.concatenate([hi, lo], axis=1)


def _bcast_rows(a, s, block):
    n = HG_CHUNK // block
    a3 = a.reshape(n, block, LANES)
    return jnp.broadcast_to(a3[:, s:s + 1, :], (n, block, LANES)).reshape(HG_CHUNK, LANES)


def _hgrn_group(qs, v, gl, lk, tri, ones_blk, st, direction):
    fwd = direction == 0
    c = _dot(tri, _split2(gl))
    yield
    cum = c[:, 0:LANES] + c[:, LANES:]
    tot = _bcast_rows(cum, HG_SUB - 1 if fwd else 0, HG_SUB)
    row = lax.broadcasted_iota(jnp.int32, (HG_CHUNK, LANES), 0)
    t_mini = row % HG_MINI
    blk = row // HG_SUB
    v16 = v.astype(BF16)
    ck = cum - lk

    o = jnp.zeros((HG_CHUNK, LANES), F32)
    ones2 = jnp.concatenate([jnp.concatenate([ones_blk, jnp.zeros_like(ones_blk)], axis=1),
                             jnp.concatenate([jnp.zeros_like(ones_blk), ones_blk], axis=1)], axis=0)
    for s0 in range(0, HG_MINI, 2):
        w = []
        for s in (s0, s0 + 1):
            keep = (t_mini >= s) if fwd else (t_mini <= s)
            w.append(jnp.where(keep, qs * jnp.exp2(cum - _bcast_rows(ck, s, HG_MINI)), 0.0).astype(BF16))
        r = _dot(jnp.concatenate(w, axis=1), ones2)
        o = o + r[:, :LANES] * _bcast_rows(v, s0, HG_MINI) + r[:, LANES:] * _bcast_rows(v, s0 + 1, HG_MINI)
        yield

    later = ((row % HG_SUB) >= HG_MINI) if fwd else ((row % HG_SUB) < HG_MINI)
    edge = _bcast_rows(cum, HG_MINI - 1 if fwd else HG_MINI, HG_SUB)
    q_edge = jnp.where(later, qs * jnp.exp2(jnp.minimum(cum - edge, 0.0)), 0.0)
    k_edge = jnp.where(later, 0.0, jnp.exp2(jnp.minimum(edge - ck, 0.0)))
    a = _dot_nt(_pair_queries(q_edge.astype(BF16)), k_edge.astype(BF16))
    q_blk = lax.broadcasted_iota(jnp.int32, (2 * HG_CHUNK, HG_CHUNK), 0) % HG_CHUNK // HG_SUB
    k_blk = lax.broadcasted_iota(jnp.int32, (2 * HG_CHUNK, HG_CHUNK), 1) // HG_SUB
    yield
    a = jnp.where(q_blk == k_blk, a, 0.0)
    o = o + _pair_merge(_dot(a.astype(BF16), v16))
    yield

    qd = qs * jnp.exp2(cum)
    kd = jnp.exp2(tot - ck)
    dec = jnp.exp2(tot)
    k_exp = jnp.concatenate([jnp.where(blk == j, kd, 0.0).astype(BF16) for j in range(N_SUB)], axis=1)
    upd = _dot(v.T.astype(BF16), k_exp)
    yield
    head_mask = ones_blk.astype(F32)
    before = [None] * N_SUB
    for j in (range(N_SUB) if fwd else range(N_SUB - 1, -1, -1)):
        before[j] = st.astype(BF16)
        st = st * dec[j * HG_SUB:j * HG_SUB + 1, :] + upd[:, j * LANES:(j + 1) * LANES] * head_mask
    yield
    q_exp = jnp.concatenate([jnp.where(blk == j, qd, 0.0).astype(BF16) for j in range(N_SUB)], axis=1)
    o = o + _dot_nt(q_exp, jnp.concatenate(before, axis=1))
    return o, st


def _round_robin(generators):
    results = [None] * len(generators)
    live = list(range(len(generators)))
    while live:
        for k in list(live):
            try:
                next(generators[k])
            except StopIteration as done:
                results[k] = done.value
                live.remove(k)
    return results


def _hgrn_kernel(qvf_ref, ff_ref, qvb_ref, fb_ref, tri_ref, ones_ref, of_ref, ob_ref, st_scr):
    @pl.when(pl.program_id(0) == 0)
    def _():
        st_scr[...] = jnp.zeros_like(st_scr)

    ones_blk = ones_ref[...]
    W = HG_WIDTH
    keys = [(b, direction, hp) for b in range(BATCH) for direction in range(2) for hp in range(N_PAIR)]
    for step in range(HG_STEP_CHUNKS):
        chunk = (step, HG_STEP_CHUNKS - 1 - step)
        chains = []
        for b, direction, hp in keys:
            qv_ref, f_ref = ((qvf_ref, ff_ref), (qvb_ref, fb_ref))[direction]
            ch = chunk[direction]
            c0 = hp * LANES
            chains.append(_hgrn_group(
                qv_ref[b, ch, :, c0:c0 + LANES], qv_ref[b, ch, :, W + c0:W + c0 + LANES],
                f_ref[b, ch, :, c0:c0 + LANES], f_ref[b, ch, :, W + c0:W + c0 + LANES],
                tri_ref[direction], ones_blk, st_scr[b, direction * N_PAIR + hp], direction))
        for (b, direction, hp), (o, st) in zip(keys, _round_robin(chains)):
            (of_ref, ob_ref)[direction][b, chunk[direction], :, hp * LANES:(hp + 1) * LANES] = o
            st_scr[b, direction * N_PAIR + hp] = st


def _hgrn_call(qv, ff, fb, tri, ones_blk):
    def chunked(a):
        return a.reshape(BATCH, N_CHUNK, HG_CHUNK, a.shape[-1])

    n_steps = N_CHUNK // HG_STEP_CHUNKS
    ctx_steps = N_CHUNK_CTX // HG_STEP_CHUNKS

    def fwd_idx(s):
        return jnp.where(s < ctx_steps, n_steps - ctx_steps + s, s - ctx_steps)

    def bwd_idx(s):
        return n_steps - 1 - s

    def spec(c, idx):
        return pl.BlockSpec((BATCH, HG_STEP_CHUNKS, HG_CHUNK, c), lambda s: (0, idx(s), 0, 0))

    out_sds = jax.ShapeDtypeStruct((BATCH, N_CHUNK, HG_CHUNK, HG_WIDTH), F32)
    o_f, o_b = pl.pallas_call(
        _hgrn_kernel,
        grid=(n_steps,),
        in_specs=[
            spec(2 * HG_WIDTH, fwd_idx), spec(2 * HG_WIDTH, fwd_idx),
            spec(2 * HG_WIDTH, bwd_idx), spec(2 * HG_WIDTH, bwd_idx),
            pl.BlockSpec((2, HG_CHUNK, HG_CHUNK), lambda s: (0, 0, 0)),
            pl.BlockSpec((LANES, LANES), lambda s: (0, 0)),
        ],
        out_specs=[spec(HG_WIDTH, fwd_idx), spec(HG_WIDTH, bwd_idx)],
        out_shape=[out_sds, out_sds],
        scratch_shapes=[pltpu.VMEM((BATCH, 2 * N_PAIR, LANES, LANES), F32)],
        compiler_params=_cparams(("arbitrary",)),
        name="hgrn2",
    )(chunked(qv), chunked(ff), chunked(qv), chunked(fb), tri, ones_blk)
    return o_f.reshape(BATCH, SEQ_ALL, HG_WIDTH), o_b.reshape(BATCH, SEQ_ALL, HG_WIDTH)


def _pair_queries(q):
    lane = lax.broadcasted_iota(jnp.int32, q.shape, 1)
    zero = jnp.zeros_like(q)
    return jnp.concatenate([jnp.where(lane < HEAD_DIM, q, zero), jnp.where(lane >= HEAD_DIM, q, zero)], axis=0)


def _pair_merge(o):
    m = o.shape[0] // 2
    lane = lax.broadcasted_iota(jnp.int32, (m, LANES), 1)
    return jnp.where(lane < HEAD_DIM, o[0:m], o[m:])


def _attend(q, keys, biases, values, extra=None):
    def lane_chunks(s):
        return [s[:, c:c + LANES] for c in range(0, s.shape[1], LANES)]

    q2 = _pair_queries(q)
    scores = []
    for k, b in zip(keys, biases):
        s = _dot_nt(q2, k)
        scores.append(s if b is None else s + b)
        yield
    m = functools.reduce(jnp.maximum, [c for s in scores for c in lane_chunks(s)])
    m = jnp.max(m, axis=-1, keepdims=True)
    if extra is not None:
        m = jnp.maximum(m, extra)
    yield
    acc = None
    for s, v in zip(scores, values):
        e = jnp.exp((s - m).astype(BF16))
        pv = _dot(e, jnp.concatenate([v, jnp.ones_like(v)], axis=1))
        acc = pv if acc is None else acc + pv
        yield
    denom = acc[:, LANES:]
    if extra is not None:
        denom = denom + jnp.exp(extra - m)
    return _pair_merge(acc[:, :LANES] / denom)


def _na_kernel(q_ref, k_ref, v_ref, kc_ref, vc_ref, bias_ref, o_ref):
    j = pl.program_id(1)

    @pl.when(j < ATT_STEPS)
    def _():
        units = []
        for rr in range(ROWS_PER_STEP):
            r = j * ROWS_PER_STEP + rr
            start = jnp.clip(r - NA_ROWS // 2, 0, GRID_W - NA_ROWS)
            k0 = pl.multiple_of(start * GRID_W, GRID_W)
            t0 = start - r + (NA_ROWS - 1)
            for p in range(NA_HEADS // 2):
                c = slice(p * LANES, (p + 1) * LANES)
                bias = jnp.concatenate(
                    [jnp.concatenate([bias_ref[h, t0 + i] for i in range(0, NA_ROWS, 2)], axis=1)
                     for h in (2 * p, 2 * p + 1)], axis=0)
                units.append(_attend(q_ref[rr * GRID_W:(rr + 1) * GRID_W, c],
                                     [k_ref[pl.ds(k0, NA_KEYS), c], kc_ref[:, c]], [bias, None],
                                     [v_ref[pl.ds(k0, NA_KEYS), c], vc_ref[:, c]]))
        outs = [o for k in range(0, len(units), ATT_IN_FLIGHT) for o in _round_robin(units[k:k + ATT_IN_FLIGHT])]
        for i, o in enumerate(outs):
            rr, p = divmod(i, NA_HEADS // 2)
            o_ref[rr * GRID_W:(rr + 1) * GRID_W, p * LANES:(p + 1) * LANES] = o.astype(BF16)

    @pl.when(j == ATT_STEPS)
    def _():
        units = [_attend(q_ref[0:CTX_LEN, p * LANES:(p + 1) * LANES], [kc_ref[:, p * LANES:(p + 1) * LANES]], [None],
                         [vc_ref[:, p * LANES:(p + 1) * LANES]]) for p in range(NA_HEADS // 2)]
        for p, o in enumerate(_round_robin(units)):
            o_ref[0:CTX_LEN, p * LANES:(p + 1) * LANES] = o.astype(BF16)


def _att_steps(need_ctx):
    return ATT_STEPS + 1 if need_ctx else ATT_STEPS


def _na_call(at, bias, *, layer, need_ctx):
    w = NA_WIDTH
    return pl.pallas_call(
        _na_kernel,
        grid=(BATCH, _att_steps(need_ctx)),
        in_specs=[
            pl.BlockSpec((None, TM_ATT, w), lambda b, j: (b, j, 0)),
            pl.BlockSpec((None, SEQ, w), lambda b, j: (b, 0, 1)),
            pl.BlockSpec((None, SEQ, w), lambda b, j: (b, 0, 2)),
            pl.BlockSpec((None, CTX_LEN, w), lambda b, j: (b, SEQ // CTX_LEN, 1)),
            pl.BlockSpec((None, CTX_LEN, w), lambda b, j: (b, SEQ // CTX_LEN, 2)),
            _resident((NA_HEADS, 2 * NA_ROWS - 2, GRID_W, 2 * GRID_W), (layer,)),
        ],
        out_specs=pl.BlockSpec((None, TM_ATT, w), lambda b, j: (b, j, 0)),
        out_shape=jax.ShapeDtypeStruct((BATCH, SEQ_ALL, w), BF16),
        compiler_params=_cparams(("parallel", "arbitrary")),
        name="nbr_attn",
    )(at, at, at, at, at, bias)


def _na_bias_table(rpb):
    col = np.arange(GRID_W)
    c0 = np.clip(col - NA_COLS // 2, 0, GRID_W - NA_COLS)
    col_ok = (col[None, :] >= c0[:, None]) & (col[None, :] < c0[:, None] + NA_COLS)
    d_col = np.clip(col[None, :] - col[:, None], 1 - NA_COLS, NA_COLS - 1)
    col_sel = (d_col[None, :, :] + NA_COLS - 1 == np.arange(2 * NA_COLS - 1)[:, None, None]).astype(np.float32)
    b = jnp.einsum('lhrd,dqk->lhrqk', rpb.astype(F32), col_sel, precision=lax.Precision.HIGHEST)
    b = jnp.where(col_ok, b, MASK_VALUE)
    return jnp.concatenate([b[:, :, :-1], b[:, :, 1:]], axis=-1)


def _sw_kernel(sink_ref, q_ref, k_ref, v_ref, kc_ref, vc_ref, o_ref):
    j = pl.program_id(1)
    n_pair = SW_HEADS // 2

    def sink_col(p, m):
        row = lax.broadcasted_iota(jnp.int32, (2 * m, 1), 0)
        return jnp.where(row < m, sink_ref[p], sink_ref[p + n_pair])

    def store_pair(rows, p, o):
        ob = o.astype(BF16)
        o_ref[rows, p * HEAD_DIM:(p + 1) * HEAD_DIM] = ob[:, :HEAD_DIM]
        o_ref[rows, (p + n_pair) * HEAD_DIM:(p + n_pair + 1) * HEAD_DIM] = ob[:, HEAD_DIM:]

    @pl.when(j < ATT_STEPS)
    def _():
        units = []
        for u in range(TM_ATT // SW_BLOCK):
            n = j * (TM_ATT // SW_BLOCK) + u
            start = jnp.clip(n * SW_BLOCK - SW_BLOCK, 0, SEQ - SW_KEYS)
            k0 = pl.multiple_of(start, SW_BLOCK)
            rel = (n * SW_BLOCK - start
                   + lax.broadcasted_iota(jnp.int32, (SW_BLOCK, SW_KEYS), 0)
                   - lax.broadcasted_iota(jnp.int32, (SW_BLOCK, SW_KEYS), 1))
            band = jnp.where(jnp.abs(rel) <= SW_WINDOW, 0.0, MASK_VALUE).astype(F32)
            band2 = jnp.concatenate([band, band], axis=0)
            kw = k_ref[pl.ds(k0, SW_KEYS), :]
            vw = v_ref[pl.ds(k0, SW_KEYS), :]
            for p in range(n_pair):
                units.append(_attend(q_ref[u * SW_BLOCK:(u + 1) * SW_BLOCK, p * LANES:(p + 1) * LANES],
                                     [kw, kc_ref[...]], [band2, None], [vw, vc_ref[...]], extra=sink_col(p, SW_BLOCK)))
        half = ATT_IN_FLIGHT // 2
        outs = [o for k in range(0, len(units), half) for o in _round_robin(units[k:k + half])]
        for i, o in enumerate(outs):
            u, p = divmod(i, n_pair)
            store_pair(slice(u * SW_BLOCK, (u + 1) * SW_BLOCK), p, o)

    @pl.when(j == ATT_STEPS)
    def _():
        units = [_attend(q_ref[0:CTX_LEN, p * LANES:(p + 1) * LANES], [kc_ref[...]], [None], [vc_ref[...]],
                         extra=sink_col(p, CTX_LEN)) for p in range(n_pair)]
        for p, o in enumerate(_round_robin(units)):
            store_pair(slice(0, CTX_LEN), p, o)


def _sw_call(at, sink_perm, need_ctx):
    q_blk = 3 * NA_WIDTH // SW_WIDTH
    k_blk = (3 * NA_WIDTH + SW_WIDTH) // SW_KV_WIDTH
    grid_spec = pltpu.PrefetchScalarGridSpec(
        num_scalar_prefetch=1,
        grid=(BATCH, _att_steps(need_ctx)),
        in_specs=[
            pl.BlockSpec((None, TM_ATT, SW_WIDTH), lambda b, j, s: (b, j, q_blk)),
            pl.BlockSpec((None, SEQ, SW_KV_WIDTH), lambda b, j, s: (b, 0, k_blk)),
            pl.BlockSpec((None, SEQ, SW_KV_WIDTH), lambda b, j, s: (b, 0, k_blk + 1)),
            pl.BlockSpec((None, CTX_LEN, SW_KV_WIDTH), lambda b, j, s: (b, SEQ // CTX_LEN, k_blk)),
            pl.BlockSpec((None, CTX_LEN, SW_KV_WIDTH), lambda b, j, s: (b, SEQ // CTX_LEN, k_blk + 1)),
        ],
        out_specs=pl.BlockSpec((None, TM_ATT, SW_WIDTH), lambda b, j, s: (b, j, 0)),
    )
    return pl.pallas_call(
        _sw_kernel,
        grid_spec=grid_spec,
        out_shape=jax.ShapeDtypeStruct((BATCH, SEQ_ALL, SW_WIDTH), BF16),
        compiler_params=_cparams(("parallel", "arbitrary")),
        name="win_attn",
    )(sink_perm, at, at, at, at, at)


N_MIX_IN = 15


def _rows(ref):
    v = ref[...]
    return v.reshape(-1, v.shape[-1])


def _mix_ffn_kernel(x_ref, mod_ref, of_ref, ob_ref, sg_ref, na_ref, sw_ref, w_ref, ng_ref, ones_ref,
                    g_ref, w1_ref, w3_ref, w2_ref, fg_ref, o_ref, *, final):
    o_all = _rows(of_ref) + _rows(ob_ref)
    sg, na, sw = _rows(sg_ref), _rows(na_ref), _rows(sw_ref)

    def mixed(r0):
        rows = slice(r0, r0 + FFN_SLAB)
        o = o_all[rows]
        ms = jnp.concatenate(
            [_dot((o[:, c:c + LANES] * o[:, c:c + LANES]).astype(BF16), ones_ref[...])
             for c in range(0, HG_WIDTH, LANES)], axis=1) * (1.0 / HEAD_DIM)
        hg = (o * lax.rsqrt(ms + EPS)) * ng_ref[...] * sg[rows]
        y = _dot(hg.astype(BF16), w_ref[0:HG_WIDTH, :])
        y = y + _dot(na[rows], w_ref[HG_WIDTH:HG_WIDTH + NA_WIDTH, :])
        y = y + _dot(sw[rows], w_ref[HG_WIDTH + NA_WIDTH:, :])
        return x_ref[rows, :] + mod_ref[5:6, :] * y

    def final_norm(y):
        ms = jnp.mean(y * y, axis=-1, keepdims=True)
        return (y * lax.rsqrt(ms + EPS)) * fg_ref[...]

    o_ref[...] = _swiglu_half_step(mixed, TM_FFN, mod_ref, g_ref, w1_ref, w3_ref, w2_ref, mod0=6,
                                   slab_output=final_norm if final else None)


def _mix_ffn_ctx_kernel(*refs):
    _mix_ffn_kernel(*refs[:N_MIX_IN], refs[-1], final=False)


def _mix_ffn_call(x, mods_all, o_f, o_b, sg, o_na, o_sw, w_out, ng, ones_blk, norm_g, w1, w3, w2, final_g,
                  *, layer, need_ctx):
    n_rows = N_TOK if need_ctx else N_LAT
    mixer_widths = (HG_WIDTH, HG_WIDTH, HG_WIDTH, NA_WIDTH, SW_WIDTH)
    params = [_resident((D_MODEL, D_MODEL), (layer,)), _resident((1, HG_WIDTH), (layer,)), _resident((LANES, LANES)),
              _resident((1, D_MODEL), (layer, 2)), _resident((D_MODEL, D_FF), (layer, 1)),
              _resident((D_MODEL, D_FF), (layer, 1)), _resident((D_FF, D_MODEL), (layer, 1)), _resident((1, D_MODEL))]
    operands = (x, mods_all, o_f, o_b, sg, o_na, o_sw, w_out, ng, ones_blk, norm_g, w1, w3, w2, final_g)
    assert len(operands) == N_MIX_IN
    out_shape = jax.ShapeDtypeStruct((n_rows, D_MODEL), F32)
    lat_tiles = SEQ // TM_FFN

    latent = pl.pallas_call(
        functools.partial(_mix_ffn_kernel, final=not need_ctx),
        grid=(N_LAT // TM_FFN,),
        in_specs=[
            pl.BlockSpec((TM_FFN, D_MODEL), lambda i: (i, 0)),
            pl.BlockSpec((None, None, N_MOD, D_MODEL), lambda i: (layer, i // lat_tiles, 0, 0)),
            *[pl.BlockSpec((None, TM_FFN, c), lambda i: (i // lat_tiles, i % lat_tiles, 0)) for c in mixer_widths],
            *params,
        ],
        out_specs=pl.BlockSpec((TM_FFN, D_MODEL), lambda i: (i, 0)),
        out_shape=out_shape,
        compiler_params=_cparams(("parallel",)),
        name="mix_ffn",
    )(*operands)
    if not need_ctx:
        return latent

    per_tile = TM_FFN // CTX_LEN
    ctx_pos = SEQ // CTX_LEN
    return pl.pallas_call(
        _mix_ffn_ctx_kernel,
        grid=(N_CTX // TM_FFN,),
        in_specs=[
            pl.BlockSpec((TM_FFN, D_MODEL), lambda k: (N_LAT // TM_FFN + k, 0)),
            pl.BlockSpec((None, None, N_MOD, D_MODEL), lambda k: (layer, BATCH, 0, 0)),
            *[pl.BlockSpec((per_tile, CTX_LEN, c), lambda k: (k, ctx_pos, 0)) for c in mixer_widths],
            *params,
            pl.BlockSpec(memory_space=pl.ANY),
        ],
        out_specs=pl.BlockSpec((TM_FFN, D_MODEL), lambda k: (N_LAT // TM_FFN + k, 0)),
        out_shape=out_shape,
        input_output_aliases={N_MIX_IN: 0},
        compiler_params=_cparams(("parallel",)),
        name="mix_ffn_ctx",
    )(*operands, latent)


def _rope_tables():
    pos = jnp.arange(SEQ)
    pos = jnp.stack([pos // GRID_W, pos % GRID_W], axis=-1).astype(F32)
    nf = HEAD_DIM // 4
    inv = ROPE_THETA ** (-jnp.arange(nf, dtype=F32) / nf)
    ang = pos[:, :, None] * inv
    cos, sin = jnp.cos(ang), jnp.sin(ang)
    cos_h = jnp.stack([cos, cos], axis=2).reshape(SEQ, HEAD_DIM)
    sin_h = jnp.stack([-sin, sin], axis=2).reshape(SEQ, HEAD_DIM)
    reps = LANES // HEAD_DIM
    cos_t = jnp.concatenate([jnp.tile(cos_h, (1, reps)), jnp.ones((CTX_LEN, LANES), F32)], axis=0)
    sin_t = jnp.concatenate([jnp.tile(sin_h, (1, reps)), jnp.zeros((CTX_LEN, LANES), F32)], axis=0)
    return cos_t, sin_t


def kernel(x, c, ctx, c_ctx, ada_w, ada_b, norm_g, ffn_w1, ffn_w3, ffn_w2, w_in, w_out,
           hg_lb_logits, hg_norm_g, na_rpb, sw_sink, final_g):
    lb_soft = jax.nn.softmax(hg_lb_logits.astype(F32), axis=0)
    lower_bounds = jnp.cumsum(lb_soft, axis=0) - lb_soft[0]
    w_in_p = w_in.astype(BF16)
    w_out_p = w_out.astype(BF16)
    sink_p = sw_sink.astype(F32)
    w1 = ffn_w1.astype(BF16)
    w3 = ffn_w3.astype(BF16)
    w2 = ffn_w2.astype(BF16)
    cos_t, sin_t = _rope_tables()
    na_bias = _na_bias_table(na_rpb)
    tri, ones_blk = _hgrn_consts()
    final_g2 = final_g.reshape(1, D_MODEL)
    norm_g4 = norm_g.reshape(DEPTH, 3, 1, D_MODEL)
    hg_norm_g3 = hg_norm_g.reshape(DEPTH, 1, HG_WIDTH)

    cond = jnp.concatenate([c, c_ctx[None, :], jnp.zeros((MOD_ROWS - BATCH - 1, D_MODEL), F32)], axis=0)
    mods_all = _ada_call(cond, ada_w, ada_b).reshape(DEPTH, MOD_ROWS, N_MOD, D_MODEL)

    xs = None
    for l in range(DEPTH):
        need_ctx = l < DEPTH - 1
        if l == 0:
            xs = _ffn_call(x.reshape(N_LAT, D_MODEL), ctx.reshape(N_CTX, D_MODEL), 0,
                           mods_all, norm_g4, w1, w3, w2, layer=l)
        else:
            xs = _ffn_call(xs, xs, N_LAT // TM_FFN, mods_all, norm_g4, w1, w3, w2, layer=l)
        qv, ff, fb, sg, at = _inproj_call(xs, mods_all, norm_g4, w_in_p, lower_bounds, cos_t, sin_t, layer=l)
        o_f, o_b = _hgrn_call(qv, ff, fb, tri, ones_blk)
        o_na = _na_call(at, na_bias, layer=l, need_ctx=need_ctx)
        o_sw = _sw_call(at, sink_p[l], need_ctx)
        xs = _mix_ffn_call(xs, mods_all, o_f, o_b, sg, o_na, o_sw, w_out_p, hg_norm_g3, ones_blk,
                           norm_g4, w1, w3, w2, final_g2, layer=l, need_ctx=need_ctx)
    return xs.reshape(BATCH, SEQ, D_MODEL)
```
